```python
import math
import jax, jax.numpy as jnp
from jax import lax
import numpy as np

D_MODEL = 2048
BATCH = 2
SEQ = 16384
DEPTH = 1
DEC_BATCH = 8
DEC_SEQ = 16
PAST_LEN = 4096

CHUNK = 64
EPS = 1e-6
SSD_EXPAND = 2
SSD_D_INNER = SSD_EXPAND * D_MODEL
SSD_HEAD_DIM = 64
SSD_N_HEADS = SSD_D_INNER // SSD_HEAD_DIM
SSD_N_GROUPS = 8
SSD_D_STATE = 128
SSD_CONV = 4
SSD_CONV_DIM = SSD_D_INNER + 2 * SSD_N_GROUPS * SSD_D_STATE
SWA_N_HEADS = 32
SWA_N_KV = 4
SWA_HEAD_DIM = 64
WINDOW = 128
WINDOW_CHUNKS = WINDOW // CHUNK
SWA_SCALE = SWA_HEAD_DIM ** -0.5
MEM_LEN = 256
MEM_N_HEADS = 4
MEM_HEAD_DIM = 512
MEM_SCALE = MEM_HEAD_DIM ** -0.5
N_BUCKETS = 32
MAX_DISTANCE = 128
N_EXPERT_GROUPS = 4
EXPERTS_PER_GROUP = 16
N_EXPERTS = N_EXPERT_GROUPS * EXPERTS_PER_GROUP
TOP_K_IN_GROUP = 2
D_FF_EXPERT = 512
MOE_BLOCK = 128
IN_SIZES = (SSD_D_INNER, SSD_CONV_DIM, SSD_N_HEADS, SWA_N_HEADS * SWA_HEAD_DIM,
            SWA_N_KV * SWA_HEAD_DIM, SWA_N_KV * SWA_HEAD_DIM, MEM_N_HEADS * MEM_HEAD_DIM, 3 * D_MODEL)
IN_COLS = sum(IN_SIZES)

kernel_name = 'hybrid_stream_ssd_swa_mem_hmoe_step'

F32 = jnp.float32


def _rmsnorm(x, g):
    xf = x.astype(F32)
    y = xf * lax.rsqrt(jnp.mean(xf * xf, axis=-1, keepdims=True) + EPS)
    return (y * g.astype(F32)).astype(x.dtype)


def _split_cols(w):
    out, start = [], 0
    for s in IN_SIZES:
        out.append(w[..., start:start + s])
        start += s
    return out


def _t5_bucket(rel):
    nb = N_BUCKETS // 2
    max_exact = nb // 2
    ret = jnp.where(rel > 0, nb, 0)
    n = jnp.abs(rel)
    nf = jnp.maximum(n, 1).astype(F32)
    large = max_exact + (jnp.log(nf / max_exact) / math.log(MAX_DISTANCE / max_exact)
                         * (nb - max_exact)).astype(jnp.int32)
    large = jnp.minimum(large, nb - 1)
    return ret + jnp.where(n < max_exact, n, large)


def _rel_bias(rel, table):
    b = jnp.moveaxis(table[_t5_bucket(rel)].astype(F32), -1, 0)
    return b.reshape(SWA_N_KV, SWA_N_HEADS // SWA_N_KV, rel.shape[0], rel.shape[1])


def _sink_softmax(logits, sink):
    m = jnp.maximum(jnp.max(logits, axis=-1, keepdims=True), sink)
    p = jnp.exp(logits - m)
    den = jnp.sum(p, axis=-1, keepdims=True) + jnp.exp(sink - m)
    return p / den


def _ssd_scan(x, dt, a, b, c, h0, chunk):
    Bt, L, H, P = x.shape
    G, N = b.shape[2], b.shape[3]
    R = H // G
    nc = L // chunk
    xdt = (x * dt[..., None]).reshape(Bt, nc, chunk, G, R, P)
    adt = (dt * a).reshape(Bt, nc, chunk, G, R)
    bb = b.reshape(Bt, nc, chunk, G, N)
    cc = c.reshape(Bt, nc, chunk, G, N)
    causal = jnp.tril(jnp.ones((chunk, chunk), dtype=bool))[None, :, :, None, None]

    def step(h, inp):
        xc, ac, bc, ccur = inp
        acs = jnp.cumsum(ac, axis=1)
        decay = jnp.exp(jnp.where(causal, acs[:, :, None] - acs[:, None, :], -jnp.inf))
        cb = jnp.einsum('blgn,bsgn->blsg', ccur, bc)
        y = jnp.einsum('blsg,blsgr,bsgrp->blgrp', cb, decay, xc)
        y = y + jnp.einsum('blgn,bgrpn->blgrp', ccur, h) * jnp.exp(acs)[..., None]
        a_end = acs[:, -1]
        h = h * jnp.exp(a_end)[..., None, None] + jnp.einsum(
            'bsgn,bsgr,bsgrp->bgrpn', bc, jnp.exp(a_end[:, None] - acs), xc)
        return h, y

    xs = tuple(jnp.moveaxis(t, 1, 0) for t in (xdt, adt, bb, cc))
    h_t, ys = lax.scan(step, h0.reshape(Bt, G, R, P, N), xs)
    y = jnp.moveaxis(ys, 0, 1).reshape(Bt, L, H, P)
    return y, h_t.reshape(Bt, H, P, N)


def _ssd_branch(z, xbc, dt_raw, conv_prev, h0, conv_w, conv_b, dt_bias, a_log, d_skip, g_norm, w_o, chunk):
    Bt, L, _ = xbc.shape
    xpad = jnp.concatenate([conv_prev.astype(xbc.dtype), xbc], axis=1)
    conv = lax.conv_general_dilated(xpad, conv_w[:, None, :].astype(xbc.dtype), (1,), 'VALID',
                                    dimension_numbers=('NWC', 'WIO', 'NWC'),
                                    feature_group_count=SSD_CONV_DIM)
    xbc_c = jax.nn.silu(conv + conv_b).astype(F32)
    new_conv = xpad[:, -(SSD_CONV - 1):]
    gn = SSD_N_GROUPS * SSD_D_STATE
    xs = xbc_c[..., :SSD_D_INNER].reshape(Bt, L, SSD_N_HEADS, SSD_HEAD_DIM)
    bs = xbc_c[..., SSD_D_INNER:SSD_D_INNER + gn].reshape(Bt, L, SSD_N_GROUPS, SSD_D_STATE)
    cs = xbc_c[..., SSD_D_INNER + gn:].reshape(Bt, L, SSD_N_GROUPS, SSD_D_STATE)
    dt = jax.nn.softplus(dt_raw.astype(F32) + dt_bias.astype(F32))
    a = -jnp.exp(a_log.astype(F32))
    y, h_new = _ssd_scan(xs, dt, a, bs, cs, h0.astype(F32), chunk)
    y = y + d_skip.astype(F32)[:, None] * xs
    y = y.reshape(Bt, L, SSD_D_INNER) * jax.nn.silu(z.astype(F32))
    yg = y.reshape(Bt, L, SSD_N_GROUPS, SSD_D_INNER // SSD_N_GROUPS)
    yg = yg * lax.rsqrt(jnp.mean(yg * yg, axis=-1, keepdims=True) + EPS)
    y = (yg.reshape(Bt, L, SSD_D_INNER) * g_norm.astype(F32)).astype(z.dtype)
    return y @ w_o, new_conv, h_new.astype(z.dtype)


def _swa_prompt(q, k, v, table, sinks):
    Bt, L = q.shape[:2]
    nc = L // CHUNK
    R = SWA_N_HEADS // SWA_N_KV
    span = (WINDOW_CHUNKS + 1) * CHUNK
    qb = q.reshape(Bt, nc, CHUNK, SWA_N_KV, R, SWA_HEAD_DIM).astype(F32)
    pad = ((0, 0), (WINDOW_CHUNKS, 0), (0, 0), (0, 0), (0, 0))
    kc = jnp.pad(k.reshape(Bt, nc, CHUNK, SWA_N_KV, SWA_HEAD_DIM), pad)
    vc = jnp.pad(v.reshape(Bt, nc, CHUNK, SWA_N_KV, SWA_HEAD_DIM), pad)
    kb = jnp.concatenate([kc[:, i:i + nc] for i in range(WINDOW_CHUNKS + 1)], axis=2).astype(F32)
    vb = jnp.concatenate([vc[:, i:i + nc] for i in range(WINDOW_CHUNKS + 1)], axis=2).astype(F32)
    rel = (jnp.arange(span)[None, :] - WINDOW_CHUNKS * CHUNK) - jnp.arange(CHUNK)[:, None]
    bias = _rel_bias(rel, table)
    key_chunk = jnp.arange(nc)[:, None] + jnp.arange(span)[None, :] // CHUNK - WINDOW_CHUNKS
    valid = (key_chunk >= 0)[None, :, None, None, None, :]
    logits = jnp.einsum('bcqgrd,bckgd->bcgrqk', qb, kb) * SWA_SCALE + bias
    logits = jnp.where(valid, logits, -jnp.inf)
    p = _sink_softmax(logits, sinks.astype(F32).reshape(SWA_N_KV, R, 1, 1))
    out = jnp.einsum('bcgrqk,bckgd->bcqgrd', p, vb)
    return out.reshape(Bt, L, SWA_N_HEADS * SWA_HEAD_DIM).astype(q.dtype)


def _swa_sample(q, k, v, k_cache, v_cache, table, sinks):
    Bt, L = q.shape[:2]
    C = k_cache.shape[1]
    R = SWA_N_HEADS // SWA_N_KV
    keys = jnp.concatenate([k_cache.astype(k.dtype), k], axis=1).astype(F32)
    vals = jnp.concatenate([v_cache.astype(v.dtype), v], axis=1).astype(F32)
    q_pos = PAST_LEN + jnp.arange(L)
    k_pos = jnp.concatenate([PAST_LEN - C + jnp.arange(C), q_pos])
    qc, kc = q_pos // CHUNK, k_pos // CHUNK
    valid = (kc[None, :] >= qc[:, None] - WINDOW_CHUNKS) & (kc[None, :] <= qc[:, None])
    bias = _rel_bias(k_pos[None, :] - q_pos[:, None], table)
    qg = q.reshape(Bt, L, SWA_N_KV, R, SWA_HEAD_DIM).astype(F32)
    logits = jnp.einsum('bqgrd,bkgd->bgrqk', qg, keys) * SWA_SCALE + bias
    logits = jnp.where(valid[None, None, None], logits, -jnp.inf)
    p = _sink_softmax(logits, sinks.astype(F32).reshape(SWA_N_KV, R, 1, 1))
    out = jnp.einsum('bgrqk,bkgd->bqgrd', p, vals)
    return out.reshape(Bt, L, SWA_N_HEADS * SWA_HEAD_DIM).astype(q.dtype)


def _mem_kv(mem, g_norm, w_k, w_v, g_k):
    Bt, M, _ = mem.shape
    mn = _rmsnorm(mem, g_norm)
    k = _rmsnorm((mn @ w_k).reshape(Bt, M, MEM_N_HEADS, MEM_HEAD_DIM), g_k)
    v = (mn @ w_v).reshape(Bt, M, MEM_N_HEADS, MEM_HEAD_DIM)
    return k, v


def _mem_attend(q, mk, mv):
    Bt, L = q.shape[:2]
    logits = jnp.einsum('bshd,bmhd->bhsm', q.astype(F32), mk.astype(F32)) * MEM_SCALE
    p = jax.nn.softmax(logits, axis=-1)
    out = jnp.einsum('bhsm,bmhd->bshd', p, mv.astype(F32))
    return out.reshape(Bt, L, MEM_N_HEADS * MEM_HEAD_DIM).astype(q.dtype)


def _moe_dispatch(x, experts, weights, w1, w3, w2):
    T, D = x.shape
    M = T * TOP_K_IN_GROUP
    e_flat = experts.reshape(M)
    w_flat = weights.reshape(M)
    tok = jnp.arange(M) // TOP_K_IN_GROUP
    order = jnp.argsort(e_flat)
    e_sorted = e_flat[order]
    counts = jnp.zeros((N_EXPERTS,), jnp.int32).at[e_flat].add(1)
    padded = (counts + MOE_BLOCK - 1) // MOE_BLOCK * MOE_BLOCK
    pad_end = jnp.cumsum(padded)
    pad_start = pad_end - padded
    start = jnp.cumsum(counts) - counts
    dest = pad_start[e_sorted] + (jnp.arange(M) - start[e_sorted])
    n_blocks = (M + N_EXPERTS * (MOE_BLOCK - 1) + MOE_BLOCK - 1) // MOE_BLOCK
    P = n_blocks * MOE_BLOCK
    row_tok = jnp.full((P,), T, jnp.int32).at[dest].set(tok[order])
    row_w = jnp.zeros((P,), weights.dtype).at[dest].set(w_flat[order])
    blk_exp = jnp.minimum(jnp.searchsorted(pad_end, jnp.arange(n_blocks) * MOE_BLOCK, side='right'),
                          N_EXPERTS - 1)
    x_pad = jnp.concatenate([x, jnp.zeros((1, D), x.dtype)], axis=0)
    xr = x_pad[row_tok].reshape(n_blocks, MOE_BLOCK, D)

    def expert_block(args):
        xb, e = args
        return (jax.nn.silu(xb @ w1[e]) * (xb @ w3[e])) @ w2[e]

    yr = lax.map(expert_block, (xr, blk_exp)).reshape(P, D)
    y = jax.ops.segment_sum(yr.astype(F32) * row_w[:, None], row_tok, num_segments=T + 1)[:T]
    return y.astype(x.dtype)


def _hier_moe(x, w_grp, w_exp_r, w1, w3, w2):
    T = x.shape[0]
    g_logits = (x @ w_grp).astype(F32)
    g_prob = jax.nn.softmax(g_logits, axis=-1)
    _, g_idx = lax.top_k(g_logits, 1)
    g_w = jnp.take_along_axis(g_prob, g_idx, axis=-1)
    e_logits = (x @ w_exp_r).astype(F32).reshape(T, N_EXPERT_GROUPS, EXPERTS_PER_GROUP)
    e_in = jnp.take_along_axis(e_logits, g_idx[:, :, None], axis=1)[:, 0]
    e_val, e_idx = lax.top_k(e_in, TOP_K_IN_GROUP)
    weights = g_w * jax.nn.softmax(e_val, axis=-1)
    experts = g_idx * EXPERTS_PER_GROUP + e_idx
    return _moe_dispatch(x, experts, weights, w1, w3, w2)


def _layer(x, lp, table, conv_prev, ssd_h0, mem_k, mem_v, swa_k_cache, swa_v_cache, ssd_chunk):
    Bt, L, _ = x.shape
    h = _rmsnorm(x, lp['g_mix'])
    wz, wxbc, wdt, wqs, wks, wvs, wqm, wg = _split_cols(lp['w_in'])
    ssd_out, conv_state, ssd_state = _ssd_branch(
        h @ wz, h @ wxbc, h @ wdt, conv_prev, ssd_h0, lp['conv_w'], lp['conv_b'], lp['dt_bias'],
        lp['a_log'], lp['d_skip'], lp['g_ssd'], lp['w_o_ssd'], ssd_chunk)
    q_s = _rmsnorm((h @ wqs).reshape(Bt, L, SWA_N_HEADS, SWA_HEAD_DIM), lp['g_q_swa'])
    k_s = _rmsnorm((h @ wks).reshape(Bt, L, SWA_N_KV, SWA_HEAD_DIM), lp['g_k_swa'])
    v_s = (h @ wvs).reshape(Bt, L, SWA_N_KV, SWA_HEAD_DIM)
    if swa_k_cache is None:
        o_s = _swa_prompt(q_s, k_s, v_s, table, lp['sinks'])
    else:
        o_s = _swa_sample(q_s, k_s, v_s, swa_k_cache, swa_v_cache, table, lp['sinks'])
    q_m = _rmsnorm((h @ wqm).reshape(Bt, L, MEM_N_HEADS, MEM_HEAD_DIM), lp['g_q_mem'])
    o_m = _mem_attend(q_m, mem_k, mem_v)
    gates = jax.nn.sigmoid((h @ wg).astype(F32)).reshape(Bt, L, 3, D_MODEL).astype(x.dtype)
    merged = (gates[..., 0, :] * ssd_out + gates[..., 1, :] * (o_s @ lp['w_o_swa'])
              + gates[..., 2, :] * (o_m @ lp['w_o_mem']))
    x = x + merged @ lp['w_out']
    h2 = _rmsnorm(x, lp['g_ffn']).reshape(Bt * L, D_MODEL)
    x = x + _hier_moe(h2, lp['w_router_grp'], lp['w_router_exp'], lp['w_exp_gate'],
                      lp['w_exp_up'], lp['w_exp_down']).reshape(Bt, L, D_MODEL)
    return x, conv_state, ssd_state, k_s, v_s


def setup_inputs(seed: int = 0) -> dict:
    key = jax.random.key(seed)
    ks = iter(jax.random.split(key, 48))

    def nrm(shape, scale):
        return scale * jax.random.normal(next(ks), shape, F32)

    def gain(shape):
        return 1.0 + 0.02 * jax.random.normal(next(ks), shape, F32)

    Lr = DEPTH
    swa_c = min(WINDOW, PAST_LEN)
    dt0 = jnp.exp(jax.random.uniform(next(ks), (Lr, SSD_N_HEADS), F32, math.log(1e-3), math.log(1e-1)))
    dt_bias = dt0 + jnp.log(-jnp.expm1(-dt0))
    a_log = jnp.log(jax.random.uniform(next(ks), (Lr, SSD_N_HEADS), F32, 1.0, 16.0))
    sw = SWA_N_HEADS * SWA_HEAD_DIM
    mw = MEM_N_HEADS * MEM_HEAD_DIM
    return {
        'x_prompt': nrm((BATCH, SEQ, D_MODEL), 1.0),
        'x_sample': nrm((DEC_BATCH, DEC_SEQ, D_MODEL), 1.0),
        'cache_conv': nrm((Lr, DEC_BATCH, SSD_CONV - 1, SSD_CONV_DIM), 1.0),
        'state_ssd': nrm((Lr, DEC_BATCH, SSD_N_HEADS, SSD_HEAD_DIM, SSD_D_STATE), 0.1),
        'cache_swa_k': nrm((Lr, DEC_BATCH, swa_c, SWA_N_KV, SWA_HEAD_DIM), 1.0),
        'cache_swa_v': nrm((Lr, DEC_BATCH, swa_c, SWA_N_KV, SWA_HEAD_DIM), 1.0),
        'cache_mem_k': nrm((Lr, DEC_BATCH, MEM_LEN, MEM_N_HEADS, MEM_HEAD_DIM), 1.0),
        'cache_mem_v': nrm((Lr, DEC_BATCH, MEM_LEN, MEM_N_HEADS, MEM_HEAD_DIM), 1.0),
        'mem_prompt': nrm((BATCH, MEM_LEN, D_MODEL), 1.0),
        'rel_bias_table': nrm((N_BUCKETS, SWA_N_HEADS), 0.3),
        'g_mix': gain((Lr, D_MODEL)),
        'w_in': nrm((Lr, D_MODEL, IN_COLS), D_MODEL ** -0.5),
        'conv_w': nrm((Lr, SSD_CONV, SSD_CONV_DIM), SSD_CONV ** -0.5),
        'conv_b': nrm((Lr, SSD_CONV_DIM), 0.01),
        'dt_bias': dt_bias,
        'a_log': a_log,
        'd_skip': gain((Lr, SSD_N_HEADS)),
        'g_ssd': gain((Lr, SSD_D_INNER)),
        'w_o_ssd': nrm((Lr, SSD_D_INNER, D_MODEL), SSD_D_INNER ** -0.5),
        'g_q_swa': gain((Lr, SWA_HEAD_DIM)),
        'g_k_swa': gain((Lr, SWA_HEAD_DIM)),
        'sinks': nrm((Lr, SWA_N_HEADS), 0.5),
        'w_o_swa': nrm((Lr, sw, D_MODEL), sw ** -0.5),
        'g_mem': gain((Lr, D_MODEL)),
        'w_mem_k': nrm((Lr, D_MODEL, mw), D_MODEL ** -0.5),
        'w_mem_v': nrm((Lr, D_MODEL, mw), D_MODEL ** -0.5),
        'g_q_mem': gain((Lr, MEM_HEAD_DIM)),
        'g_k_mem': gain((Lr, MEM_HEAD_DIM)),
        'w_o_mem': nrm((Lr, mw, D_MODEL), mw ** -0.5),
        'w_out': nrm((Lr, D_MODEL, D_MODEL), D_MODEL ** -0.5),
        'g_ffn': gain((Lr, D_MODEL)),
        'w_router_grp': nrm((Lr, D_MODEL, N_EXPERT_GROUPS), D_MODEL ** -0.5),
        'w_router_exp': nrm((Lr, D_MODEL, N_EXPERTS), D_MODEL ** -0.5),
        'w_exp_gate': nrm((Lr, N_EXPERTS, D_MODEL, D_FF_EXPERT), D_MODEL ** -0.5),
        'w_exp_up': nrm((Lr, N_EXPERTS, D_MODEL, D_FF_EXPERT), D_MODEL ** -0.5),
        'w_exp_down': nrm((Lr, N_EXPERTS, D_FF_EXPERT, D_MODEL), D_FF_EXPERT ** -0.5),
    }


def reference(x_prompt, x_sample, cache_conv, state_ssd, cache_swa_k, cache_swa_v, cache_mem_k, cache_mem_v,
              mem_prompt, rel_bias_table, g_mix, w_in, conv_w, conv_b, dt_bias, a_log, d_skip, g_ssd, w_o_ssd,
              g_q_swa, g_k_swa, sinks, w_o_swa, g_mem, w_mem_k, w_mem_v, g_q_mem, g_k_mem, w_o_mem, w_out,
              g_ffn, w_router_grp, w_router_exp, w_exp_gate, w_exp_up, w_exp_down):
    B, S, _ = x_prompt.shape
    Sd = x_sample.shape[1]
    keep = min(WINDOW, S)
    yp, ys = x_prompt, x_sample
    pc, pst, pk, pv, pmk, pmv = [], [], [], [], [], []
    sc, sst, sk, sv = [], [], [], []
    for l in range(DEPTH):
        lp = {'g_mix': g_mix[l], 'w_in': w_in[l], 'conv_w': conv_w[l], 'conv_b': conv_b[l],
              'dt_bias': dt_bias[l], 'a_log': a_log[l], 'd_skip': d_skip[l], 'g_ssd': g_ssd[l],
              'w_o_ssd': w_o_ssd[l], 'g_q_swa': g_q_swa[l], 'g_k_swa': g_k_swa[l], 'sinks': sinks[l],
              'w_o_swa': w_o_swa[l], 'g_q_mem': g_q_mem[l], 'w_o_mem': w_o_mem[l], 'w_out': w_out[l],
              'g_ffn': g_ffn[l], 'w_router_grp': w_router_grp[l], 'w_router_exp': w_router_exp[l],
              'w_exp_gate': w_exp_gate[l], 'w_exp_up': w_exp_up[l], 'w_exp_down': w_exp_down[l]}
        mk_p, mv_p = _mem_kv(mem_prompt, g_mem[l], w_mem_k[l], w_mem_v[l], g_k_mem[l])
        conv0 = jnp.zeros((B, SSD_CONV - 1, SSD_CONV_DIM), x_prompt.dtype)
        h0 = jnp.zeros((B, SSD_N_HEADS, SSD_HEAD_DIM, SSD_D_STATE), F32)
        yp, c_p, s_p, k_p, v_p = _layer(yp, lp, rel_bias_table, conv0, h0, mk_p, mv_p, None, None, CHUNK)
        pc.append(c_p); pst.append(s_p); pk.append(k_p[:, S - keep:]); pv.append(v_p[:, S - keep:])
        pmk.append(mk_p); pmv.append(mv_p)
        ys, c_s, s_s, k_s, v_s = _layer(ys, lp, rel_bias_table, cache_conv[l], state_ssd[l], cache_mem_k[l],
                                        cache_mem_v[l], cache_swa_k[l], cache_swa_v[l], Sd)
        sc.append(c_s); sst.append(s_s); sk.append(k_s); sv.append(v_s)
    conv_p, ssd_p, swa_k_p, swa_v_p = jnp.stack(pc), jnp.stack(pst), jnp.stack(pk), jnp.stack(pv)
    mem_k_p, mem_v_p = jnp.stack(pmk), jnp.stack(pmv)
    conv_s, ssd_s, swa_k_s, swa_v_s = jnp.stack(sc), jnp.stack(sst), jnp.stack(sk), jnp.stack(sv)
    return (yp, ys, conv_p, ssd_p, swa_k_p, swa_v_p, mem_k_p, mem_v_p, conv_s, ssd_s, swa_k_s, swa_v_s)
```

```python
import functools
import math

import numpy as np
import jax
import jax.numpy as jnp
from jax import lax
from jax.experimental import pallas as pl
from jax.experimental.pallas import tpu as pltpu

F32 = jnp.float32
BF16 = jnp.bfloat16
EPS = 1e-6
NEG_INF = float("-inf")

CHUNK = 64
SSD_HEAD_DIM = 64
SSD_N_HEADS = 64
SSD_N_GROUPS = 8
SSD_D_STATE = 128
SSD_D_INNER = SSD_N_HEADS * SSD_HEAD_DIM
SSD_GN = SSD_N_GROUPS * SSD_D_STATE
SSD_CONV_DIM = SSD_D_INNER + 2 * SSD_GN
SSD_CONV = 4
SWA_N_HEADS = 32
SWA_N_KV = 4
SWA_HEAD_DIM = 64
SWA_REP = SWA_N_HEADS // SWA_N_KV
WINDOW = 128
WINDOW_CHUNKS = WINDOW // CHUNK
MEM_N_HEADS = 4
MEM_HEAD_DIM = 512
N_BUCKETS = 32
MAX_DISTANCE = 128
N_EXPERT_GROUPS = 4
EXPERTS_PER_GROUP = 16
N_EXPERTS = N_EXPERT_GROUPS * EXPERTS_PER_GROUP
TOP_K = 2
PAST_LEN = 4096

LANES = 128
SUBLANES = 8
VMEM_LIMIT = 56 * 1024 * 1024
MOE_ROWS = 256
PAD_HEADS = LANES


def _params(sem, vmem=VMEM_LIMIT):
    return pltpu.CompilerParams(dimension_semantics=sem, vmem_limit_bytes=vmem)


def _sigmoid(x):
    return 1.0 / (1.0 + jnp.exp(-x))


def _split3(x):
    x1 = x.astype(BF16)
    r1 = x - x1.astype(F32)
    x2 = r1.astype(BF16)
    x3 = (r1 - x2.astype(F32)).astype(BF16)
    return x1, x2, x3


def _rmsnorm_kernel(x_ref, g_ref, o_ref):
    x = x_ref[...]
    ms = jnp.mean(x * x, axis=-1, keepdims=True)
    o_ref[...] = (x * lax.rsqrt(ms + EPS) * g_ref[...]).astype(o_ref.dtype)


def _rmsnorm(x, g, tm):
    T, D = x.shape
    return pl.pallas_call(
        _rmsnorm_kernel,
        grid=(T // tm,),
        in_specs=[pl.BlockSpec((tm, D), lambda i: (i, 0)),
                  pl.BlockSpec((1, D), lambda i: (0, 0))],
        out_specs=pl.BlockSpec((tm, D), lambda i: (i, 0)),
        out_shape=jax.ShapeDtypeStruct((T, D), BF16),
        compiler_params=_params(("parallel",)),
        name="rmsnorm",
    )(x, g.reshape(1, D).astype(F32))


def _mm_kernel(a_ref, b_ref, *refs, epi):
    o_ref = refs[-1]
    acc = jnp.dot(a_ref[...], b_ref[...], preferred_element_type=F32)
    if epi is not None:
        acc = epi(acc, *[r[...] for r in refs[:-1]])
    o_ref[...] = acc.astype(o_ref.dtype)


def _matmul(a, b, out_dtype, tm, tn, epi=None, col_extras=(), tile_extras=(), const_extras=(), name="matmul"):
    M, K = a.shape
    N = b.shape[1]
    assert M % tm == 0 and N % tn == 0, (M, N, tm, tn)
    in_specs = [pl.BlockSpec((tm, K), lambda i, j: (i, 0)),
                pl.BlockSpec((K, tn), lambda i, j: (0, j))]
    for _ in col_extras:
        in_specs.append(pl.BlockSpec((1, tn), lambda i, j: (0, j)))
    for _ in tile_extras:
        in_specs.append(pl.BlockSpec((tm, tn), lambda i, j: (i, j)))
    for c in const_extras:
        in_specs.append(pl.BlockSpec(c.shape, lambda i, j: (0, 0)))
    return pl.pallas_call(
        functools.partial(_mm_kernel, epi=epi),
        grid=(M // tm, N // tn),
        in_specs=in_specs,
        out_specs=pl.BlockSpec((tm, tn), lambda i, j: (i, j)),
        out_shape=jax.ShapeDtypeStruct((M, N), out_dtype),
        compiler_params=_params(("parallel", "arbitrary")),
        name=name,
    )(a, b, *col_extras, *tile_extras, *const_extras)


def _group_mean_matrix(width, group):
    idx = np.arange(width) // group
    return jnp.asarray((idx[:, None] == idx[None, :]).astype(np.float32) / group, dtype=BF16)


def _epi_group_norm(acc, gain, mavg):
    s = acc * acc
    hi = s.astype(BF16)
    lo = (s - hi.astype(F32)).astype(BF16)
    ms = (jnp.dot(hi, mavg, preferred_element_type=F32)
          + jnp.dot(lo, mavg, preferred_element_type=F32))
    return acc * lax.rsqrt(ms + EPS) * gain


def _epi_kv(acc, gain, mavg):
    normed = _epi_group_norm(acc, gain, mavg)
    return jnp.where(pl.program_id(1) == 0, normed, acc)


def _epi_row_norm(acc, gain):
    ms = jnp.mean(acc * acc, axis=-1, keepdims=True)
    return acc * lax.rsqrt(ms + EPS) * gain


def _epi_sigmoid(acc):
    return _sigmoid(acc)


def _epi_residual(acc, res):
    return acc + res


def _ssd_kernel(z_ref, xbc_ref, dt_ref, cprev_ref, h0_ref, cw_ref, cb_ref, dtb_ref, alog_ref,
                dskip_ref, gn_ref, y_ref, hout_ref, cout_ref, xp_s, h_s, conv_s, y_s, *, Lc):
    c = pl.program_id(1)
    n_chunks = pl.num_programs(1)
    P2 = 2 * SSD_HEAD_DIM
    L2 = 2 * Lc

    @pl.when(c == 0)
    def _init():
        xp_s[0:SUBLANES, :] = cprev_ref[0]
        h_s[...] = h0_ref[0]

    xp_s[SUBLANES:SUBLANES + Lc, :] = xbc_ref[...].astype(F32)
    cblk = 512
    for j in range(SSD_CONV_DIM // cblk):
        sl = slice(j * cblk, (j + 1) * cblk)
        acc = cb_ref[:, sl] + cw_ref[0:1, sl] * xp_s[SUBLANES - 3:SUBLANES - 3 + Lc, sl]
        acc = acc + cw_ref[1:2, sl] * xp_s[SUBLANES - 2:SUBLANES - 2 + Lc, sl]
        acc = acc + cw_ref[2:3, sl] * xp_s[SUBLANES - 1:SUBLANES - 1 + Lc, sl]
        acc = acc + cw_ref[3:4, sl] * xp_s[SUBLANES:SUBLANES + Lc, sl]
        conv_s[:, sl] = acc * _sigmoid(acc)
    xp_s[0:SUBLANES, :] = xp_s[Lc:Lc + SUBLANES, :]

    dtv = dt_ref[...] + dtb_ref[...]
    dt = jnp.maximum(dtv, 0.0) + jnp.log1p(jnp.exp(-jnp.abs(dtv)))
    adt = dt * (-jnp.exp(alog_ref[...]))
    row = lax.broadcasted_iota(jnp.int32, (Lc, Lc), 0)
    col = lax.broadcasted_iota(jnp.int32, (Lc, Lc), 1)
    tri = (col <= row).astype(BF16)
    row2 = lax.broadcasted_iota(jnp.int32, (Lc, L2), 0)
    col2 = lax.broadcasted_iota(jnp.int32, (Lc, L2), 1)
    col2m = jnp.where(col2 >= Lc, col2 - Lc, col2)
    tri_t2 = (row2 <= col2m).astype(BF16)
    causal2 = col2m <= row2
    a1, a2, a3 = _split3(adt)
    acs = (jnp.dot(tri, a1, preferred_element_type=F32)
           + jnp.dot(tri, a2, preferred_element_type=F32)
           + jnp.dot(tri, a3, preferred_element_type=F32))
    tdims = (((0,), (0,)), ((), ()))
    acs_t2 = (lax.dot_general(a1, tri_t2, tdims, preferred_element_type=F32)
              + lax.dot_general(a2, tri_t2, tdims, preferred_element_type=F32)
              + lax.dot_general(a3, tri_t2, tdims, preferred_element_type=F32))

    lane_p = lax.broadcasted_iota(jnp.int32, (Lc, P2), 1)
    first_p = lane_p < SSD_HEAD_DIM
    first_l = col2 < Lc
    first_l1 = first_l[0:1, :]
    ndims = (((1,), (1,)), ((), ()))

    for g in range(SSD_N_GROUPS):
        b_g = conv_s[:, SSD_D_INNER + g * SSD_D_STATE:SSD_D_INNER + (g + 1) * SSD_D_STATE].astype(BF16)
        c_g = conv_s[:, SSD_D_INNER + SSD_GN + g * SSD_D_STATE:
                     SSD_D_INNER + SSD_GN + (g + 1) * SSD_D_STATE].astype(BF16)
        b2 = jnp.concatenate([b_g, b_g], axis=0)
        cb2 = lax.dot_general(c_g, b2, ndims, preferred_element_type=F32)
        gw = SSD_HEAD_DIM * (SSD_N_HEADS // SSD_N_GROUPS)
        inter = jnp.dot(c_g, h_s[:, g * gw:(g + 1) * gw].astype(BF16), preferred_element_type=F32)
        for jj in range(gw // P2):
            j = g * (gw // P2) + jj
            sl = slice(j * P2, (j + 1) * P2)
            acs_a = acs[:, 2 * j:2 * j + 1]
            acs_b = acs[:, 2 * j + 1:2 * j + 2]
            col_l = jnp.where(first_l, acs_a, acs_b)
            row_l = jnp.where(first_l1, acs_t2[2 * j:2 * j + 1, :], acs_t2[2 * j + 1:2 * j + 2, :])
            dec = jnp.exp(jnp.where(causal2, col_l - row_l, NEG_INF))
            m_pair = (cb2 * dec).astype(BF16)
            col_p = col_l if L2 == P2 else jnp.where(first_p, acs_a, acs_b)
            dt_p = jnp.where(first_p, dt[:, 2 * j:2 * j + 1], dt[:, 2 * j + 1:2 * j + 2])
            xs_p = conv_s[:, sl]
            xdt = xs_p * dt_p
            rhs = jnp.concatenate([jnp.where(first_p, xdt, 0.0), jnp.where(first_p, 0.0, xdt)],
                                  axis=0).astype(BF16)
            y = jnp.dot(m_pair, rhs, preferred_element_type=F32)
            y = y + inter[:, jj * P2:(jj + 1) * P2] * jnp.exp(col_p) + dskip_ref[:, sl] * xs_p
            y_s[:, sl] = y
            a_end = col_p[Lc - 1:Lc, :]
            xw = (xdt * jnp.exp(a_end - col_p)).astype(BF16)
            h_s[:, sl] = (h_s[:, sl] * jnp.exp(a_end)
                          + lax.dot_general(b_g, xw, tdims, preferred_element_type=F32))

    gdim = SSD_D_INNER // SSD_N_GROUPS
    for g in range(SSD_N_GROUPS):
        sl = slice(g * gdim, (g + 1) * gdim)
        zz = z_ref[:, sl].astype(F32)
        yy = y_s[:, sl] * (zz * _sigmoid(zz))
        ms = jnp.mean(yy * yy, axis=-1, keepdims=True)
        y_ref[:, sl] = (yy * lax.rsqrt(ms + EPS) * gn_ref[:, sl]).astype(y_ref.dtype)

    @pl.when(c == n_chunks - 1)
    def _fin():
        hout_ref[0] = h_s[...]
        cout_ref[0] = xp_s[0:SUBLANES, :]


def _ssd(z, xbc, dt, conv_prev8, h0_t, p, B, L, Lc):
    T = B * L
    nc = L // Lc
    tok = lambda b, c: (b * nc + c, 0)
    per_b = lambda b, c: (b, 0, 0)
    whole = lambda b, c: (0, 0)
    y, h_t, conv8 = pl.pallas_call(
        functools.partial(_ssd_kernel, Lc=Lc),
        grid=(B, nc),
        in_specs=[pl.BlockSpec((Lc, SSD_D_INNER), tok),
                  pl.BlockSpec((Lc, SSD_CONV_DIM), tok),
                  pl.BlockSpec((Lc, PAD_HEADS), tok),
                  pl.BlockSpec((1, SUBLANES, SSD_CONV_DIM), per_b),
                  pl.BlockSpec((1, SSD_D_STATE, SSD_D_INNER), per_b),
                  pl.BlockSpec((SSD_CONV, SSD_CONV_DIM), whole),
                  pl.BlockSpec((1, SSD_CONV_DIM), whole),
                  pl.BlockSpec((1, PAD_HEADS), whole),
                  pl.BlockSpec((1, PAD_HEADS), whole),
                  pl.BlockSpec((1, SSD_D_INNER), whole),
                  pl.BlockSpec((1, SSD_D_INNER), whole)],
        out_specs=[pl.BlockSpec((Lc, SSD_D_INNER), tok),
                   pl.BlockSpec((1, SSD_D_STATE, SSD_D_INNER), per_b),
                   pl.BlockSpec((1, SUBLANES, SSD_CONV_DIM), per_b)],
        out_shape=[jax.ShapeDtypeStruct((T, SSD_D_INNER), BF16),
                   jax.ShapeDtypeStruct((B, SSD_D_STATE, SSD_D_INNER), F32),
                   jax.ShapeDtypeStruct((B, SUBLANES, SSD_CONV_DIM), F32)],
        scratch_shapes=[pltpu.VMEM((SUBLANES + Lc, SSD_CONV_DIM), F32),
                        pltpu.VMEM((SSD_D_STATE, SSD_D_INNER), F32),
                        pltpu.VMEM((Lc, SSD_CONV_DIM), F32),
                        pltpu.VMEM((Lc, SSD_D_INNER), F32)],
        compiler_params=_params(("arbitrary", "arbitrary")),
        name="ssd_scan",
    )(z, xbc, dt, conv_prev8, h0_t, p["conv_w"], p["conv_b"], p["dt_bias"], p["a_log"],
      p["d_skip"], p["g_ssd"])
    return y, h_t, conv8


def _bias_kernel(idx_ref, tab_ref, o_ref):
    h = pl.program_id(0)
    idx = idx_ref[...]
    acc = jnp.full(idx.shape, NEG_INF, F32)
    for b in range(N_BUCKETS):
        acc = jnp.where(idx == b, tab_ref[b, h], acc)
    o_ref[0] = acc


def _bias_from_table(idx, table):
    Q, K = idx.shape
    return pl.pallas_call(
        _bias_kernel,
        grid=(SWA_N_HEADS,),
        in_specs=[pl.BlockSpec((Q, K), lambda h: (0, 0)),
                  pl.BlockSpec(memory_space=pltpu.SMEM)],
        out_specs=pl.BlockSpec((1, Q, K), lambda h: (h, 0, 0)),
        out_shape=jax.ShapeDtypeStruct((SWA_N_HEADS, Q, K), F32),
        compiler_params=_params(("arbitrary",)),
        name="rel_bias",
    )(jnp.asarray(idx, jnp.int32), table.astype(F32))


def _t5_bucket_np(rel):
    nb = N_BUCKETS // 2
    max_exact = nb // 2
    ret = np.where(rel > 0, nb, 0)
    n = np.abs(rel)
    nf = np.maximum(n, 1).astype(np.float32)
    large = max_exact + (np.log(nf / np.float32(max_exact)) / np.float32(math.log(MAX_DISTANCE / max_exact))
                         * np.float32(nb - max_exact)).astype(np.int32)
    large = np.minimum(large, nb - 1)
    return (ret + np.where(n < max_exact, n, large)).astype(np.int32)


def _bucket_map(q_pos, k_pos):
    qc, kc = q_pos // CHUNK, k_pos // CHUNK
    valid = (kc[None, :] >= qc[:, None] - WINDOW_CHUNKS) & (kc[None, :] <= qc[:, None])
    return np.where(valid, _t5_bucket_np(k_pos[None, :] - q_pos[:, None]), -1).astype(np.int32)


def _swa_kernel(q_ref, kp_ref, vp_ref, kc_ref, vc_ref, bp_ref, bc_ref, sink_ref, o_ref, *, mask_first):
    scale = SWA_HEAD_DIM ** -0.5
    ndims = (((1,), (1,)), ((), ()))
    kp = kp_ref[...].astype(BF16)
    vp = vp_ref[...].astype(BF16)
    kc = kc_ref[...].astype(BF16)
    vc = vc_ref[...].astype(BF16)
    has_prev = pl.program_id(1) > 0
    for g in range(SWA_N_KV):
        ks = slice(g * SWA_HEAD_DIM, (g + 1) * SWA_HEAD_DIM)
        for r in range(SWA_REP):
            h = g * SWA_REP + r
            hs = slice(h * SWA_HEAD_DIM, (h + 1) * SWA_HEAD_DIM)
            qh = q_ref[:, hs]
            sp = lax.dot_general(qh, kp[:, ks], ndims, preferred_element_type=F32) * scale + bp_ref[h]
            sc = lax.dot_general(qh, kc[:, ks], ndims, preferred_element_type=F32) * scale + bc_ref[h]
            if mask_first:
                sp = jnp.where(has_prev, sp, NEG_INF)
            sink = sink_ref[h]
            m = jnp.maximum(jnp.maximum(jnp.max(sp, axis=-1, keepdims=True),
                                        jnp.max(sc, axis=-1, keepdims=True)), sink)
            pp = jnp.exp(sp - m)
            pc = jnp.exp(sc - m)
            den = (jnp.sum(pp, axis=-1, keepdims=True) + jnp.sum(pc, axis=-1, keepdims=True)
                   + jnp.exp(sink - m))
            o = (jnp.dot(pp.astype(BF16), vp[:, ks], preferred_element_type=F32)
                 + jnp.dot(pc.astype(BF16), vc[:, ks], preferred_element_type=F32))
            o_ref[:, hs] = (o / den).astype(o_ref.dtype)


def _swa(q, k_prev_arr, v_prev_arr, prev_map, kv, bias_prev, bias_cur, sinks, B, L, TQ, mask_first):
    T = B * L
    nblk = L // TQ
    kvw = SWA_N_KV * SWA_HEAD_DIM
    n_prev = bias_prev.shape[2]
    return pl.pallas_call(
        functools.partial(_swa_kernel, mask_first=mask_first),
        grid=(B, nblk),
        in_specs=[pl.BlockSpec((TQ, SWA_N_HEADS * SWA_HEAD_DIM), lambda b, i: (b * nblk + i, 0)),
                  pl.BlockSpec((n_prev, kvw), prev_map[0]),
                  pl.BlockSpec((n_prev, kvw), prev_map[1]),
                  pl.BlockSpec((TQ, kvw), lambda b, i: (b * nblk + i, 0)),
                  pl.BlockSpec((TQ, kvw), lambda b, i: (b * nblk + i, 1)),
                  pl.BlockSpec(bias_prev.shape, lambda b, i: (0, 0, 0)),
                  pl.BlockSpec(bias_cur.shape, lambda b, i: (0, 0, 0)),
                  pl.BlockSpec(memory_space=pltpu.SMEM)],
        out_specs=pl.BlockSpec((TQ, SWA_N_HEADS * SWA_HEAD_DIM), lambda b, i: (b * nblk + i, 0)),
        out_shape=jax.ShapeDtypeStruct((T, SWA_N_HEADS * SWA_HEAD_DIM), BF16),
        compiler_params=_params(("parallel", "arbitrary")),
        name="swa_attention",
    )(q, k_prev_arr, v_prev_arr, kv, kv, bias_prev, bias_cur, sinks.astype(F32))


def _mem_kernel(q_ref, mk_ref, mv_ref, o_ref):
    scale = MEM_HEAD_DIM ** -0.5
    ndims = (((1,), (1,)), ((), ()))
    for h in range(MEM_N_HEADS):
        hs = slice(h * MEM_HEAD_DIM, (h + 1) * MEM_HEAD_DIM)
        s = lax.dot_general(q_ref[:, hs], mk_ref[:, hs], ndims, preferred_element_type=F32) * scale
        m = jnp.max(s, axis=-1, keepdims=True)
        p = jnp.exp(s - m)
        den = jnp.sum(p, axis=-1, keepdims=True)
        o = jnp.dot(p.astype(BF16), mv_ref[:, hs], preferred_element_type=F32)
        o_ref[:, hs] = (o / den).astype(o_ref.dtype)


def _mem_attend(q, mk, mv, L, tm):
    T, W = q.shape
    M = mk.shape[0] // (T // L)
    return pl.pallas_call(
        _mem_kernel,
        grid=(T // tm,),
        in_specs=[pl.BlockSpec((tm, W), lambda i: (i, 0)),
                  pl.BlockSpec((M, W), lambda i: ((i * tm) // L, 0)),
                  pl.BlockSpec((M, W), lambda i: ((i * tm) // L, 0))],
        out_specs=pl.BlockSpec((tm, W), lambda i: (i, 0)),
        out_shape=jax.ShapeDtypeStruct((T, W), BF16),
        compiler_params=_params(("parallel",)),
        name="mem_attention",
    )(q, mk, mv)


def _merge_kernel(ys_ref, os_ref, om_ref, w1_ref, w2_ref, w3_ref, g0_ref, g1_ref, g2_ref, o_ref):
    a = jnp.dot(ys_ref[...], w1_ref[...], preferred_element_type=F32)
    b = jnp.dot(os_ref[...], w2_ref[...], preferred_element_type=F32)
    c = jnp.dot(om_ref[...], w3_ref[...], preferred_element_type=F32)
    o = (g0_ref[...].astype(F32) * a + g1_ref[...].astype(F32) * b + g2_ref[...].astype(F32) * c)
    o_ref[...] = o.astype(o_ref.dtype)


def _merge(y_ssd, o_s, o_m, w1, w2, w3, gates, tm, tn):
    T = y_ssd.shape[0]
    D = w1.shape[1]
    nj = D // tn
    row = lambda i, j: (i, 0)
    colw = lambda i, j: (0, j)
    return pl.pallas_call(
        _merge_kernel,
        grid=(T // tm, nj),
        in_specs=[pl.BlockSpec((tm, y_ssd.shape[1]), row),
                  pl.BlockSpec((tm, o_s.shape[1]), row),
                  pl.BlockSpec((tm, o_m.shape[1]), row),
                  pl.BlockSpec((w1.shape[0], tn), colw),
                  pl.BlockSpec((w2.shape[0], tn), colw),
                  pl.BlockSpec((w3.shape[0], tn), colw),
                  pl.BlockSpec((tm, tn), lambda i, j: (i, j)),
                  pl.BlockSpec((tm, tn), lambda i, j: (i, j + nj)),
                  pl.BlockSpec((tm, tn), lambda i, j: (i, j + 2 * nj))],
        out_specs=pl.BlockSpec((tm, tn), lambda i, j: (i, j)),
        out_shape=jax.ShapeDtypeStruct((T, D), BF16),
        compiler_params=_params(("parallel", "arbitrary")),
        name="gated_merge",
    )(y_ssd, o_s, o_m, w1, w2, w3, gates, gates, gates)


def _norm_route_kernel(x_ref, g_ref, whi_ref, wlo_ref, h_ref, r_ref):
    x = x_ref[...]
    ms = jnp.mean(x * x, axis=-1, keepdims=True)
    h = x * lax.rsqrt(ms + EPS) * g_ref[...]
    h_ref[...] = h
    hb = h.astype(BF16)
    lo = (h - hb.astype(F32)).astype(BF16)
    lg = (jnp.dot(hb, whi_ref[...], preferred_element_type=F32)
          + jnp.dot(lo, whi_ref[...], preferred_element_type=F32)
          + jnp.dot(hb, wlo_ref[...], preferred_element_type=F32))
    lane = lax.broadcasted_iota(jnp.int32, lg.shape, 1)
    lane_f = lane.astype(F32)
    far = float(LANES)
    gl = jnp.where((lane >= N_EXPERTS) & (lane < N_EXPERTS + N_EXPERT_GROUPS), lg, NEG_INF)
    gmax = jnp.max(gl, axis=-1, keepdims=True)
    gidx = jnp.min(jnp.where(gl == gmax, lane_f - N_EXPERTS, far), axis=-1, keepdims=True)
    gw = 1.0 / jnp.sum(jnp.exp(gl - gmax), axis=-1, keepdims=True)
    lo_e = gidx * EXPERTS_PER_GROUP
    el = jnp.where((lane_f >= lo_e) & (lane_f < lo_e + EXPERTS_PER_GROUP), lg, NEG_INF)
    v1 = jnp.max(el, axis=-1, keepdims=True)
    i1 = jnp.min(jnp.where(el == v1, lane_f, far), axis=-1, keepdims=True)
    el2 = jnp.where(lane_f == i1, NEG_INF, el)
    v2 = jnp.max(el2, axis=-1, keepdims=True)
    i2 = jnp.min(jnp.where(el2 == v2, lane_f, far), axis=-1, keepdims=True)
    e = jnp.exp(v2 - v1)
    w1 = gw / (1.0 + e)
    w2 = gw * e / (1.0 + e)
    r_ref[...] = jnp.where(lane == 0, w1, jnp.where(lane == 1, w2,
                           jnp.where(lane == 2, i1, jnp.where(lane == 3, i2, 0.0))))


def _norm_route(x, g, w_hi, w_lo, tm):
    T, D = x.shape
    return pl.pallas_call(
        _norm_route_kernel,
        grid=(T // tm,),
        in_specs=[pl.BlockSpec((tm, D), lambda i: (i, 0)),
                  pl.BlockSpec((1, D), lambda i: (0, 0)),
                  pl.BlockSpec((D, LANES), lambda i: (0, 0)),
                  pl.BlockSpec((D, LANES), lambda i: (0, 0))],
        out_specs=[pl.BlockSpec((tm, D), lambda i: (i, 0)),
                   pl.BlockSpec((tm, LANES), lambda i: (i, 0))],
        out_shape=[jax.ShapeDtypeStruct((T, D), F32),
                   jax.ShapeDtypeStruct((T, LANES), F32)],
        compiler_params=_params(("parallel",)),
        name="ffn_norm_router",
    )(x, g.reshape(1, D).astype(F32), w_hi, w_lo)


def _moe_kernel(be_ref, nv_ref, tok_ref, dst_ref, rw_ref, h_hbm, w1_ref, w3_ref, w2_ref, yu_hbm,
                xbuf, ybuf, w1b, w3b, w2b, sem_in, sem_out):
    i = pl.program_id(0)
    nv = nv_ref[i]

    def gather_copy(r):
        return pltpu.make_async_copy(h_hbm.at[pl.ds(tok_ref[0, 0, r], 1), :],
                                     xbuf.at[pl.ds(r, 1), :], sem_in)

    def scatter_copy(r):
        return pltpu.make_async_copy(ybuf.at[pl.ds(r, 1), :],
                                     yu_hbm.at[pl.ds(dst_ref[0, 0, r], 1), :], sem_out)

    @pl.when(i == 0)
    def _zero():
        xbuf[...] = jnp.zeros(xbuf.shape, xbuf.dtype)

    prev = be_ref[jnp.maximum(i - 1, 0)]

    @pl.when((i == 0) | (be_ref[i] != prev))
    def _load_expert():
        w1b[...] = w1_ref[0].astype(BF16)
        w3b[...] = w3_ref[0].astype(BF16)
        w2b[...] = w2_ref[0].astype(BF16)

    @pl.when(nv > 0)
    def _compute():
        def issue(r, carry):
            gather_copy(r).start()
            return carry
        lax.fori_loop(0, nv, issue, 0)

        def drain(r, carry):
            gather_copy(r).wait()
            return carry
        lax.fori_loop(0, nv, drain, 0)

        x = xbuf[...].astype(BF16)
        a = jnp.dot(x, w1b[...], preferred_element_type=F32)
        b = jnp.dot(x, w3b[...], preferred_element_type=F32)
        mid = ((a * _sigmoid(a)) * b).astype(BF16)
        y = jnp.dot(mid, w2b[...], preferred_element_type=F32)
        ybuf[...] = y * rw_ref[...]

        def issue_out(r, carry):
            scatter_copy(r).start()
            return carry
        lax.fori_loop(0, nv, issue_out, 0)

        def drain_out(r, carry):
            scatter_copy(r).wait()
            return carry
        lax.fori_loop(0, nv, drain_out, 0)


def _moe(h2, route, w_gate, w_up, w_down):
    T, D = h2.shape
    F = w_gate.shape[2]
    M = T * TOP_K
    BM = MOE_ROWS
    nb = (M + N_EXPERTS * (BM - 1) + BM - 1) // BM
    P = nb * BM
    w_flat = route[:, 0:TOP_K].reshape(M)
    e_flat = route[:, TOP_K:2 * TOP_K].astype(jnp.int32).reshape(M)
    order = jnp.argsort(e_flat, stable=True).astype(jnp.int32)
    e_sorted = e_flat[order]
    counts = jnp.zeros((N_EXPERTS,), jnp.int32).at[e_flat].add(1)
    padded = (counts + BM - 1) // BM * BM
    pad_end = jnp.cumsum(padded)
    pad_start = pad_end - padded
    start = jnp.cumsum(counts) - counts
    dest = pad_start[e_sorted] + (jnp.arange(M, dtype=jnp.int32) - start[e_sorted])
    row_tok = jnp.zeros((P,), jnp.int32).at[dest].set(order // TOP_K)
    row_dst = jnp.zeros((P,), jnp.int32).at[dest].set(order)
    row_w = jnp.zeros((P,), F32).at[dest].set(w_flat[order])
    blk0 = jnp.arange(nb, dtype=jnp.int32) * BM
    blk_exp = jnp.minimum(jnp.searchsorted(pad_end, blk0, side="right"), N_EXPERTS - 1).astype(jnp.int32)
    blk_nv = jnp.clip(counts[blk_exp] - (blk0 - pad_start[blk_exp]), 0, BM).astype(jnp.int32)

    grid_spec = pltpu.PrefetchScalarGridSpec(
        num_scalar_prefetch=2,
        grid=(nb,),
        in_specs=[pl.BlockSpec((1, 1, BM), lambda i, be, nv: (i, 0, 0), memory_space=pltpu.SMEM),
                  pl.BlockSpec((1, 1, BM), lambda i, be, nv: (i, 0, 0), memory_space=pltpu.SMEM),
                  pl.BlockSpec((BM, 1), lambda i, be, nv: (i, 0)),
                  pl.BlockSpec(memory_space=pl.ANY),
                  pl.BlockSpec((1, D, F), lambda i, be, nv: (be[i], 0, 0)),
                  pl.BlockSpec((1, D, F), lambda i, be, nv: (be[i], 0, 0)),
                  pl.BlockSpec((1, F, D), lambda i, be, nv: (be[i], 0, 0))],
        out_specs=pl.BlockSpec(memory_space=pl.ANY),
        scratch_shapes=[pltpu.VMEM((BM, D), F32),
                        pltpu.VMEM((BM, D), F32),
                        pltpu.VMEM((D, F), BF16),
                        pltpu.VMEM((D, F), BF16),
                        pltpu.VMEM((F, D), BF16),
                        pltpu.SemaphoreType.DMA,
                        pltpu.SemaphoreType.DMA],
    )
    yu = pl.pallas_call(
        _moe_kernel,
        grid_spec=grid_spec,
        out_shape=jax.ShapeDtypeStruct((M, D), F32),
        compiler_params=_params(("arbitrary",)),
        name="moe_experts",
    )(blk_exp, blk_nv, row_tok.reshape(nb, 1, BM), row_dst.reshape(nb, 1, BM), row_w.reshape(P, 1),
      h2, w_gate, w_up, w_down)
    return yu.reshape(T, TOP_K * D)


def _combine_kernel(x_ref, y0_ref, y1_ref, o_ref):
    o_ref[...] = x_ref[...] + (y0_ref[...] + y1_ref[...])


def _combine(x, yu, tm):
    T, D = x.shape
    return pl.pallas_call(
        _combine_kernel,
        grid=(T // tm,),
        in_specs=[pl.BlockSpec((tm, D), lambda i: (i, 0)),
                  pl.BlockSpec((tm, D), lambda i: (i, 0)),
                  pl.BlockSpec((tm, D), lambda i: (i, 1))],
        out_specs=pl.BlockSpec((tm, D), lambda i: (i, 0)),
        out_shape=jax.ShapeDtypeStruct((T, D), F32),
        compiler_params=_params(("parallel",)),
        name="moe_combine",
    )(x, yu, yu)


def _layer(x, B, L, p, conv_prev8, h0_t, mk, mv, swa_prev, bias_prev, bias_cur, Lc, TQ):
    T, D = x.shape
    tm = min(512, T)
    h = _rmsnorm(x, p["g_mix"], tm)
    z = _matmul(h, p["wz"], BF16, tm, 512, name="proj_z")
    xbc = _matmul(h, p["wxbc"], BF16, tm, 512, name="proj_xbc")
    dt = _matmul(h, p["wdt"], F32, tm, PAD_HEADS, name="proj_dt")
    q_s = _matmul(h, p["wqs"], BF16, tm, 256, epi=_epi_group_norm, col_extras=(p["g_q_swa"],),
                  const_extras=(p["mavg"],), name="proj_q_swa")
    kv = _matmul(h, p["wkv"], F32, tm, 256, epi=_epi_kv, col_extras=(p["g_kv"],),
                 const_extras=(p["mavg"],), name="proj_kv_swa")
    q_m = _matmul(h, p["wqm"], BF16, tm, MEM_HEAD_DIM, epi=_epi_row_norm, col_extras=(p["g_q_mem"],),
                  name="proj_q_mem")
    gates = _matmul(h, p["wg"], BF16, tm, 512, epi=_epi_sigmoid, name="proj_gates")

    y_ssd, h_t, conv8 = _ssd(z, xbc, dt, conv_prev8, h0_t, p, B, L, Lc)

    nblk = L // TQ
    if swa_prev is None:
        prev_map = (lambda b, i: (b * nblk + jnp.maximum(i - 1, 0), 0),
                    lambda b, i: (b * nblk + jnp.maximum(i - 1, 0), 1))
        o_s = _swa(q_s, kv, kv, prev_map, kv, bias_prev, bias_cur, p["sinks"], B, L, TQ, True)
    else:
        prev_map = (lambda b, i: (b, 0), lambda b, i: (b, 0))
        o_s = _swa(q_s, swa_prev[0], swa_prev[1], prev_map, kv, bias_prev, bias_cur, p["sinks"],
                   B, L, TQ, False)

    o_m = _mem_attend(q_m, mk, mv, L, min(256, L))

    merged = _merge(y_ssd, o_s, o_m, p["w_o_ssd"], p["w_o_swa"], p["w_o_mem"], gates, tm, 512)
    x1 = _matmul(merged, p["w_out"], F32, tm, 512, epi=_epi_residual, tile_extras=(x,), name="proj_out")

    h2, route = _norm_route(x1, p["g_ffn"], p["w_r_hi"], p["w_r_lo"], min(256, T))
    yu = _moe(h2, route, p["w_exp_gate"], p["w_exp_up"], p["w_exp_down"])
    y = _combine(x1, yu, tm)
    return y, conv8, h_t, kv


def _state_to_heads(h_t, B):
    return jnp.transpose(h_t.reshape(B, SSD_D_STATE, SSD_N_HEADS, SSD_HEAD_DIM), (0, 2, 3, 1))


def kernel(x_prompt, x_sample, cache_conv, state_ssd, cache_swa_k, cache_swa_v, cache_mem_k, cache_mem_v, mem_prompt, rel_bias_table, g_mix, w_in, conv_w, conv_b, dt_bias, a_log, d_skip, g_ssd, w_o_ssd, g_q_swa, g_k_swa, sinks, w_o_swa, g_mem, w_mem_k, w_mem_v, g_q_mem, g_k_mem, w_o_mem, w_out, g_ffn, w_router_grp, w_router_exp, w_exp_gate, w_exp_up, w_exp_down):
    B, S, D = x_prompt.shape
    Bd, Sd, _ = x_sample.shape
    depth = w_in.shape[0]
    assert depth == 1
    l = 0
    kvw = SWA_N_KV * SWA_HEAD_DIM
    qw = SWA_N_HEADS * SWA_HEAD_DIM
    mw = MEM_N_HEADS * MEM_HEAD_DIM

    sizes = (SSD_D_INNER, SSD_CONV_DIM, SSD_N_HEADS, qw, kvw, kvw, mw, 3 * D)
    offs = np.concatenate([[0], np.cumsum(sizes)])
    w = w_in[l]
    cols = [w[:, int(offs[k]):int(offs[k + 1])] for k in range(len(sizes))]
    pad_h = PAD_HEADS - SSD_N_HEADS
    w_r = jnp.pad(jnp.concatenate([w_router_exp[l], w_router_grp[l]], axis=1),
                  ((0, 0), (0, LANES - N_EXPERTS - N_EXPERT_GROUPS)))
    w_r_hi = w_r.astype(BF16)
    p = {
        "g_mix": g_mix[l],
        "wz": cols[0].astype(BF16),
        "wxbc": cols[1].astype(BF16),
        "wdt": jnp.pad(cols[2], ((0, 0), (0, pad_h))).astype(BF16),
        "wqs": cols[3].astype(BF16),
        "wkv": jnp.concatenate([cols[4], cols[5]], axis=1).astype(BF16),
        "wqm": cols[6].astype(BF16),
        "wg": cols[7].astype(BF16),
        "conv_w": conv_w[l].astype(F32),
        "conv_b": conv_b[l].reshape(1, SSD_CONV_DIM).astype(F32),
        "dt_bias": jnp.pad(dt_bias[l], (0, pad_h)).reshape(1, PAD_HEADS).astype(F32),
        "a_log": jnp.pad(a_log[l], (0, pad_h)).reshape(1, PAD_HEADS).astype(F32),
        "d_skip": jnp.repeat(d_skip[l], SSD_HEAD_DIM).reshape(1, SSD_D_INNER).astype(F32),
        "g_ssd": g_ssd[l].reshape(1, SSD_D_INNER).astype(F32),
        "g_q_swa": jnp.tile(g_q_swa[l], SWA_N_HEADS).reshape(1, qw).astype(F32),
        "g_kv": jnp.concatenate([jnp.tile(g_k_swa[l], SWA_N_KV), jnp.ones((kvw,), F32)]).reshape(1, 2 * kvw),
        "g_q_mem": jnp.tile(g_q_mem[l], MEM_N_HEADS).reshape(1, mw).astype(F32),
        "mavg": _group_mean_matrix(256, SWA_HEAD_DIM),
        "sinks": sinks[l],
        "w_o_ssd": w_o_ssd[l].astype(BF16),
        "w_o_swa": w_o_swa[l].astype(BF16),
        "w_o_mem": w_o_mem[l].astype(BF16),
        "w_out": w_out[l].astype(BF16),
        "g_ffn": g_ffn[l],
        "w_r_hi": w_r_hi,
        "w_r_lo": (w_r - w_r_hi.astype(F32)).astype(BF16),
        "w_exp_gate": w_exp_gate[l],
        "w_exp_up": w_exp_up[l],
        "w_exp_down": w_exp_down[l],
    }

    M = mem_prompt.shape[1]
    mn = _rmsnorm(mem_prompt.reshape(B * M, D), g_mem[l], min(256, B * M))
    mk_p = _matmul(mn, w_mem_k[l].astype(BF16), F32, min(256, B * M), MEM_HEAD_DIM, epi=_epi_row_norm,
                   col_extras=(jnp.tile(g_k_mem[l], MEM_N_HEADS).reshape(1, mw).astype(F32),), name="mem_k")
    mv_p = _matmul(mn, w_mem_v[l].astype(BF16), F32, min(256, B * M), MEM_HEAD_DIM, name="mem_v")

    TQ = 2 * CHUNK
    qpos = np.arange(TQ)
    bias_prev_p = _bias_from_table(_bucket_map(qpos, np.arange(TQ) - TQ), rel_bias_table)
    bias_cur_p = _bias_from_table(_bucket_map(qpos, np.arange(TQ)), rel_bias_table)
    C = cache_swa_k.shape[2]
    qpos_s = PAST_LEN + np.arange(Sd)
    bias_prev_s = _bias_from_table(_bucket_map(qpos_s, PAST_LEN - C + np.arange(C)), rel_bias_table)
    bias_cur_s = _bias_from_table(_bucket_map(qpos_s, qpos_s), rel_bias_table)

    conv0 = jnp.zeros((B, SUBLANES, SSD_CONV_DIM), F32)
    h0 = jnp.zeros((B, SSD_D_STATE, SSD_D_INNER), F32)
    yp, conv8_p, ht_p, kv_p = _layer(x_prompt.reshape(B * S, D), B, S, p, conv0, h0,
                                     mk_p.astype(BF16), mv_p.astype(BF16), None,
                                     bias_prev_p, bias_cur_p, CHUNK, TQ)
    conv_prev = jnp.pad(cache_conv[l], ((0, 0), (SUBLANES - (SSD_CONV - 1), 0), (0, 0)))
    h0_s = jnp.transpose(state_ssd[l], (0, 3, 1, 2)).reshape(Bd, SSD_D_STATE, SSD_D_INNER)
    ys, conv8_s, ht_s, kv_s = _layer(x_sample.reshape(Bd * Sd, D), Bd, Sd, p, conv_prev, h0_s,
                                     cache_mem_k[l].reshape(Bd * M, mw).astype(BF16),
                                     cache_mem_v[l].reshape(Bd * M, mw).astype(BF16),
                                     (cache_swa_k[l].reshape(Bd * C, kvw), cache_swa_v[l].reshape(Bd * C, kvw)),
                                     bias_prev_s, bias_cur_s, Sd, Sd)

    keep = min(WINDOW, S)
    kv_p = kv_p.reshape(B, S, 2, SWA_N_KV, SWA_HEAD_DIM)[:, S - keep:]
    kv_s = kv_s.reshape(Bd, Sd, 2, SWA_N_KV, SWA_HEAD_DIM)
    tail = SUBLANES - (SSD_CONV - 1)
    return (yp.reshape(B, S, D), ys.reshape(Bd, Sd, D),
            conv8_p[None, :, tail:], _state_to_heads(ht_p, B)[None],
            kv_p[None, :, :, 0], kv_p[None, :, :, 1],
            mk_p.reshape(1, B, M, MEM_N_HEADS, MEM_HEAD_DIM), mv_p.reshape(1, B, M, MEM_N_HEADS, MEM_HEAD_DIM),
            conv8_s[None, :, tail:], _state_to_heads(ht_s, Bd)[None],
            kv_s[None, :, :, 0], kv_s[None, :, :, 1])
```

```python
import functools
import math

import numpy as np
import jax
import jax.numpy as jnp
from jax import lax
from jax.experimental import pallas as pl
from jax.experimental.pallas import tpu as pltpu

F32 = jnp.float32
BF16 = jnp.bfloat16
EPS = 1e-6
NEG_INF = float("-inf")

CHUNK = 64
SSD_HEAD_DIM = 64
SSD_N_HEADS = 64
SSD_N_GROUPS = 8
SSD_D_STATE = 128
SSD_D_INNER = SSD_N_HEADS * SSD_HEAD_DIM
SSD_GN = SSD_N_GROUPS * SSD_D_STATE
SSD_CONV_DIM = SSD_D_INNER + 2 * SSD_GN
SSD_CONV = 4
SWA_N_HEADS = 32
SWA_N_KV = 4
SWA_HEAD_DIM = 64
SWA_REP = SWA_N_HEADS // SWA_N_KV
WINDOW = 128
WINDOW_CHUNKS = WINDOW // CHUNK
MEM_N_HEADS = 4
MEM_HEAD_DIM = 512
N_BUCKETS = 32
MAX_DISTANCE = 128
N_EXPERT_GROUPS = 4
EXPERTS_PER_GROUP = 16
N_EXPERTS = N_EXPERT_GROUPS * EXPERTS_PER_GROUP
TOP_K = 2
PAST_LEN = 4096

LANES = 128
SUBLANES = 8
VMEM_LIMIT = 56 * 1024 * 1024
MOE_ROWS = 256
MM_ROWS = 1024
MM_COLS = 1024
PAD_HEADS = LANES


def _params(sem, vmem=VMEM_LIMIT):
    return pltpu.CompilerParams(dimension_semantics=sem, vmem_limit_bytes=vmem)


def _sigmoid(x):
    return 0.5 * (jnp.tanh(0.5 * x) + 1.0)


def _split3(x):
    x1 = x.astype(BF16)
    r1 = x - x1.astype(F32)
    x2 = r1.astype(BF16)
    x3 = (r1 - x2.astype(F32)).astype(BF16)
    return x1, x2, x3


def _rmsnorm_kernel(x_ref, g_ref, o_ref):
    x = x_ref[...]
    ms = jnp.mean(x * x, axis=-1, keepdims=True)
    o_ref[...] = (x * lax.rsqrt(ms + EPS) * g_ref[...]).astype(o_ref.dtype)


def _rmsnorm(x, g, tm):
    T, D = x.shape
    return pl.pallas_call(
        _rmsnorm_kernel,
        grid=(T // tm,),
        in_specs=[pl.BlockSpec((tm, D), lambda i: (i, 0)),
                  pl.BlockSpec((1, D), lambda i: (0, 0))],
        out_specs=pl.BlockSpec((tm, D), lambda i: (i, 0)),
        out_shape=jax.ShapeDtypeStruct((T, D), BF16),
        compiler_params=_params(("parallel",)),
        name="rmsnorm",
    )(x, g.reshape(1, D).astype(F32))


def _mm_kernel(a_ref, b_ref, *refs, epi):
    o_ref = refs[-1]
    acc = jnp.dot(a_ref[...], b_ref[...], preferred_element_type=F32)
    if epi is not None:
        acc = epi(acc, *[r[...] for r in refs[:-1]])
    o_ref[...] = acc.astype(o_ref.dtype)


def _matmul(a, b, out_dtype, tm, tn, epi=None, col_extras=(), tile_extras=(), const_extras=(), name="matmul"):
    M, K = a.shape
    N = b.shape[1]
    assert M % tm == 0 and N % tn == 0, (M, N, tm, tn)
    in_specs = [pl.BlockSpec((tm, K), lambda i, j: (i, 0)),
                pl.BlockSpec((K, tn), lambda i, j: (0, j))]
    for _ in col_extras:
        in_specs.append(pl.BlockSpec((1, tn), lambda i, j: (0, j)))
    for _ in tile_extras:
        in_specs.append(pl.BlockSpec((tm, tn), lambda i, j: (i, j)))
    for c in const_extras:
        in_specs.append(pl.BlockSpec(c.shape, lambda i, j: (0, 0)))
    return pl.pallas_call(
        functools.partial(_mm_kernel, epi=epi),
        grid=(M // tm, N // tn),
        in_specs=in_specs,
        out_specs=pl.BlockSpec((tm, tn), lambda i, j: (i, j)),
        out_shape=jax.ShapeDtypeStruct((M, N), out_dtype),
        compiler_params=_params(("parallel", "arbitrary")),
        name=name,
    )(a, b, *col_extras, *tile_extras, *const_extras)


def _group_mean_matrix(width, group):
    idx = np.arange(width) // group
    return jnp.asarray((idx[:, None] == idx[None, :]).astype(np.float32) / group, dtype=BF16)


def _epi_group_norm(acc, gain, mavg):
    w = mavg.shape[0]
    outs = []
    for c in range(acc.shape[1] // w):
        a = acc[:, c * w:(c + 1) * w]
        s = a * a
        hi = s.astype(BF16)
        lo = (s - hi.astype(F32)).astype(BF16)
        ms = (jnp.dot(hi, mavg, preferred_element_type=F32)
              + jnp.dot(lo, mavg, preferred_element_type=F32))
        outs.append(a * lax.rsqrt(ms + EPS))
    normed = outs[0] if len(outs) == 1 else jnp.concatenate(outs, axis=1)
    return normed * gain


def _epi_kv(acc, gain, mavg):
    normed = _epi_group_norm(acc, gain, mavg)
    return jnp.where(pl.program_id(1) == 0, normed, acc)


def _epi_row_norm(acc, gain):
    ms = jnp.mean(acc * acc, axis=-1, keepdims=True)
    return acc * lax.rsqrt(ms + EPS) * gain


def _epi_sigmoid(acc):
    return _sigmoid(acc)


def _epi_residual(acc, res):
    return acc + res


def _ssd_kernel(z_ref, xbc_ref, dt_ref, cprev_ref, h0_ref, cw_ref, cb_ref, dtb_ref, alog_ref,
                dskip_ref, gn_ref, y_ref, hout_ref, cout_ref, xp_s, h_s, conv_s, y_s, *, Lc):
    c = pl.program_id(1)
    n_chunks = pl.num_programs(1)
    P2 = 2 * SSD_HEAD_DIM
    L2 = 2 * Lc

    @pl.when(c == 0)
    def _init():
        xp_s[0:SUBLANES, :] = cprev_ref[0]
        h_s[...] = h0_ref[0]

    xp_s[SUBLANES:SUBLANES + Lc, :] = xbc_ref[...].astype(F32)
    cblk = 512
    for j in range(SSD_CONV_DIM // cblk):
        sl = slice(j * cblk, (j + 1) * cblk)
        acc = cb_ref[:, sl] + cw_ref[0:1, sl] * xp_s[SUBLANES - 3:SUBLANES - 3 + Lc, sl]
        acc = acc + cw_ref[1:2, sl] * xp_s[SUBLANES - 2:SUBLANES - 2 + Lc, sl]
        acc = acc + cw_ref[2:3, sl] * xp_s[SUBLANES - 1:SUBLANES - 1 + Lc, sl]
        acc = acc + cw_ref[3:4, sl] * xp_s[SUBLANES:SUBLANES + Lc, sl]
        conv_s[:, sl] = acc * _sigmoid(acc)
    xp_s[0:SUBLANES, :] = xp_s[Lc:Lc + SUBLANES, :]

    dtv = dt_ref[...] + dtb_ref[...]
    dt = jnp.maximum(dtv, 0.0) + jnp.log1p(jnp.exp(-jnp.abs(dtv)))
    adt = dt * (-jnp.exp(alog_ref[...]))
    row = lax.broadcasted_iota(jnp.int32, (Lc, Lc), 0)
    col = lax.broadcasted_iota(jnp.int32, (Lc, Lc), 1)
    tri = (col <= row).astype(BF16)
    row2 = lax.broadcasted_iota(jnp.int32, (Lc, L2), 0)
    col2 = lax.broadcasted_iota(jnp.int32, (Lc, L2), 1)
    col2m = jnp.where(col2 >= Lc, col2 - Lc, col2)
    tri_t2 = (row2 <= col2m).astype(BF16)
    causal2 = col2m <= row2
    a1, a2, a3 = _split3(adt)
    acs = (jnp.dot(tri, a1, preferred_element_type=F32)
           + jnp.dot(tri, a2, preferred_element_type=F32)
           + jnp.dot(tri, a3, preferred_element_type=F32))
    tdims = (((0,), (0,)), ((), ()))
    acs_t2 = (lax.dot_general(a1, tri_t2, tdims, preferred_element_type=F32)
              + lax.dot_general(a2, tri_t2, tdims, preferred_element_type=F32)
              + lax.dot_general(a3, tri_t2, tdims, preferred_element_type=F32))

    lane_p = lax.broadcasted_iota(jnp.int32, (Lc, P2), 1)
    first_p = lane_p < SSD_HEAD_DIM
    first_l = col2 < Lc
    first_l1 = first_l[0:1, :]
    ndims = (((1,), (1,)), ((), ()))

    for g in range(SSD_N_GROUPS):
        b_g = conv_s[:, SSD_D_INNER + g * SSD_D_STATE:SSD_D_INNER + (g + 1) * SSD_D_STATE].astype(BF16)
        c_g = conv_s[:, SSD_D_INNER + SSD_GN + g * SSD_D_STATE:
                     SSD_D_INNER + SSD_GN + (g + 1) * SSD_D_STATE].astype(BF16)
        b2 = jnp.concatenate([b_g, b_g], axis=0)
        cb2 = lax.dot_general(c_g, b2, ndims, preferred_element_type=F32)
        gw = SSD_HEAD_DIM * (SSD_N_HEADS // SSD_N_GROUPS)
        inter = jnp.dot(c_g, h_s[:, g * gw:(g + 1) * gw].astype(BF16), preferred_element_type=F32)
        for jj in range(gw // P2):
            j = g * (gw // P2) + jj
            sl = slice(j * P2, (j + 1) * P2)
            acs_a = acs[:, 2 * j:2 * j + 1]
            acs_b = acs[:, 2 * j + 1:2 * j + 2]
            col_l = jnp.where(first_l, acs_a, acs_b)
            row_l = jnp.where(first_l1, acs_t2[2 * j:2 * j + 1, :], acs_t2[2 * j + 1:2 * j + 2, :])
            dec = jnp.exp(jnp.where(causal2, col_l - row_l, NEG_INF))
            m_pair = (cb2 * dec).astype(BF16)
            col_p = col_l if L2 == P2 else jnp.where(first_p, acs_a, acs_b)
            dt_p = jnp.where(first_p, dt[:, 2 * j:2 * j + 1], dt[:, 2 * j + 1:2 * j + 2])
            xs_p = conv_s[:, sl]
            xdt = xs_p * dt_p
            rhs = jnp.concatenate([jnp.where(first_p, xdt, 0.0), jnp.where(first_p, 0.0, xdt)],
                                  axis=0).astype(BF16)
            y = jnp.dot(m_pair, rhs, preferred_element_type=F32)
            y = y + inter[:, jj * P2:(jj + 1) * P2] * jnp.exp(col_p) + dskip_ref[:, sl] * xs_p
            y_s[:, sl] = y
            a_end = col_p[Lc - 1:Lc, :]
            xw = (xdt * jnp.exp(a_end - col_p)).astype(BF16)
            h_s[:, sl] = (h_s[:, sl] * jnp.exp(a_end)
                          + lax.dot_general(b_g, xw, tdims, preferred_element_type=F32))

    gdim = SSD_D_INNER // SSD_N_GROUPS
    for g in range(SSD_N_GROUPS):
        sl = slice(g * gdim, (g + 1) * gdim)
        zz = z_ref[:, sl].astype(F32)
        yy = y_s[:, sl] * (zz * _sigmoid(zz))
        ms = jnp.mean(yy * yy, axis=-1, keepdims=True)
        y_ref[:, sl] = (yy * lax.rsqrt(ms + EPS) * gn_ref[:, sl]).astype(y_ref.dtype)

    @pl.when(c == n_chunks - 1)
    def _fin():
        hout_ref[0] = h_s[...]
        cout_ref[0] = xp_s[0:SUBLANES, :]


def _ssd(z, xbc, dt, conv_prev8, h0_t, p, B, L, Lc):
    T = B * L
    nc = L // Lc
    tok = lambda b, c: (b * nc + c, 0)
    per_b = lambda b, c: (b, 0, 0)
    whole = lambda b, c: (0, 0)
    y, h_t, conv8 = pl.pallas_call(
        functools.partial(_ssd_kernel, Lc=Lc),
        grid=(B, nc),
        in_specs=[pl.BlockSpec((Lc, SSD_D_INNER), tok),
                  pl.BlockSpec((Lc, SSD_CONV_DIM), tok),
                  pl.BlockSpec((Lc, PAD_HEADS), tok),
                  pl.BlockSpec((1, SUBLANES, SSD_CONV_DIM), per_b),
                  pl.BlockSpec((1, SSD_D_STATE, SSD_D_INNER), per_b),
                  pl.BlockSpec((SSD_CONV, SSD_CONV_DIM), whole),
                  pl.BlockSpec((1, SSD_CONV_DIM), whole),
                  pl.BlockSpec((1, PAD_HEADS), whole),
                  pl.BlockSpec((1, PAD_HEADS), whole),
                  pl.BlockSpec((1, SSD_D_INNER), whole),
                  pl.BlockSpec((1, SSD_D_INNER), whole)],
        out_specs=[pl.BlockSpec((Lc, SSD_D_INNER), tok),
                   pl.BlockSpec((1, SSD_D_STATE, SSD_D_INNER), per_b),
                   pl.BlockSpec((1, SUBLANES, SSD_CONV_DIM), per_b)],
        out_shape=[jax.ShapeDtypeStruct((T, SSD_D_INNER), BF16),
                   jax.ShapeDtypeStruct((B, SSD_D_STATE, SSD_D_INNER), F32),
                   jax.ShapeDtypeStruct((B, SUBLANES, SSD_CONV_DIM), F32)],
        scratch_shapes=[pltpu.VMEM((SUBLANES + Lc, SSD_CONV_DIM), F32),
                        pltpu.VMEM((SSD_D_STATE, SSD_D_INNER), F32),
                        pltpu.VMEM((Lc, SSD_CONV_DIM), F32),
                        pltpu.VMEM((Lc, SSD_D_INNER), F32)],
        compiler_params=_params(("arbitrary", "arbitrary")),
        name="ssd_scan",
    )(z, xbc, dt, conv_prev8, h0_t, p["conv_w"], p["conv_b"], p["dt_bias"], p["a_log"],
      p["d_skip"], p["g_ssd"])
    return y, h_t, conv8


def _bias_kernel(idx_ref, tab_ref, o_ref):
    h = pl.program_id(0)
    idx = idx_ref[...]
    acc = jnp.full(idx.shape, NEG_INF, F32)
    for b in range(N_BUCKETS):
        acc = jnp.where(idx == b, tab_ref[b, h], acc)
    o_ref[0] = acc


def _bias_from_table(idx, table):
    Q, K = idx.shape
    return pl.pallas_call(
        _bias_kernel,
        grid=(SWA_N_HEADS,),
        in_specs=[pl.BlockSpec((Q, K), lambda h: (0, 0)),
                  pl.BlockSpec(memory_space=pltpu.SMEM)],
        out_specs=pl.BlockSpec((1, Q, K), lambda h: (h, 0, 0)),
        out_shape=jax.ShapeDtypeStruct((SWA_N_HEADS, Q, K), F32),
        compiler_params=_params(("arbitrary",)),
        name="rel_bias",
    )(jnp.asarray(idx, jnp.int32), table.astype(F32))


def _t5_bucket_np(rel):
    nb = N_BUCKETS // 2
    max_exact = nb // 2
    ret = np.where(rel > 0, nb, 0)
    n = np.abs(rel)
    nf = np.maximum(n, 1).astype(np.float32)
    large = max_exact + (np.log(nf / np.float32(max_exact)) / np.float32(math.log(MAX_DISTANCE / max_exact))
                         * np.float32(nb - max_exact)).astype(np.int32)
    large = np.minimum(large, nb - 1)
    return (ret + np.where(n < max_exact, n, large)).astype(np.int32)


def _bucket_map(q_pos, k_pos):
    qc, kc = q_pos // CHUNK, k_pos // CHUNK
    valid = (kc[None, :] >= qc[:, None] - WINDOW_CHUNKS) & (kc[None, :] <= qc[:, None])
    return np.where(valid, _t5_bucket_np(k_pos[None, :] - q_pos[:, None]), -1).astype(np.int32)


def _swa_kernel(q_ref, kp_ref, vp_ref, kc_ref, vc_ref, bias_ref, sink_ref, o_ref, s_scr, p_scr, t_scr,
                *, mask_first):
    TQ = q_ref.shape[0]
    NP = kp_ref.shape[0]
    NK = NP + kc_ref.shape[0]
    HD = SWA_HEAD_DIM
    PW = 2 * HD
    pairs = SWA_REP // 2
    ndims = (((1,), (1,)), ((), ()))
    first_o = lax.broadcasted_iota(jnp.int32, (TQ, PW), 1) < HD
    first_s = lax.broadcasted_iota(jnp.int32, (TQ, 2 * NK), 1) < NK
    zero = jnp.zeros((NK, HD), BF16)
    one = jnp.ones((NK, HD), BF16)
    for g in range(SWA_N_KV):
        ks = slice(g * HD, (g + 1) * HD)
        k_g = jnp.concatenate([kp_ref[:, ks], kc_ref[:, ks]], axis=0).astype(BF16)
        v_g = jnp.concatenate([vp_ref[:, ks], vc_ref[:, ks]], axis=0).astype(BF16)
        kk = jnp.concatenate([jnp.concatenate([k_g, zero], axis=1),
                              jnp.concatenate([zero, k_g], axis=1)], axis=0)
        vv = jnp.concatenate([jnp.concatenate([v_g, zero, one, zero], axis=1),
                              jnp.concatenate([zero, v_g, zero, one], axis=1)], axis=0)
        for pr in range(pairs):
            pidx = g * pairs + pr
            s_scr[pr * TQ:(pr + 1) * TQ, :] = (
                lax.dot_general(q_ref[:, pidx * PW:(pidx + 1) * PW], kk, ndims, preferred_element_type=F32)
                + bias_ref[pidx])
        if mask_first:
            @pl.when(pl.program_id(1) == 0)
            def _mask_prev():
                col = lax.broadcasted_iota(jnp.int32, s_scr.shape, 1)
                s_scr[...] = jnp.where((col & (NK - 1)) < NP, NEG_INF, s_scr[...])
        for pr in range(pairs):
            pidx = g * pairs + pr
            rows = slice(pr * TQ, (pr + 1) * TQ)
            s = s_scr[rows, :]
            sink_a = sink_ref[2 * pidx]
            sink_b = sink_ref[2 * pidx + 1]
            ma = jnp.maximum(jnp.max(s[:, :NK], axis=-1, keepdims=True), sink_a)
            mb = jnp.maximum(jnp.max(s[:, NK:], axis=-1, keepdims=True), sink_b)
            p_scr[rows, :] = jnp.exp(s - jnp.where(first_s, ma, mb)).astype(BF16)
            t_scr[rows, :] = jnp.where(first_o, jnp.exp(sink_a - ma), jnp.exp(sink_b - mb))
        for pr in range(pairs):
            pidx = g * pairs + pr
            rows = slice(pr * TQ, (pr + 1) * TQ)
            ov = jnp.dot(p_scr[rows, :], vv, preferred_element_type=F32)
            o = ov[:, :PW] / (ov[:, PW:] + t_scr[rows, :])
            o_ref[:, pidx * PW:(pidx + 1) * PW] = o.astype(o_ref.dtype)


def _pair_bias(bias_prev, bias_cur):
    full = jnp.concatenate([bias_prev, bias_cur], axis=-1)
    H, Q, NK = full.shape
    return jnp.transpose(full.reshape(H // 2, 2, Q, NK), (0, 2, 1, 3)).reshape(H // 2, Q, 2 * NK)


def _swa(q, k_prev_arr, v_prev_arr, prev_map, k_cur_arr, v_cur_arr, cur_map, n_cur, bias, sinks,
         B, L, TQ, mask_first):
    T = B * L
    nblk = L // TQ
    kvw = SWA_N_KV * SWA_HEAD_DIM
    n_keys = bias.shape[2] // 2
    n_prev = n_keys - n_cur
    assert n_keys & (n_keys - 1) == 0
    pairs = SWA_REP // 2
    return pl.pallas_call(
        functools.partial(_swa_kernel, mask_first=mask_first),
        grid=(B, nblk),
        in_specs=[pl.BlockSpec((TQ, SWA_N_HEADS * SWA_HEAD_DIM), lambda b, i: (b * nblk + i, 0)),
                  pl.BlockSpec((n_prev, kvw), prev_map[0]),
                  pl.BlockSpec((n_prev, kvw), prev_map[1]),
                  pl.BlockSpec((n_cur, kvw), cur_map[0]),
                  pl.BlockSpec((n_cur, kvw), cur_map[1]),
                  pl.BlockSpec(bias.shape, lambda b, i: (0, 0, 0)),
                  pl.BlockSpec(memory_space=pltpu.SMEM)],
        out_specs=pl.BlockSpec((TQ, SWA_N_HEADS * SWA_HEAD_DIM), lambda b, i: (b * nblk + i, 0)),
        out_shape=jax.ShapeDtypeStruct((T, SWA_N_HEADS * SWA_HEAD_DIM), BF16),
        scratch_shapes=[pltpu.VMEM((pairs * TQ, 2 * n_keys), F32),
                        pltpu.VMEM((pairs * TQ, 2 * n_keys), BF16),
                        pltpu.VMEM((pairs * TQ, 2 * SWA_HEAD_DIM), F32)],
        compiler_params=_params(("parallel", "arbitrary")),
        name="swa_attention",
    )(q, k_prev_arr, v_prev_arr, k_cur_arr, v_cur_arr, bias, sinks.astype(F32))


def _mem_kernel(q_ref, mk_ref, mv_ref, o_ref):
    scale = MEM_HEAD_DIM ** -0.5
    ndims = (((1,), (1,)), ((), ()))
    for h in range(MEM_N_HEADS):
        hs = slice(h * MEM_HEAD_DIM, (h + 1) * MEM_HEAD_DIM)
        s = lax.dot_general(q_ref[:, hs], mk_ref[:, hs], ndims, preferred_element_type=F32) * scale
        m = jnp.max(s, axis=-1, keepdims=True)
        p = jnp.exp(s - m)
        den = jnp.sum(p, axis=-1, keepdims=True)
        o = jnp.dot(p.astype(BF16), mv_ref[:, hs], preferred_element_type=F32)
        o_ref[:, hs] = (o / den).astype(o_ref.dtype)


def _mem_attend(q, mk, mv, L, tm):
    T, W = q.shape
    M = mk.shape[0] // (T // L)
    return pl.pallas_call(
        _mem_kernel,
        grid=(T // tm,),
        in_specs=[pl.BlockSpec((tm, W), lambda i: (i, 0)),
                  pl.BlockSpec((M, W), lambda i: ((i * tm) // L, 0)),
                  pl.BlockSpec((M, W), lambda i: ((i * tm) // L, 0))],
        out_specs=pl.BlockSpec((tm, W), lambda i: (i, 0)),
        out_shape=jax.ShapeDtypeStruct((T, W), BF16),
        compiler_params=_params(("parallel",)),
        name="mem_attention",
    )(q, mk, mv)


def _merge_kernel(ys_ref, os_ref, om_ref, w1_ref, w2_ref, w3_ref, g0_ref, g1_ref, g2_ref, o_ref):
    a = jnp.dot(ys_ref[...], w1_ref[...], preferred_element_type=F32)
    b = jnp.dot(os_ref[...], w2_ref[...], preferred_element_type=F32)
    c = jnp.dot(om_ref[...], w3_ref[...], preferred_element_type=F32)
    o = (g0_ref[...].astype(F32) * a + g1_ref[...].astype(F32) * b + g2_ref[...].astype(F32) * c)
    o_ref[...] = o.astype(o_ref.dtype)


def _merge(y_ssd, o_s, o_m, w1, w2, w3, gates, tm, tn):
    T = y_ssd.shape[0]
    D = w1.shape[1]
    nj = D // tn
    row = lambda i, j: (i, 0)
    colw = lambda i, j: (0, j)
    return pl.pallas_call(
        _merge_kernel,
        grid=(T // tm, nj),
        in_specs=[pl.BlockSpec((tm, y_ssd.shape[1]), row),
                  pl.BlockSpec((tm, o_s.shape[1]), row),
                  pl.BlockSpec((tm, o_m.shape[1]), row),
                  pl.BlockSpec((w1.shape[0], tn), colw),
                  pl.BlockSpec((w2.shape[0], tn), colw),
                  pl.BlockSpec((w3.shape[0], tn), colw),
                  pl.BlockSpec((tm, tn), lambda i, j: (i, j)),
                  pl.BlockSpec((tm, tn), lambda i, j: (i, j + nj)),
                  pl.BlockSpec((tm, tn), lambda i, j: (i, j + 2 * nj))],
        out_specs=pl.BlockSpec((tm, tn), lambda i, j: (i, j)),
        out_shape=jax.ShapeDtypeStruct((T, D), BF16),
        compiler_params=_params(("parallel", "arbitrary")),
        name="gated_merge",
    )(y_ssd, o_s, o_m, w1, w2, w3, gates, gates, gates)


def _norm_route_kernel(x_ref, g_ref, whi_ref, wlo_ref, h_ref, r_ref):
    x = x_ref[...]
    ms = jnp.mean(x * x, axis=-1, keepdims=True)
    h = x * lax.rsqrt(ms + EPS) * g_ref[...]
    h_ref[...] = h
    hb = h.astype(BF16)
    lo = (h - hb.astype(F32)).astype(BF16)
    lg = (jnp.dot(hb, whi_ref[...], preferred_element_type=F32)
          + jnp.dot(lo, whi_ref[...], preferred_element_type=F32)
          + jnp.dot(hb, wlo_ref[...], preferred_element_type=F32))
    lane = lax.broadcasted_iota(jnp.int32, lg.shape, 1)
    lane_f = lane.astype(F32)
    far = float(LANES)
    gl = jnp.where((lane >= N_EXPERTS) & (lane < N_EXPERTS + N_EXPERT_GROUPS), lg, NEG_INF)
    gmax = jnp.max(gl, axis=-1, keepdims=True)
    gidx = jnp.min(jnp.where(gl == gmax, lane_f - N_EXPERTS, far), axis=-1, keepdims=True)
    gw = 1.0 / jnp.sum(jnp.exp(gl - gmax), axis=-1, keepdims=True)
    lo_e = gidx * EXPERTS_PER_GROUP
    el = jnp.where((lane_f >= lo_e) & (lane_f < lo_e + EXPERTS_PER_GROUP), lg, NEG_INF)
    v1 = jnp.max(el, axis=-1, keepdims=True)
    i1 = jnp.min(jnp.where(el == v1, lane_f, far), axis=-1, keepdims=True)
    el2 = jnp.where(lane_f == i1, NEG_INF, el)
    v2 = jnp.max(el2, axis=-1, keepdims=True)
    i2 = jnp.min(jnp.where(el2 == v2, lane_f, far), axis=-1, keepdims=True)
    e = jnp.exp(v2 - v1)
    w1 = gw / (1.0 + e)
    w2 = gw * e / (1.0 + e)
    r_ref[...] = jnp.where(lane == 0, w1, jnp.where(lane == 1, w2,
                           jnp.where(lane == 2, i1, jnp.where(lane == 3, i2, 0.0))))


def _norm_route(x, g, w_hi, w_lo, tm):
    T, D = x.shape
    return pl.pallas_call(
        _norm_route_kernel,
        grid=(T // tm,),
        in_specs=[pl.BlockSpec((tm, D), lambda i: (i, 0)),
                  pl.BlockSpec((1, D), lambda i: (0, 0)),
                  pl.BlockSpec((D, LANES), lambda i: (0, 0)),
                  pl.BlockSpec((D, LANES), lambda i: (0, 0))],
        out_specs=[pl.BlockSpec((tm, D), lambda i: (i, 0)),
                   pl.BlockSpec((tm, LANES), lambda i: (i, 0))],
        out_shape=[jax.ShapeDtypeStruct((T, D), F32),
                   jax.ShapeDtypeStruct((T, LANES), F32)],
        compiler_params=_params(("parallel",)),
        name="ffn_norm_router",
    )(x, g.reshape(1, D).astype(F32), w_hi, w_lo)


def _moe_kernel(be_ref, nv_ref, tok_ref, tokn_ref, dst_ref, rw_ref, h_hbm, w1_ref, w3_ref, w2_ref, yu_hbm,
                xbuf, ybuf, w1b, w3b, w2b, sem_in, sem_out):
    i = pl.program_id(0)
    nb = pl.num_programs(0)
    rows = xbuf.shape[1]
    nv = nv_ref[i]
    slot = i % 2
    nxt = jnp.minimum(i + 1, nb - 1)
    next_valid = (i + 1 < nb) & (nv_ref[nxt] > 0)

    def gather(idx_ref, s, start):
        for r in range(rows):
            src = h_hbm.at[pl.ds(idx_ref[0, 0, r] if start else 0, 1), :]
            cp = pltpu.make_async_copy(src, xbuf.at[s, pl.ds(r, 1), :], sem_in.at[s])
            cp.start() if start else cp.wait()

    def scatter(s, start):
        for r in range(rows):
            dst = yu_hbm.at[pl.ds(dst_ref[0, 0, r] if start else 0, 1), :]
            cp = pltpu.make_async_copy(ybuf.at[s, pl.ds(r, 1), :], dst, sem_out.at[s])
            cp.start() if start else cp.wait()

    @pl.when(i == 0)
    def _clear_spare_rows():
        ybuf[...] = jnp.zeros(ybuf.shape, ybuf.dtype)
        base = yu_hbm.shape[0] - 2 * rows
        for s in range(2):
            cp = pltpu.make_async_copy(ybuf.at[s], yu_hbm.at[pl.ds(base + s * rows, rows), :], sem_out.at[s])
            cp.start()
            cp.wait()

    @pl.when((i == 0) & (nv > 0))
    def _prologue():
        gather(tok_ref, 0, True)

    @pl.when(next_valid)
    def _prefetch():
        gather(tokn_ref, 1 - slot, True)

    prev = be_ref[jnp.maximum(i - 1, 0)]

    @pl.when((i == 0) | (be_ref[i] != prev))
    def _load_expert():
        w1b[...] = w1_ref[0].astype(BF16)
        w3b[...] = w3_ref[0].astype(BF16)
        w2b[...] = w2_ref[0].astype(BF16)

    @pl.when(nv > 0)
    def _compute():
        gather(None, slot, False)

        @pl.when(i >= 2)
        def _free_ybuf():
            scatter(slot, False)

        x = xbuf[slot].astype(BF16)
        a = jnp.dot(x, w1b[...], preferred_element_type=F32)
        b = jnp.dot(x, w3b[...], preferred_element_type=F32)
        mid = ((a * _sigmoid(a)) * b).astype(BF16)
        y = jnp.dot(mid, w2b[...], preferred_element_type=F32)
        ybuf[slot] = y * rw_ref[...]
        scatter(slot, True)

        @pl.when(jnp.logical_not(next_valid))
        def _drain():
            scatter(slot, False)

            @pl.when(i >= 1)
            def _drain_prev():
                scatter(1 - slot, False)


def _moe(h2, route, w_gate, w_up, w_down):
    T, D = h2.shape
    F = w_gate.shape[2]
    M = T * TOP_K
    BM = MOE_ROWS
    nb = (M + N_EXPERTS * (BM - 1) + BM - 1) // BM
    P = nb * BM
    w_flat = route[:, 0:TOP_K].reshape(M)
    e_flat = route[:, TOP_K:2 * TOP_K].astype(jnp.int32).reshape(M)
    order = jnp.argsort(e_flat, stable=True).astype(jnp.int32)
    counts = jnp.sum((e_flat[:, None] == jnp.arange(N_EXPERTS, dtype=jnp.int32)[None, :]).astype(jnp.int32), axis=0)
    padded = (counts + BM - 1) // BM * BM
    pad_end = jnp.cumsum(padded)
    pad_start = pad_end - padded
    start = jnp.cumsum(counts) - counts
    blk0 = jnp.arange(nb, dtype=jnp.int32) * BM
    blk_exp = jnp.minimum(jnp.sum((pad_end[None, :] <= blk0[:, None]).astype(jnp.int32), axis=1), N_EXPERTS - 1)
    blk_nv = jnp.clip(counts[blk_exp] - (blk0 - pad_start[blk_exp]), 0, BM).astype(jnp.int32)
    pos = jnp.arange(P, dtype=jnp.int32)
    blk = pos // BM
    e_p = blk_exp[blk]
    off = pos - pad_start[e_p]
    valid = off < counts[e_p]
    m = order[jnp.clip(start[e_p] + off, 0, M - 1)]
    row_tok = jnp.where(valid, m // TOP_K, 0).astype(jnp.int32)
    row_dst = jnp.where(valid, (m % TOP_K) * T + m // TOP_K, M + (blk % 2) * BM + pos % BM).astype(jnp.int32)
    row_w = jnp.where(valid, w_flat[m], 0.0).astype(F32)
    tok3 = row_tok.reshape(nb, 1, BM)

    grid_spec = pltpu.PrefetchScalarGridSpec(
        num_scalar_prefetch=2,
        grid=(nb,),
        in_specs=[pl.BlockSpec((1, 1, BM), lambda i, be, nv: (i, 0, 0), memory_space=pltpu.SMEM),
                  pl.BlockSpec((1, 1, BM), lambda i, be, nv: (jnp.minimum(i + 1, nb - 1), 0, 0),
                               memory_space=pltpu.SMEM),
                  pl.BlockSpec((1, 1, BM), lambda i, be, nv: (i, 0, 0), memory_space=pltpu.SMEM),
                  pl.BlockSpec((BM, 1), lambda i, be, nv: (i, 0)),
                  pl.BlockSpec(memory_space=pl.ANY),
                  pl.BlockSpec((1, D, F), lambda i, be, nv: (be[i], 0, 0)),
                  pl.BlockSpec((1, D, F), lambda i, be, nv: (be[i], 0, 0)),
                  pl.BlockSpec((1, F, D), lambda i, be, nv: (be[i], 0, 0))],
        out_specs=pl.BlockSpec(memory_space=pl.ANY),
        scratch_shapes=[pltpu.VMEM((2, BM, D), F32),
                        pltpu.VMEM((2, BM, D), F32),
                        pltpu.VMEM((D, F), BF16),
                        pltpu.VMEM((D, F), BF16),
                        pltpu.VMEM((F, D), BF16),
                        pltpu.SemaphoreType.DMA((2,)),
                        pltpu.SemaphoreType.DMA((2,))],
    )
    return pl.pallas_call(
        _moe_kernel,
        grid_spec=grid_spec,
        out_shape=jax.ShapeDtypeStruct((M + 2 * BM, D), F32),
        compiler_params=_params(("arbitrary",)),
        name="moe_experts",
    )(blk_exp, blk_nv, tok3, tok3, row_dst.reshape(nb, 1, BM), row_w.reshape(P, 1),
      h2, w_gate, w_up, w_down)


def _combine_kernel(x_ref, y0_ref, y1_ref, o_ref):
    o_ref[...] = x_ref[...] + (y0_ref[...] + y1_ref[...])


def _combine(x, yu, tm):
    T, D = x.shape
    return pl.pallas_call(
        _combine_kernel,
        grid=(T // tm,),
        in_specs=[pl.BlockSpec((tm, D), lambda i: (i, 0)),
                  pl.BlockSpec((tm, D), lambda i: (i, 0)),
                  pl.BlockSpec((tm, D), lambda i: (i + T // tm, 0))],
        out_specs=pl.BlockSpec((tm, D), lambda i: (i, 0)),
        out_shape=jax.ShapeDtypeStruct((T, D), F32),
        compiler_params=_params(("parallel",)),
        name="moe_combine",
    )(x, yu, yu)


def _layer(x, B, L, p, conv_prev8, h0_t, mk, mv, swa_prev, swa_bias, Lc, TQ):
    T, D = x.shape
    tm = min(512, T)
    tmm = min(MM_ROWS, T)
    h = _rmsnorm(x, p["g_mix"], tm)
    z = _matmul(h, p["wz"], BF16, tmm, MM_COLS, name="proj_z")
    xbc = _matmul(h, p["wxbc"], BF16, tmm, MM_COLS, name="proj_xbc")
    dt = _matmul(h, p["wdt"], F32, tmm, PAD_HEADS, name="proj_dt")
    q_s = _matmul(h, p["wqs"], BF16, tmm, 512, epi=_epi_group_norm, col_extras=(p["g_q_swa"],),
                  const_extras=(p["mavg"],), name="proj_q_swa")
    kv = _matmul(h, p["wkv"], F32, tmm, 256, epi=_epi_kv, col_extras=(p["g_kv"],),
                 const_extras=(p["mavg"],), name="proj_kv_swa")
    q_m = _matmul(h, p["wqm"], BF16, tmm, MEM_HEAD_DIM, epi=_epi_row_norm, col_extras=(p["g_q_mem"],),
                  name="proj_q_mem")
    gates = _matmul(h, p["wg"], BF16, tmm, MM_COLS, epi=_epi_sigmoid, name="proj_gates")

    y_ssd, h_t, conv8 = _ssd(z, xbc, dt, conv_prev8, h0_t, p, B, L, Lc)

    nblk = L // TQ
    kvw = SWA_N_KV * SWA_HEAD_DIM
    if swa_prev is None:
        prev_map = (lambda b, i: (b * nblk + jnp.maximum(i - 1, 0), 0),
                    lambda b, i: (b * nblk + jnp.maximum(i - 1, 0), 1))
        cur_map = (lambda b, i: (b * nblk + i, 0), lambda b, i: (b * nblk + i, 1))
        o_s = _swa(q_s, kv, kv, prev_map, kv, kv, cur_map, TQ, swa_bias, p["sinks"], B, L, TQ, True)
    else:
        n_cur = swa_bias.shape[2] // 2 - swa_prev[0].shape[0] // B
        kv3 = jnp.pad(kv.reshape(B, L, 2 * kvw), ((0, 0), (0, n_cur - L), (0, 0)))
        k_new = kv3[:, :, :kvw].reshape(B * n_cur, kvw)
        v_new = kv3[:, :, kvw:].reshape(B * n_cur, kvw)
        per_b = (lambda b, i: (b, 0), lambda b, i: (b, 0))
        o_s = _swa(q_s, swa_prev[0], swa_prev[1], per_b, k_new, v_new, per_b, n_cur, swa_bias, p["sinks"],
                   B, L, TQ, False)

    o_m = _mem_attend(q_m, mk, mv, L, min(256, L))

    merged = _merge(y_ssd, o_s, o_m, p["w_o_ssd"], p["w_o_swa"], p["w_o_mem"], gates, tm, 512)
    x1 = _matmul(merged, p["w_out"], F32, tmm, MM_COLS, epi=_epi_residual, tile_extras=(x,), name="proj_out")

    h2, route = _norm_route(x1, p["g_ffn"], p["w_r_hi"], p["w_r_lo"], min(256, T))
    yu = _moe(h2, route, p["w_exp_gate"], p["w_exp_up"], p["w_exp_down"])
    y = _combine(x1, yu, tm)
    return y, conv8, h_t, kv


def _state_to_heads(h_t, B):
    return jnp.transpose(h_t.reshape(B, SSD_D_STATE, SSD_N_HEADS, SSD_HEAD_DIM), (0, 2, 3, 1))


def kernel(x_prompt, x_sample, cache_conv, state_ssd, cache_swa_k, cache_swa_v, cache_mem_k, cache_mem_v, mem_prompt, rel_bias_table, g_mix, w_in, conv_w, conv_b, dt_bias, a_log, d_skip, g_ssd, w_o_ssd, g_q_swa, g_k_swa, sinks, w_o_swa, g_mem, w_mem_k, w_mem_v, g_q_mem, g_k_mem, w_o_mem, w_out, g_ffn, w_router_grp, w_router_exp, w_exp_gate, w_exp_up, w_exp_down):
    B, S, D = x_prompt.shape
    Bd, Sd, _ = x_sample.shape
    depth = w_in.shape[0]
    assert depth == 1
    l = 0
    kvw = SWA_N_KV * SWA_HEAD_DIM
    qw = SWA_N_HEADS * SWA_HEAD_DIM
    mw = MEM_N_HEADS * MEM_HEAD_DIM

    sizes = (SSD_D_INNER, SSD_CONV_DIM, SSD_N_HEADS, qw, kvw, kvw, mw, 3 * D)
    offs = np.concatenate([[0], np.cumsum(sizes)])
    w = w_in[l]
    cols = [w[:, int(offs[k]):int(offs[k + 1])] for k in range(len(sizes))]
    pad_h = PAD_HEADS - SSD_N_HEADS
    w_r = jnp.pad(jnp.concatenate([w_router_exp[l], w_router_grp[l]], axis=1),
                  ((0, 0), (0, LANES - N_EXPERTS - N_EXPERT_GROUPS)))
    w_r_hi = w_r.astype(BF16)
    p = {
        "g_mix": g_mix[l],
        "wz": cols[0].astype(BF16),
        "wxbc": cols[1].astype(BF16),
        "wdt": jnp.pad(cols[2], ((0, 0), (0, pad_h))).astype(BF16),
        "wqs": cols[3].astype(BF16),
        "wkv": jnp.concatenate([cols[4], cols[5]], axis=1).astype(BF16),
        "wqm": cols[6].astype(BF16),
        "wg": cols[7].astype(BF16),
        "conv_w": conv_w[l].astype(F32),
        "conv_b": conv_b[l].reshape(1, SSD_CONV_DIM).astype(F32),
        "dt_bias": jnp.pad(dt_bias[l], (0, pad_h)).reshape(1, PAD_HEADS).astype(F32),
        "a_log": jnp.pad(a_log[l], (0, pad_h)).reshape(1, PAD_HEADS).astype(F32),
        "d_skip": jnp.repeat(d_skip[l], SSD_HEAD_DIM).reshape(1, SSD_D_INNER).astype(F32),
        "g_ssd": g_ssd[l].reshape(1, SSD_D_INNER).astype(F32),
        "g_q_swa": (jnp.tile(g_q_swa[l], SWA_N_HEADS) * SWA_HEAD_DIM ** -0.5).reshape(1, qw).astype(F32),
        "g_kv": jnp.concatenate([jnp.tile(g_k_swa[l], SWA_N_KV), jnp.ones((kvw,), F32)]).reshape(1, 2 * kvw),
        "g_q_mem": jnp.tile(g_q_mem[l], MEM_N_HEADS).reshape(1, mw).astype(F32),
        "mavg": _group_mean_matrix(256, SWA_HEAD_DIM),
        "sinks": sinks[l],
        "w_o_ssd": w_o_ssd[l].astype(BF16),
        "w_o_swa": w_o_swa[l].astype(BF16),
        "w_o_mem": w_o_mem[l].astype(BF16),
        "w_out": w_out[l].astype(BF16),
        "g_ffn": g_ffn[l],
        "w_r_hi": w_r_hi,
        "w_r_lo": (w_r - w_r_hi.astype(F32)).astype(BF16),
        "w_exp_gate": w_exp_gate[l],
        "w_exp_up": w_exp_up[l],
        "w_exp_down": w_exp_down[l],
    }

    M = mem_prompt.shape[1]
    mn = _rmsnorm(mem_prompt.reshape(B * M, D), g_mem[l], min(256, B * M))
    mk_p = _matmul(mn, w_mem_k[l].astype(BF16), F32, min(256, B * M), MEM_HEAD_DIM, epi=_epi_row_norm,
                   col_extras=(jnp.tile(g_k_mem[l], MEM_N_HEADS).reshape(1, mw).astype(F32),), name="mem_k")
    mv_p = _matmul(mn, w_mem_v[l].astype(BF16), F32, min(256, B * M), MEM_HEAD_DIM, name="mem_v")

    TQ = 2 * CHUNK
    qpos = np.arange(TQ)
    bias_p = _pair_bias(_bias_from_table(_bucket_map(qpos, np.arange(TQ) - TQ), rel_bias_table),
                        _bias_from_table(_bucket_map(qpos, np.arange(TQ)), rel_bias_table))
    C = cache_swa_k.shape[2]
    qpos_s = PAST_LEN + np.arange(Sd)
    cur_map_s = _bucket_map(qpos_s, PAST_LEN + np.arange(C))
    cur_map_s[:, Sd:] = -1
    bias_s = _pair_bias(_bias_from_table(_bucket_map(qpos_s, PAST_LEN - C + np.arange(C)), rel_bias_table),
                        _bias_from_table(cur_map_s, rel_bias_table))

    conv0 = jnp.zeros((B, SUBLANES, SSD_CONV_DIM), F32)
    h0 = jnp.zeros((B, SSD_D_STATE, SSD_D_INNER), F32)
    yp, conv8_p, ht_p, kv_p = _layer(x_prompt.reshape(B * S, D), B, S, p, conv0, h0,
                                     mk_p.astype(BF16), mv_p.astype(BF16), None,
                                     bias_p, CHUNK, TQ)
    conv_prev = jnp.pad(cache_conv[l], ((0, 0), (SUBLANES - (SSD_CONV - 1), 0), (0, 0)))
    h0_s = jnp.transpose(state_ssd[l], (0, 3, 1, 2)).reshape(Bd, SSD_D_STATE, SSD_D_INNER)
    ys, conv8_s, ht_s, kv_s = _layer(x_sample.reshape(Bd * Sd, D), Bd, Sd, p, conv_prev, h0_s,
                                     cache_mem_k[l].reshape(Bd * M, mw).astype(BF16),
                                     cache_mem_v[l].reshape(Bd * M, mw).astype(BF16),
                                     (cache_swa_k[l].reshape(Bd * C, kvw), cache_swa_v[l].reshape(Bd * C, kvw)),
                                     bias_s, Sd, Sd)

    keep = min(WINDOW, S)
    kv_p = kv_p.reshape(B, S, 2, SWA_N_KV, SWA_HEAD_DIM)[:, S - keep:]
    kv_s = kv_s.reshape(Bd, Sd, 2, SWA_N_KV, SWA_HEAD_DIM)
    tail = SUBLANES - (SSD_CONV - 1)
    return (yp.reshape(B, S, D), ys.reshape(Bd, Sd, D),
            conv8_p[None, :, tail:], _state_to_heads(ht_p, B)[None],
            kv_p[None, :, :, 0], kv_p[None, :, :, 1],
            mk_p.reshape(1, B, M, MEM_N_HEADS, MEM_HEAD_DIM), mv_p.reshape(1, B, M, MEM_N_HEADS, MEM_HEAD_DIM),
            conv8_s[None, :, tail:], _state_to_heads(ht_s, Bd)[None],
            kv_s[None, :, :, 0], kv_s[None, :, :, 1])
```

```python
import functools
import math

import numpy as np
import jax
import jax.numpy as jnp
from jax import lax
from jax.experimental import pallas as pl
from jax.experimental.pallas import tpu as pltpu

F32 = jnp.float32
BF16 = jnp.bfloat16
EPS = 1e-6
NEG_INF = float("-inf")

CHUNK = 64
SSD_HEAD_DIM = 64
SSD_N_HEADS = 64
SSD_N_GROUPS = 8
SSD_D_STATE = 128
SSD_D_INNER = SSD_N_HEADS * SSD_HEAD_DIM
SSD_GN = SSD_N_GROUPS * SSD_D_STATE
SSD_CONV_DIM = SSD_D_INNER + 2 * SSD_GN
SSD_CONV = 4
SWA_N_HEADS = 32
SWA_N_KV = 4
SWA_HEAD_DIM = 64
SWA_REP = SWA_N_HEADS // SWA_N_KV
WINDOW = 128
WINDOW_CHUNKS = WINDOW // CHUNK
MEM_N_HEADS = 4
MEM_HEAD_DIM = 512
N_BUCKETS = 32
MAX_DISTANCE = 128
N_EXPERT_GROUPS = 4
EXPERTS_PER_GROUP = 16
N_EXPERTS = N_EXPERT_GROUPS * EXPERTS_PER_GROUP
TOP_K = 2
PAST_LEN = 4096

LANES = 128
SUBLANES = 8
VMEM_LIMIT = 56 * 1024 * 1024
MOE_ROWS = 256
MM_ROWS = 1024
MM_COLS = 1024
PAD_HEADS = LANES


def _params(sem, vmem=VMEM_LIMIT):
    return pltpu.CompilerParams(dimension_semantics=sem, vmem_limit_bytes=vmem)


def _sigmoid(x):
    return 0.5 * (jnp.tanh(0.5 * x) + 1.0)


def _split3(x):
    x1 = x.astype(BF16)
    r1 = x - x1.astype(F32)
    x2 = r1.astype(BF16)
    x3 = (r1 - x2.astype(F32)).astype(BF16)
    return x1, x2, x3


def _rmsnorm_kernel(x_ref, g_ref, o_ref):
    x = x_ref[...]
    ms = jnp.mean(x * x, axis=-1, keepdims=True)
    o_ref[...] = (x * lax.rsqrt(ms + EPS) * g_ref[...]).astype(o_ref.dtype)


def _rmsnorm(x, g, tm):
    T, D = x.shape
    return pl.pallas_call(
        _rmsnorm_kernel,
        grid=(T // tm,),
        in_specs=[pl.BlockSpec((tm, D), lambda i: (i, 0)),
                  pl.BlockSpec((1, D), lambda i: (0, 0))],
        out_specs=pl.BlockSpec((tm, D), lambda i: (i, 0)),
        out_shape=jax.ShapeDtypeStruct((T, D), BF16),
        compiler_params=_params(("parallel",)),
        name="rmsnorm",
    )(x, g.reshape(1, D).astype(F32))


def _mm_kernel(a_ref, b_ref, *refs, epi):
    o_ref = refs[-1]
    acc = jnp.dot(a_ref[...], b_ref[...], preferred_element_type=F32)
    if epi is not None:
        acc = epi(acc, *[r[...] for r in refs[:-1]])
    o_ref[...] = acc.astype(o_ref.dtype)


def _matmul(a, b, out_dtype, tm, tn, epi=None, col_extras=(), tile_extras=(), const_extras=(), name="matmul"):
    M, K = a.shape
    N = b.shape[1]
    assert M % tm == 0 and N % tn == 0, (M, N, tm, tn)
    in_specs = [pl.BlockSpec((tm, K), lambda i, j: (i, 0)),
                pl.BlockSpec((K, tn), lambda i, j: (0, j))]
    for _ in col_extras:
        in_specs.append(pl.BlockSpec((1, tn), lambda i, j: (0, j)))
    for _ in tile_extras:
        in_specs.append(pl.BlockSpec((tm, tn), lambda i, j: (i, j)))
    for c in const_extras:
        in_specs.append(pl.BlockSpec(c.shape, lambda i, j: (0, 0)))
    return pl.pallas_call(
        functools.partial(_mm_kernel, epi=epi),
        grid=(M // tm, N // tn),
        in_specs=in_specs,
        out_specs=pl.BlockSpec((tm, tn), lambda i, j: (i, j)),
        out_shape=jax.ShapeDtypeStruct((M, N), out_dtype),
        compiler_params=_params(("parallel", "arbitrary")),
        name=name,
    )(a, b, *col_extras, *tile_extras, *const_extras)


def _group_mean_matrix(width, group):
    idx = np.arange(width) // group
    return jnp.asarray((idx[:, None] == idx[None, :]).astype(np.float32) / group, dtype=BF16)


def _epi_group_norm(acc, gain, mavg):
    w = mavg.shape[0]
    outs = []
    for c in range(acc.shape[1] // w):
        a = acc[:, c * w:(c + 1) * w]
        s = a * a
        hi = s.astype(BF16)
        lo = (s - hi.astype(F32)).astype(BF16)
        ms = (jnp.dot(hi, mavg, preferred_element_type=F32)
              + jnp.dot(lo, mavg, preferred_element_type=F32))
        outs.append(a * lax.rsqrt(ms + EPS))
    normed = outs[0] if len(outs) == 1 else jnp.concatenate(outs, axis=1)
    return normed * gain


def _epi_kv(acc, gain, mavg):
    normed = _epi_group_norm(acc, gain, mavg)
    return jnp.where(pl.program_id(1) == 0, normed, acc)


def _epi_row_norm(acc, gain):
    ms = jnp.mean(acc * acc, axis=-1, keepdims=True)
    return acc * lax.rsqrt(ms + EPS) * gain


def _epi_sigmoid(acc):
    return _sigmoid(acc)


def _epi_residual(acc, res):
    return acc + res


def _ssd_kernel(z_ref, xbc_ref, dt_ref, cprev_ref, h0_ref, cw_ref, cb_ref, dtb_ref, alog_ref,
                dskip_ref, gn_ref, y_ref, hout_ref, cout_ref, xp_s, h_s, conv_s, y_s, *, Lc):
    c = pl.program_id(1)
    n_chunks = pl.num_programs(1)
    P2 = 2 * SSD_HEAD_DIM
    L2 = 2 * Lc

    @pl.when(c == 0)
    def _init():
        xp_s[0:SUBLANES, :] = cprev_ref[0]
        h_s[...] = h0_ref[0]

    xp_s[SUBLANES:SUBLANES + Lc, :] = xbc_ref[...].astype(F32)
    cblk = 512
    for j in range(SSD_CONV_DIM // cblk):
        sl = slice(j * cblk, (j + 1) * cblk)
        acc = cb_ref[:, sl] + cw_ref[0:1, sl] * xp_s[SUBLANES - 3:SUBLANES - 3 + Lc, sl]
        acc = acc + cw_ref[1:2, sl] * xp_s[SUBLANES - 2:SUBLANES - 2 + Lc, sl]
        acc = acc + cw_ref[2:3, sl] * xp_s[SUBLANES - 1:SUBLANES - 1 + Lc, sl]
        acc = acc + cw_ref[3:4, sl] * xp_s[SUBLANES:SUBLANES + Lc, sl]
        conv_s[:, sl] = acc * _sigmoid(acc)
    xp_s[0:SUBLANES, :] = xp_s[Lc:Lc + SUBLANES, :]

    dtv = dt_ref[...] + dtb_ref[...]
    dt = jnp.maximum(dtv, 0.0) + jnp.log1p(jnp.exp(-jnp.abs(dtv)))
    adt = dt * (-jnp.exp(alog_ref[...]))
    row = lax.broadcasted_iota(jnp.int32, (Lc, Lc), 0)
    col = lax.broadcasted_iota(jnp.int32, (Lc, Lc), 1)
    tri = (col <= row).astype(BF16)
    row2 = lax.broadcasted_iota(jnp.int32, (Lc, L2), 0)
    col2 = lax.broadcasted_iota(jnp.int32, (Lc, L2), 1)
    col2m = jnp.where(col2 >= Lc, col2 - Lc, col2)
    tri_t2 = (row2 <= col2m).astype(BF16)
    causal2 = col2m <= row2
    a1, a2, a3 = _split3(adt)
    acs = (jnp.dot(tri, a1, preferred_element_type=F32)
           + jnp.dot(tri, a2, preferred_element_type=F32)
           + jnp.dot(tri, a3, preferred_element_type=F32))
    tdims = (((0,), (0,)), ((), ()))
    acs_t2 = (lax.dot_general(a1, tri_t2, tdims, preferred_element_type=F32)
              + lax.dot_general(a2, tri_t2, tdims, preferred_element_type=F32)
              + lax.dot_general(a3, tri_t2, tdims, preferred_element_type=F32))

    lane_p = lax.broadcasted_iota(jnp.int32, (Lc, P2), 1)
    first_p = lane_p < SSD_HEAD_DIM
    first_l = col2 < Lc
    first_l1 = first_l[0:1, :]
    ndims = (((1,), (1,)), ((), ()))

    for g in range(SSD_N_GROUPS):
        b_g = conv_s[:, SSD_D_INNER + g * SSD_D_STATE:SSD_D_INNER + (g + 1) * SSD_D_STATE].astype(BF16)
        c_g = conv_s[:, SSD_D_INNER + SSD_GN + g * SSD_D_STATE:
                     SSD_D_INNER + SSD_GN + (g + 1) * SSD_D_STATE].astype(BF16)
        b2 = jnp.concatenate([b_g, b_g], axis=0)
        cb2 = lax.dot_general(c_g, b2, ndims, preferred_element_type=F32)
        gw = SSD_HEAD_DIM * (SSD_N_HEADS // SSD_N_GROUPS)
        inter = jnp.dot(c_g, h_s[:, g * gw:(g + 1) * gw].astype(BF16), preferred_element_type=F32)
        for jj in range(gw // P2):
            j = g * (gw // P2) + jj
            sl = slice(j * P2, (j + 1) * P2)
            acs_a = acs[:, 2 * j:2 * j + 1]
            acs_b = acs[:, 2 * j + 1:2 * j + 2]
            col_l = jnp.where(first_l, acs_a, acs_b)
            row_l = jnp.where(first_l1, acs_t2[2 * j:2 * j + 1, :], acs_t2[2 * j + 1:2 * j + 2, :])
            dec = jnp.exp(jnp.where(causal2, col_l - row_l, NEG_INF))
            m_pair = (cb2 * dec).astype(BF16)
            col_p = col_l if L2 == P2 else jnp.where(first_p, acs_a, acs_b)
            dt_p = jnp.where(first_p, dt[:, 2 * j:2 * j + 1], dt[:, 2 * j + 1:2 * j + 2])
            xs_p = conv_s[:, sl]
            xdt = xs_p * dt_p
            rhs = jnp.concatenate([jnp.where(first_p, xdt, 0.0), jnp.where(first_p, 0.0, xdt)],
                                  axis=0).astype(BF16)
            y = jnp.dot(m_pair, rhs, preferred_element_type=F32)
            y = y + inter[:, jj * P2:(jj + 1) * P2] * jnp.exp(col_p) + dskip_ref[:, sl] * xs_p
            y_s[:, sl] = y
            a_end = col_p[Lc - 1:Lc, :]
            xw = (xdt * jnp.exp(a_end - col_p)).astype(BF16)
            h_s[:, sl] = (h_s[:, sl] * jnp.exp(a_end)
                          + lax.dot_general(b_g, xw, tdims, preferred_element_type=F32))

    gdim = SSD_D_INNER // SSD_N_GROUPS
    for g in range(SSD_N_GROUPS):
        sl = slice(g * gdim, (g + 1) * gdim)
        zz = z_ref[:, sl].astype(F32)
        yy = y_s[:, sl] * (zz * _sigmoid(zz))
        ms = jnp.mean(yy * yy, axis=-1, keepdims=True)
        y_ref[:, sl] = (yy * lax.rsqrt(ms + EPS) * gn_ref[:, sl]).astype(y_ref.dtype)

    @pl.when(c == n_chunks - 1)
    def _fin():
        hout_ref[0] = h_s[...]
        cout_ref[0] = xp_s[0:SUBLANES, :]


def _ssd(z, xbc, dt, conv_prev8, h0_t, p, B, L, Lc):
    T = B * L
    nc = L // Lc
    tok = lambda b, c: (b * nc + c, 0)
    per_b = lambda b, c: (b, 0, 0)
    whole = lambda b, c: (0, 0)
    y, h_t, conv8 = pl.pallas_call(
        functools.partial(_ssd_kernel, Lc=Lc),
        grid=(B, nc),
        in_specs=[pl.BlockSpec((Lc, SSD_D_INNER), tok),
                  pl.BlockSpec((Lc, SSD_CONV_DIM), tok),
                  pl.BlockSpec((Lc, PAD_HEADS), tok),
                  pl.BlockSpec((1, SUBLANES, SSD_CONV_DIM), per_b),
                  pl.BlockSpec((1, SSD_D_STATE, SSD_D_INNER), per_b),
                  pl.BlockSpec((SSD_CONV, SSD_CONV_DIM), whole),
                  pl.BlockSpec((1, SSD_CONV_DIM), whole),
                  pl.BlockSpec((1, PAD_HEADS), whole),
                  pl.BlockSpec((1, PAD_HEADS), whole),
                  pl.BlockSpec((1, SSD_D_INNER), whole),
                  pl.BlockSpec((1, SSD_D_INNER), whole)],
        out_specs=[pl.BlockSpec((Lc, SSD_D_INNER), tok),
                   pl.BlockSpec((1, SSD_D_STATE, SSD_D_INNER), per_b),
                   pl.BlockSpec((1, SUBLANES, SSD_CONV_DIM), per_b)],
        out_shape=[jax.ShapeDtypeStruct((T, SSD_D_INNER), BF16),
                   jax.ShapeDtypeStruct((B, SSD_D_STATE, SSD_D_INNER), F32),
                   jax.ShapeDtypeStruct((B, SUBLANES, SSD_CONV_DIM), F32)],
        scratch_shapes=[pltpu.VMEM((SUBLANES + Lc, SSD_CONV_DIM), F32),
                        pltpu.VMEM((SSD_D_STATE, SSD_D_INNER), F32),
                        pltpu.VMEM((Lc, SSD_CONV_DIM), F32),
                        pltpu.VMEM((Lc, SSD_D_INNER), F32)],
        compiler_params=_params(("arbitrary", "arbitrary")),
        name="ssd_scan",
    )(z, xbc, dt, conv_prev8, h0_t, p["conv_w"], p["conv_b"], p["dt_bias"], p["a_log"],
      p["d_skip"], p["g_ssd"])
    return y, h_t, conv8


def _bias_kernel(idx_ref, tab_ref, o_ref):
    h = pl.program_id(0)
    idx = idx_ref[...]
    acc = jnp.full(idx.shape, NEG_INF, F32)
    for b in range(N_BUCKETS):
        acc = jnp.where(idx == b, tab_ref[b, h], acc)
    o_ref[0] = acc


def _bias_from_table(idx, table):
    Q, K = idx.shape
    return pl.pallas_call(
        _bias_kernel,
        grid=(SWA_N_HEADS,),
        in_specs=[pl.BlockSpec((Q, K), lambda h: (0, 0)),
                  pl.BlockSpec(memory_space=pltpu.SMEM)],
        out_specs=pl.BlockSpec((1, Q, K), lambda h: (h, 0, 0)),
        out_shape=jax.ShapeDtypeStruct((SWA_N_HEADS, Q, K), F32),
        compiler_params=_params(("arbitrary",)),
        name="rel_bias",
    )(jnp.asarray(idx, jnp.int32), table.astype(F32))


def _t5_bucket_np(rel):
    nb = N_BUCKETS // 2
    max_exact = nb // 2
    ret = np.where(rel > 0, nb, 0)
    n = np.abs(rel)
    nf = np.maximum(n, 1).astype(np.float32)
    large = max_exact + (np.log(nf / np.float32(max_exact)) / np.float32(math.log(MAX_DISTANCE / max_exact))
                         * np.float32(nb - max_exact)).astype(np.int32)
    large = np.minimum(large, nb - 1)
    return (ret + np.where(n < max_exact, n, large)).astype(np.int32)


def _bucket_map(q_pos, k_pos):
    qc, kc = q_pos // CHUNK, k_pos // CHUNK
    valid = (kc[None, :] >= qc[:, None] - WINDOW_CHUNKS) & (kc[None, :] <= qc[:, None])
    return np.where(valid, _t5_bucket_np(k_pos[None, :] - q_pos[:, None]), -1).astype(np.int32)


def _swa_kernel(q_ref, kp_ref, vp_ref, kc_ref, vc_ref, bias_ref, sink_ref, o_ref, s_scr, p_scr, t_scr,
                *, mask_first):
    TQ = q_ref.shape[0]
    NP = kp_ref.shape[0]
    NK = NP + kc_ref.shape[0]
    HD = SWA_HEAD_DIM
    PW = 2 * HD
    pairs = SWA_REP // 2
    ndims = (((1,), (1,)), ((), ()))
    first_o = lax.broadcasted_iota(jnp.int32, (TQ, PW), 1) < HD
    first_s = lax.broadcasted_iota(jnp.int32, (TQ, 2 * NK), 1) < NK
    zero = jnp.zeros((NK, HD), BF16)
    one = jnp.ones((NK, HD), BF16)
    for g in range(SWA_N_KV):
        ks = slice(g * HD, (g + 1) * HD)
        k_g = jnp.concatenate([kp_ref[:, ks], kc_ref[:, ks]], axis=0).astype(BF16)
        v_g = jnp.concatenate([vp_ref[:, ks], vc_ref[:, ks]], axis=0).astype(BF16)
        kk = jnp.concatenate([jnp.concatenate([k_g, zero], axis=1),
                              jnp.concatenate([zero, k_g], axis=1)], axis=0)
        vv = jnp.concatenate([jnp.concatenate([v_g, zero, one, zero], axis=1),
                              jnp.concatenate([zero, v_g, zero, one], axis=1)], axis=0)
        for pr in range(pairs):
            pidx = g * pairs + pr
            s_scr[pr * TQ:(pr + 1) * TQ, :] = (
                lax.dot_general(q_ref[:, pidx * PW:(pidx + 1) * PW], kk, ndims, preferred_element_type=F32)
                + bias_ref[pidx])
        if mask_first:
            @pl.when(pl.program_id(1) == 0)
            def _mask_prev():
                col = lax.broadcasted_iota(jnp.int32, s_scr.shape, 1)
                s_scr[...] = jnp.where((col & (NK - 1)) < NP, NEG_INF, s_scr[...])
        for pr in range(pairs):
            pidx = g * pairs + pr
            rows = slice(pr * TQ, (pr + 1) * TQ)
            s = s_scr[rows, :]
            sink_a = sink_ref[2 * pidx]
            sink_b = sink_ref[2 * pidx + 1]
            ma = jnp.maximum(jnp.max(s[:, :NK], axis=-1, keepdims=True), sink_a)
            mb = jnp.maximum(jnp.max(s[:, NK:], axis=-1, keepdims=True), sink_b)
            p_scr[rows, :] = jnp.exp(s - jnp.where(first_s, ma, mb)).astype(BF16)
            t_scr[rows, :] = jnp.where(first_o, jnp.exp(sink_a - ma), jnp.exp(sink_b - mb))
        for pr in range(pairs):
            pidx = g * pairs + pr
            rows = slice(pr * TQ, (pr + 1) * TQ)
            ov = jnp.dot(p_scr[rows, :], vv, preferred_element_type=F32)
            o = ov[:, :PW] / (ov[:, PW:] + t_scr[rows, :])
            o_ref[:, pidx * PW:(pidx + 1) * PW] = o.astype(o_ref.dtype)


def _pair_bias(bias_prev, bias_cur):
    full = jnp.concatenate([bias_prev, bias_cur], axis=-1)
    H, Q, NK = full.shape
    return jnp.transpose(full.reshape(H // 2, 2, Q, NK), (0, 2, 1, 3)).reshape(H // 2, Q, 2 * NK)


def _swa(q, k_prev_arr, v_prev_arr, prev_map, k_cur_arr, v_cur_arr, cur_map, n_cur, bias, sinks,
         B, L, TQ, mask_first):
    T = B * L
    nblk = L // TQ
    kvw = SWA_N_KV * SWA_HEAD_DIM
    n_keys = bias.shape[2] // 2
    n_prev = n_keys - n_cur
    assert n_keys & (n_keys - 1) == 0
    pairs = SWA_REP // 2
    return pl.pallas_call(
        functools.partial(_swa_kernel, mask_first=mask_first),
        grid=(B, nblk),
        in_specs=[pl.BlockSpec((TQ, SWA_N_HEADS * SWA_HEAD_DIM), lambda b, i: (b * nblk + i, 0)),
                  pl.BlockSpec((n_prev, kvw), prev_map[0]),
                  pl.BlockSpec((n_prev, kvw), prev_map[1]),
                  pl.BlockSpec((n_cur, kvw), cur_map[0]),
                  pl.BlockSpec((n_cur, kvw), cur_map[1]),
                  pl.BlockSpec(bias.shape, lambda b, i: (0, 0, 0)),
                  pl.BlockSpec(memory_space=pltpu.SMEM)],
        out_specs=pl.BlockSpec((TQ, SWA_N_HEADS * SWA_HEAD_DIM), lambda b, i: (b * nblk + i, 0)),
        out_shape=jax.ShapeDtypeStruct((T, SWA_N_HEADS * SWA_HEAD_DIM), BF16),
        scratch_shapes=[pltpu.VMEM((pairs * TQ, 2 * n_keys), F32),
                        pltpu.VMEM((pairs * TQ, 2 * n_keys), BF16),
                        pltpu.VMEM((pairs * TQ, 2 * SWA_HEAD_DIM), F32)],
        compiler_params=_params(("parallel", "arbitrary")),
        name="swa_attention",
    )(q, k_prev_arr, v_prev_arr, k_cur_arr, v_cur_arr, bias, sinks.astype(F32))


def _mem_kernel(q_ref, mk_ref, mv_ref, o_ref):
    scale = MEM_HEAD_DIM ** -0.5
    ndims = (((1,), (1,)), ((), ()))
    for h in range(MEM_N_HEADS):
        hs = slice(h * MEM_HEAD_DIM, (h + 1) * MEM_HEAD_DIM)
        s = lax.dot_general(q_ref[:, hs], mk_ref[:, hs], ndims, preferred_element_type=F32) * scale
        m = jnp.max(s, axis=-1, keepdims=True)
        p = jnp.exp(s - m)
        den = jnp.sum(p, axis=-1, keepdims=True)
        o = jnp.dot(p.astype(BF16), mv_ref[:, hs], preferred_element_type=F32)
        o_ref[:, hs] = (o / den).astype(o_ref.dtype)


def _mem_attend(q, mk, mv, L, tm):
    T, W = q.shape
    M = mk.shape[0] // (T // L)
    return pl.pallas_call(
        _mem_kernel,
        grid=(T // tm,),
        in_specs=[pl.BlockSpec((tm, W), lambda i: (i, 0)),
                  pl.BlockSpec((M, W), lambda i: ((i * tm) // L, 0)),
                  pl.BlockSpec((M, W), lambda i: ((i * tm) // L, 0))],
        out_specs=pl.BlockSpec((tm, W), lambda i: (i, 0)),
        out_shape=jax.ShapeDtypeStruct((T, W), BF16),
        compiler_params=_params(("parallel",)),
        name="mem_attention",
    )(q, mk, mv)


def _merge_kernel(ys_ref, os_ref, om_ref, w1_ref, w2_ref, w3_ref, g0_ref, g1_ref, g2_ref, o_ref):
    a = jnp.dot(ys_ref[...], w1_ref[...], preferred_element_type=F32)
    b = jnp.dot(os_ref[...], w2_ref[...], preferred_element_type=F32)
    c = jnp.dot(om_ref[...], w3_ref[...], preferred_element_type=F32)
    o = (g0_ref[...].astype(F32) * a + g1_ref[...].astype(F32) * b + g2_ref[...].astype(F32) * c)
    o_ref[...] = o.astype(o_ref.dtype)


def _merge(y_ssd, o_s, o_m, w1, w2, w3, gates, tm, tn):
    T = y_ssd.shape[0]
    D = w1.shape[1]
    nj = D // tn
    row = lambda i, j: (i, 0)
    colw = lambda i, j: (0, j)
    return pl.pallas_call(
        _merge_kernel,
        grid=(T // tm, nj),
        in_specs=[pl.BlockSpec((tm, y_ssd.shape[1]), row),
                  pl.BlockSpec((tm, o_s.shape[1]), row),
                  pl.BlockSpec((tm, o_m.shape[1]), row),
                  pl.BlockSpec((w1.shape[0], tn), colw),
                  pl.BlockSpec((w2.shape[0], tn), colw),
                  pl.BlockSpec((w3.shape[0], tn), colw),
                  pl.BlockSpec((tm, tn), lambda i, j: (i, j)),
                  pl.BlockSpec((tm, tn), lambda i, j: (i, j + nj)),
                  pl.BlockSpec((tm, tn), lambda i, j: (i, j + 2 * nj))],
        out_specs=pl.BlockSpec((tm, tn), lambda i, j: (i, j)),
        out_shape=jax.ShapeDtypeStruct((T, D), BF16),
        compiler_params=_params(("parallel", "arbitrary")),
        name="gated_merge",
    )(y_ssd, o_s, o_m, w1, w2, w3, gates, gates, gates)


def _norm_route_kernel(xa_ref, xb_ref, g_ref, whi_ref, wlo_ref, h_ref, r_ref, *, n_a):
    x = jnp.where(pl.program_id(0) < n_a, xa_ref[...], xb_ref[...])
    ms = jnp.mean(x * x, axis=-1, keepdims=True)
    h = x * lax.rsqrt(ms + EPS) * g_ref[...]
    for s in range(h_ref.shape[1]):
        h_ref[:, s, :] = h[:, s * LANES:(s + 1) * LANES]
    hb = h.astype(BF16)
    lo = (h - hb.astype(F32)).astype(BF16)
    lg = (jnp.dot(hb, whi_ref[...], preferred_element_type=F32)
          + jnp.dot(lo, whi_ref[...], preferred_element_type=F32)
          + jnp.dot(hb, wlo_ref[...], preferred_element_type=F32))
    lane = lax.broadcasted_iota(jnp.int32, lg.shape, 1)
    lane_f = lane.astype(F32)
    far = float(LANES)
    gl = jnp.where((lane >= N_EXPERTS) & (lane < N_EXPERTS + N_EXPERT_GROUPS), lg, NEG_INF)
    gmax = jnp.max(gl, axis=-1, keepdims=True)
    gidx = jnp.min(jnp.where(gl == gmax, lane_f - N_EXPERTS, far), axis=-1, keepdims=True)
    gw = 1.0 / jnp.sum(jnp.exp(gl - gmax), axis=-1, keepdims=True)
    lo_e = gidx * EXPERTS_PER_GROUP
    el = jnp.where((lane_f >= lo_e) & (lane_f < lo_e + EXPERTS_PER_GROUP), lg, NEG_INF)
    v1 = jnp.max(el, axis=-1, keepdims=True)
    i1 = jnp.min(jnp.where(el == v1, lane_f, far), axis=-1, keepdims=True)
    el2 = jnp.where(lane_f == i1, NEG_INF, el)
    v2 = jnp.max(el2, axis=-1, keepdims=True)
    i2 = jnp.min(jnp.where(el2 == v2, lane_f, far), axis=-1, keepdims=True)
    e = jnp.exp(v2 - v1)
    w1 = gw / (1.0 + e)
    w2 = gw * e / (1.0 + e)
    r_ref[...] = jnp.where(lane == 0, w1, jnp.where(lane == 1, w2,
                           jnp.where(lane == 2, i1, jnp.where(lane == 3, i2, 0.0))))


def _norm_route(xa, xb, g, w_hi, w_lo, tm):
    Ta, D = xa.shape
    Tb = xb.shape[0]
    T = Ta + Tb
    n_a, n_b = Ta // tm, Tb // tm
    assert n_a * tm == Ta and n_b * tm == Tb
    return pl.pallas_call(
        functools.partial(_norm_route_kernel, n_a=n_a),
        grid=(n_a + n_b,),
        in_specs=[pl.BlockSpec((tm, D), lambda i: (jnp.minimum(i, n_a - 1), 0)),
                  pl.BlockSpec((tm, D), lambda i: (jnp.maximum(i - n_a, 0), 0)),
                  pl.BlockSpec((1, D), lambda i: (0, 0)),
                  pl.BlockSpec((D, LANES), lambda i: (0, 0)),
                  pl.BlockSpec((D, LANES), lambda i: (0, 0))],
        out_specs=[pl.BlockSpec((tm, D // LANES, LANES), lambda i: (i, 0, 0)),
                   pl.BlockSpec((tm, LANES), lambda i: (i, 0))],
        out_shape=[jax.ShapeDtypeStruct((T, D // LANES, LANES), F32),
                   jax.ShapeDtypeStruct((T, LANES), F32)],
        compiler_params=_params(("parallel",)),
        name="ffn_norm_router",
    )(xa, xb, g.reshape(1, D).astype(F32), w_hi, w_lo)


def _moe_kernel(be_ref, nv_ref, s0_ref, ord_ref, h_hbm, w1_ref, w3_ref, w2_ref, yu_hbm,
                xbuf, ybuf, w1b, w3b, w2b, sem_in, sem_out, *, n_tok):
    i = pl.program_id(0)
    nb = pl.num_programs(0)
    rows = xbuf.shape[1]
    nslab = xbuf.shape[2]
    n_assign = TOP_K * n_tok
    nv = nv_ref[i]
    slot = i % 2
    nxt = jnp.minimum(i + 1, nb - 1)
    next_valid = (i + 1 < nb) & (nv_ref[nxt] > 0)

    def assignment(blk, r):
        return ord_ref[jnp.minimum(s0_ref[blk] + r, n_assign - 1)]

    def gather_start(blk, s):
        for r in range(rows):
            tok = lax.shift_right_logical(assignment(blk, r), 1)
            pltpu.make_async_copy(h_hbm.at[pl.ds(tok, 1)], xbuf.at[s, pl.ds(r, 1)], sem_in.at[s]).start()

    def gather_wait(s):
        for r in range(rows):
            pltpu.make_async_copy(h_hbm.at[pl.ds(0, 1)], xbuf.at[s, pl.ds(r, 1)], sem_in.at[s]).wait()

    def scatter_start(blk, s):
        n_valid = nv_ref[blk]
        for r in range(rows):
            m = assignment(blk, r)
            dst = jnp.where(r < n_valid, (m & 1) * n_tok + lax.shift_right_logical(m, 1),
                            n_assign + s * rows + r)
            pltpu.make_async_copy(ybuf.at[s, pl.ds(r, 1)], yu_hbm.at[pl.ds(dst, 1)], sem_out.at[s]).start()

    def scatter_wait(s):
        for r in range(rows):
            pltpu.make_async_copy(ybuf.at[s, pl.ds(r, 1)], yu_hbm.at[pl.ds(0, 1)], sem_out.at[s]).wait()

    @pl.when(i == 0)
    def _clear_spare_rows():
        ybuf[...] = jnp.zeros(ybuf.shape, ybuf.dtype)
        for s in range(2):
            cp = pltpu.make_async_copy(ybuf.at[s], yu_hbm.at[pl.ds(n_assign + s * rows, rows)], sem_out.at[s])
            cp.start()
            cp.wait()

    @pl.when((i == 0) & (nv > 0))
    def _prologue():
        gather_start(i, 0)

    @pl.when(next_valid)
    def _prefetch():
        gather_start(nxt, 1 - slot)

    prev = be_ref[jnp.maximum(i - 1, 0)]

    @pl.when((i == 0) | (be_ref[i] != prev))
    def _load_expert():
        w1b[...] = w1_ref[0].astype(BF16)
        w3b[...] = w3_ref[0].astype(BF16)
        w2b[...] = w2_ref[0].astype(BF16)

    @pl.when(nv > 0)
    def _compute():
        gather_wait(slot)

        @pl.when(i >= 2)
        def _free_ybuf():
            scatter_wait(slot)

        x = jnp.concatenate([xbuf[slot, :, s, :] for s in range(nslab)], axis=1).astype(BF16)
        a = jnp.dot(x, w1b[...], preferred_element_type=F32)
        b = jnp.dot(x, w3b[...], preferred_element_type=F32)
        mid = ((a * _sigmoid(a)) * b).astype(BF16)
        y = jnp.dot(mid, w2b[...], preferred_element_type=F32)
        for s in range(nslab):
            ybuf[slot, :, s, :] = y[:, s * LANES:(s + 1) * LANES]
        scatter_start(i, slot)

        @pl.when(jnp.logical_not(next_valid))
        def _drain():
            scatter_wait(slot)

            @pl.when(i >= 1)
            def _drain_prev():
                scatter_wait(1 - slot)


def _moe(h2, route, w_gate, w_up, w_down):
    T, nslab, _ = h2.shape
    D = nslab * LANES
    F = w_gate.shape[2]
    assert TOP_K == 2
    M = T * TOP_K
    BM = MOE_ROWS
    nb = (M + N_EXPERTS * (BM - 1) + BM - 1) // BM
    e_flat = route[:, TOP_K:2 * TOP_K].astype(jnp.int32).reshape(M)
    _, order = lax.sort((e_flat, jnp.arange(M, dtype=jnp.int32)), num_keys=1, is_stable=True)
    experts = jnp.arange(N_EXPERTS, dtype=jnp.int32)
    counts = jnp.sum((experts[:, None] == e_flat[None, :]).astype(jnp.int32), axis=1)
    padded = (counts + BM - 1) // BM * BM
    pad_end = jnp.cumsum(padded)
    pad_start = pad_end - padded
    start = jnp.cumsum(counts) - counts
    blk0 = jnp.arange(nb, dtype=jnp.int32) * BM
    blk_exp = jnp.minimum(jnp.sum((pad_end[None, :] <= blk0[:, None]).astype(jnp.int32), axis=1), N_EXPERTS - 1)
    pick = (blk_exp[:, None] == experts[None, :]).astype(jnp.int32)
    off0 = blk0 - jnp.sum(pick * pad_start[None, :], axis=1)
    blk_nv = jnp.clip(jnp.sum(pick * counts[None, :], axis=1) - off0, 0, BM).astype(jnp.int32)
    blk_s0 = jnp.clip(jnp.sum(pick * start[None, :], axis=1) + off0, 0, M - 1).astype(jnp.int32)

    grid_spec = pltpu.PrefetchScalarGridSpec(
        num_scalar_prefetch=4,
        grid=(nb,),
        in_specs=[pl.BlockSpec(memory_space=pl.ANY),
                  pl.BlockSpec((1, D, F), lambda i, be, nv, s0, od: (be[i], 0, 0)),
                  pl.BlockSpec((1, D, F), lambda i, be, nv, s0, od: (be[i], 0, 0)),
                  pl.BlockSpec((1, F, D), lambda i, be, nv, s0, od: (be[i], 0, 0))],
        out_specs=pl.BlockSpec(memory_space=pl.ANY),
        scratch_shapes=[pltpu.VMEM((2, BM, nslab, LANES), F32),
                        pltpu.VMEM((2, BM, nslab, LANES), F32),
                        pltpu.VMEM((D, F), BF16),
                        pltpu.VMEM((D, F), BF16),
                        pltpu.VMEM((F, D), BF16),
                        pltpu.SemaphoreType.DMA((2,)),
                        pltpu.SemaphoreType.DMA((2,))],
    )
    return pl.pallas_call(
        functools.partial(_moe_kernel, n_tok=T),
        grid_spec=grid_spec,
        out_shape=jax.ShapeDtypeStruct((M + 2 * BM, nslab, LANES), F32),
        compiler_params=_params(("arbitrary",)),
        name="moe_experts",
    )(blk_exp.astype(jnp.int32), blk_nv, blk_s0, order, h2, w_gate, w_up, w_down)


def _combine_kernel(x_ref, r_ref, y0_ref, y1_ref, o_ref):
    w0 = r_ref[:, 0:1]
    w1 = r_ref[:, 1:2]
    for s in range(y0_ref.shape[1]):
        sl = slice(s * LANES, (s + 1) * LANES)
        o_ref[:, sl] = x_ref[:, sl] + (w0 * y0_ref[:, s, :] + w1 * y1_ref[:, s, :])


def _combine(x, route, yu, blk0, tm):
    T, D = x.shape
    n_all = route.shape[0] // tm
    slab = (tm, yu.shape[1], LANES)
    return pl.pallas_call(
        _combine_kernel,
        grid=(T // tm,),
        in_specs=[pl.BlockSpec((tm, D), lambda i: (i, 0)),
                  pl.BlockSpec((tm, LANES), lambda i: (blk0 + i, 0)),
                  pl.BlockSpec(slab, lambda i: (blk0 + i, 0, 0)),
                  pl.BlockSpec(slab, lambda i: (n_all + blk0 + i, 0, 0))],
        out_specs=pl.BlockSpec((tm, D), lambda i: (i, 0)),
        out_shape=jax.ShapeDtypeStruct((T, D), F32),
        compiler_params=_params(("parallel",)),
        name="moe_combine",
    )(x, route, yu, yu)


def _layer(x, B, L, p, conv_prev8, h0_t, mk, mv, swa_prev, swa_bias, Lc, TQ):
    T, D = x.shape
    tm = min(512, T)
    tmm = min(MM_ROWS, T)
    h = _rmsnorm(x, p["g_mix"], tm)
    z = _matmul(h, p["wz"], BF16, tmm, MM_COLS, name="proj_z")
    xbc = _matmul(h, p["wxbc"], BF16, tmm, MM_COLS, name="proj_xbc")
    dt = _matmul(h, p["wdt"], F32, tmm, PAD_HEADS, name="proj_dt")
    q_s = _matmul(h, p["wqs"], BF16, tmm, 512, epi=_epi_group_norm, col_extras=(p["g_q_swa"],),
                  const_extras=(p["mavg"],), name="proj_q_swa")
    kv = _matmul(h, p["wkv"], F32, tmm, 256, epi=_epi_kv, col_extras=(p["g_kv"],),
                 const_extras=(p["mavg"],), name="proj_kv_swa")
    q_m = _matmul(h, p["wqm"], BF16, tmm, MEM_HEAD_DIM, epi=_epi_row_norm, col_extras=(p["g_q_mem"],),
                  name="proj_q_mem")
    gates = _matmul(h, p["wg"], BF16, tmm, MM_COLS, epi=_epi_sigmoid, name="proj_gates")

    y_ssd, h_t, conv8 = _ssd(z, xbc, dt, conv_prev8, h0_t, p, B, L, Lc)

    nblk = L // TQ
    kvw = SWA_N_KV * SWA_HEAD_DIM
    if swa_prev is None:
        prev_map = (lambda b, i: (b * nblk + jnp.maximum(i - 1, 0), 0),
                    lambda b, i: (b * nblk + jnp.maximum(i - 1, 0), 1))
        cur_map = (lambda b, i: (b * nblk + i, 0), lambda b, i: (b * nblk + i, 1))
        o_s = _swa(q_s, kv, kv, prev_map, kv, kv, cur_map, TQ, swa_bias, p["sinks"], B, L, TQ, True)
    else:
        n_cur = swa_bias.shape[2] // 2 - swa_prev[0].shape[0] // B
        kv3 = jnp.pad(kv.reshape(B, L, 2 * kvw), ((0, 0), (0, n_cur - L), (0, 0)))
        k_new = kv3[:, :, :kvw].reshape(B * n_cur, kvw)
        v_new = kv3[:, :, kvw:].reshape(B * n_cur, kvw)
        per_b = (lambda b, i: (b, 0), lambda b, i: (b, 0))
        o_s = _swa(q_s, swa_prev[0], swa_prev[1], per_b, k_new, v_new, per_b, n_cur, swa_bias, p["sinks"],
                   B, L, TQ, False)

    o_m = _mem_attend(q_m, mk, mv, L, min(256, L))

    merged = _merge(y_ssd, o_s, o_m, p["w_o_ssd"], p["w_o_swa"], p["w_o_mem"], gates, tm, 512)
    x1 = _matmul(merged, p["w_out"], F32, tmm, MM_COLS, epi=_epi_residual, tile_extras=(x,), name="proj_out")

    return x1, conv8, h_t, kv


def _state_to_heads(h_t, B):
    return jnp.transpose(h_t.reshape(B, SSD_D_STATE, SSD_N_HEADS, SSD_HEAD_DIM), (0, 2, 3, 1))


def kernel(x_prompt, x_sample, cache_conv, state_ssd, cache_swa_k, cache_swa_v, cache_mem_k, cache_mem_v, mem_prompt, rel_bias_table, g_mix, w_in, conv_w, conv_b, dt_bias, a_log, d_skip, g_ssd, w_o_ssd, g_q_swa, g_k_swa, sinks, w_o_swa, g_mem, w_mem_k, w_mem_v, g_q_mem, g_k_mem, w_o_mem, w_out, g_ffn, w_router_grp, w_router_exp, w_exp_gate, w_exp_up, w_exp_down):
    B, S, D = x_prompt.shape
    Bd, Sd, _ = x_sample.shape
    depth = w_in.shape[0]
    assert depth == 1
    l = 0
    kvw = SWA_N_KV * SWA_HEAD_DIM
    qw = SWA_N_HEADS * SWA_HEAD_DIM
    mw = MEM_N_HEADS * MEM_HEAD_DIM

    sizes = (SSD_D_INNER, SSD_CONV_DIM, SSD_N_HEADS, qw, kvw, kvw, mw, 3 * D)
    offs = np.concatenate([[0], np.cumsum(sizes)])
    w = w_in[l]
    cols = [w[:, int(offs[k]):int(offs[k + 1])] for k in range(len(sizes))]
    pad_h = PAD_HEADS - SSD_N_HEADS
    w_r = jnp.pad(jnp.concatenate([w_router_exp[l], w_router_grp[l]], axis=1),
                  ((0, 0), (0, LANES - N_EXPERTS - N_EXPERT_GROUPS)))
    w_r_hi = w_r.astype(BF16)
    p = {
        "g_mix": g_mix[l],
        "wz": cols[0].astype(BF16),
        "wxbc": cols[1].astype(BF16),
        "wdt": jnp.pad(cols[2], ((0, 0), (0, pad_h))).astype(BF16),
        "wqs": cols[3].astype(BF16),
        "wkv": jnp.concatenate([cols[4], cols[5]], axis=1).astype(BF16),
        "wqm": cols[6].astype(BF16),
        "wg": cols[7].astype(BF16),
        "conv_w": conv_w[l].astype(F32),
        "conv_b": conv_b[l].reshape(1, SSD_CONV_DIM).astype(F32),
        "dt_bias": jnp.pad(dt_bias[l], (0, pad_h)).reshape(1, PAD_HEADS).astype(F32),
        "a_log": jnp.pad(a_log[l], (0, pad_h)).reshape(1, PAD_HEADS).astype(F32),
        "d_skip": jnp.repeat(d_skip[l], SSD_HEAD_DIM).reshape(1, SSD_D_INNER).astype(F32),
        "g_ssd": g_ssd[l].reshape(1, SSD_D_INNER).astype(F32),
        "g_q_swa": (jnp.tile(g_q_swa[l], SWA_N_HEADS) * SWA_HEAD_DIM ** -0.5).reshape(1, qw).astype(F32),
        "g_kv": jnp.concatenate([jnp.tile(g_k_swa[l], SWA_N_KV), jnp.ones((kvw,), F32)]).reshape(1, 2 * kvw),
        "g_q_mem": jnp.tile(g_q_mem[l], MEM_N_HEADS).reshape(1, mw).astype(F32),
        "mavg": _group_mean_matrix(256, SWA_HEAD_DIM),
        "sinks": sinks[l],
        "w_o_ssd": w_o_ssd[l].astype(BF16),
        "w_o_swa": w_o_swa[l].astype(BF16),
        "w_o_mem": w_o_mem[l].astype(BF16),
        "w_out": w_out[l].astype(BF16),
        "g_ffn": g_ffn[l],
        "w_r_hi": w_r_hi,
        "w_r_lo": (w_r - w_r_hi.astype(F32)).astype(BF16),
        "w_exp_gate": w_exp_gate[l],
        "w_exp_up": w_exp_up[l],
        "w_exp_down": w_exp_down[l],
    }

    M = mem_prompt.shape[1]
    mn = _rmsnorm(mem_prompt.reshape(B * M, D), g_mem[l], min(256, B * M))
    mk_p = _matmul(mn, w_mem_k[l].astype(BF16), F32, min(256, B * M), MEM_HEAD_DIM, epi=_epi_row_norm,
                   col_extras=(jnp.tile(g_k_mem[l], MEM_N_HEADS).reshape(1, mw).astype(F32),), name="mem_k")
    mv_p = _matmul(mn, w_mem_v[l].astype(BF16), F32, min(256, B * M), MEM_HEAD_DIM, name="mem_v")

    TQ = 2 * CHUNK
    qpos = np.arange(TQ)
    bias_p = _pair_bias(_bias_from_table(_bucket_map(qpos, np.arange(TQ) - TQ), rel_bias_table),
                        _bias_from_table(_bucket_map(qpos, np.arange(TQ)), rel_bias_table))
    C = cache_swa_k.shape[2]
    qpos_s = PAST_LEN + np.arange(Sd)
    cur_map_s = _bucket_map(qpos_s, PAST_LEN + np.arange(C))
    cur_map_s[:, Sd:] = -1
    bias_s = _pair_bias(_bias_from_table(_bucket_map(qpos_s, PAST_LEN - C + np.arange(C)), rel_bias_table),
                        _bias_from_table(cur_map_s, rel_bias_table))

    conv0 = jnp.zeros((B, SUBLANES, SSD_CONV_DIM), F32)
    h0 = jnp.zeros((B, SSD_D_STATE, SSD_D_INNER), F32)
    x1_p, conv8_p, ht_p, kv_p = _layer(x_prompt.reshape(B * S, D), B, S, p, conv0, h0,
                                     mk_p.astype(BF16), mv_p.astype(BF16), None,
                                     bias_p, CHUNK, TQ)
    conv_prev = jnp.pad(cache_conv[l], ((0, 0), (SUBLANES - (SSD_CONV - 1), 0), (0, 0)))
    h0_s = jnp.transpose(state_ssd[l], (0, 3, 1, 2)).reshape(Bd, SSD_D_STATE, SSD_D_INNER)
    x1_s, conv8_s, ht_s, kv_s = _layer(x_sample.reshape(Bd * Sd, D), Bd, Sd, p, conv_prev, h0_s,
                                     cache_mem_k[l].reshape(Bd * M, mw).astype(BF16),
                                     cache_mem_v[l].reshape(Bd * M, mw).astype(BF16),
                                     (cache_swa_k[l].reshape(Bd * C, kvw), cache_swa_v[l].reshape(Bd * C, kvw)),
                                     bias_s, Sd, Sd)

    tr = math.gcd(B * S, Bd * Sd, MOE_ROWS)
    h2, route = _norm_route(x1_p, x1_s, p["g_ffn"], p["w_r_hi"], p["w_r_lo"], tr)
    yu = _moe(h2, route, p["w_exp_gate"], p["w_exp_up"], p["w_exp_down"])
    yp = _combine(x1_p, route, yu, 0, tr)
    ys = _combine(x1_s, route, yu, (B * S) // tr, tr)

    keep = min(WINDOW, S)
    kv_p = kv_p.reshape(B, S, 2, SWA_N_KV, SWA_HEAD_DIM)[:, S - keep:]
    kv_s = kv_s.reshape(Bd, Sd, 2, SWA_N_KV, SWA_HEAD_DIM)
    tail = SUBLANES - (SSD_CONV - 1)
    return (yp.reshape(B, S, D), ys.reshape(Bd, Sd, D),
            conv8_p[None, :, tail:], _state_to_heads(ht_p, B)[None],
            kv_p[None, :, :, 0], kv_p[None, :, :, 1],
            mk_p.reshape(1, B, M, MEM_N_HEADS, MEM_HEAD_DIM), mv_p.reshape(1, B, M, MEM_N_HEADS, MEM_HEAD_DIM),
            conv8_s[None, :, tail:], _state_to_heads(ht_s, Bd)[None],
            kv_s[None, :, :, 0], kv_s[None, :, :, 1])
```

```python
import functools
import math

import numpy as np
import jax
import jax.numpy as jnp
from jax import lax
from jax.experimental import pallas as pl
from jax.experimental.pallas import tpu as pltpu

F32 = jnp.float32
BF16 = jnp.bfloat16
EPS = 1e-6
NEG_INF = float("-inf")

CHUNK = 64
SSD_HEAD_DIM = 64
SSD_N_HEADS = 64
SSD_N_GROUPS = 8
SSD_D_STATE = 128
SSD_D_INNER = SSD_N_HEADS * SSD_HEAD_DIM
SSD_GN = SSD_N_GROUPS * SSD_D_STATE
SSD_CONV_DIM = SSD_D_INNER + 2 * SSD_GN
SSD_CONV = 4
SWA_N_HEADS = 32
SWA_N_KV = 4
SWA_HEAD_DIM = 64
SWA_REP = SWA_N_HEADS // SWA_N_KV
WINDOW = 128
WINDOW_CHUNKS = WINDOW // CHUNK
MEM_N_HEADS = 4
MEM_HEAD_DIM = 512
N_BUCKETS = 32
MAX_DISTANCE = 128
N_EXPERT_GROUPS = 4
EXPERTS_PER_GROUP = 16
N_EXPERTS = N_EXPERT_GROUPS * EXPERTS_PER_GROUP
TOP_K = 2
PAST_LEN = 4096

LANES = 128
SUBLANES = 8
VMEM_LIMIT = 56 * 1024 * 1024
MOE_ROWS = 256
MM_ROWS = 1024
MM_COLS = 1024
PAD_HEADS = LANES


def _params(sem, vmem=VMEM_LIMIT):
    return pltpu.CompilerParams(dimension_semantics=sem, vmem_limit_bytes=vmem)


def _sigmoid(x):
    return 0.5 * (jnp.tanh(0.5 * x) + 1.0)


def _split3(x):
    x1 = x.astype(BF16)
    r1 = x - x1.astype(F32)
    x2 = r1.astype(BF16)
    x3 = (r1 - x2.astype(F32)).astype(BF16)
    return x1, x2, x3


def _rmsnorm_kernel(x_ref, g_ref, o_ref):
    x = x_ref[...]
    ms = jnp.mean(x * x, axis=-1, keepdims=True)
    o_ref[...] = (x * lax.rsqrt(ms + EPS) * g_ref[...]).astype(o_ref.dtype)


def _rmsnorm(x, g, tm):
    T, D = x.shape
    return pl.pallas_call(
        _rmsnorm_kernel,
        grid=(T // tm,),
        in_specs=[pl.BlockSpec((tm, D), lambda i: (i, 0)),
                  pl.BlockSpec((1, D), lambda i: (0, 0))],
        out_specs=pl.BlockSpec((tm, D), lambda i: (i, 0)),
        out_shape=jax.ShapeDtypeStruct((T, D), BF16),
        compiler_params=_params(("parallel",)),
        name="rmsnorm",
    )(x, g.reshape(1, D).astype(F32))


def _mm_kernel(a_ref, b_ref, *refs, epi):
    o_ref = refs[-1]
    acc = jnp.dot(a_ref[...], b_ref[...], preferred_element_type=F32)
    if epi is not None:
        acc = epi(acc, *[r[...] for r in refs[:-1]])
    o_ref[...] = acc.astype(o_ref.dtype)


def _matmul(a, b, out_dtype, tm, tn, epi=None, col_extras=(), tile_extras=(), const_extras=(), name="matmul"):
    M, K = a.shape
    N = b.shape[1]
    assert M % tm == 0 and N % tn == 0, (M, N, tm, tn)
    in_specs = [pl.BlockSpec((tm, K), lambda i, j: (i, 0)),
                pl.BlockSpec((K, tn), lambda i, j: (0, j))]
    for _ in col_extras:
        in_specs.append(pl.BlockSpec((1, tn), lambda i, j: (0, j)))
    for _ in tile_extras:
        in_specs.append(pl.BlockSpec((tm, tn), lambda i, j: (i, j)))
    for c in const_extras:
        in_specs.append(pl.BlockSpec(c.shape, lambda i, j: (0, 0)))
    return pl.pallas_call(
        functools.partial(_mm_kernel, epi=epi),
        grid=(M // tm, N // tn),
        in_specs=in_specs,
        out_specs=pl.BlockSpec((tm, tn), lambda i, j: (i, j)),
        out_shape=jax.ShapeDtypeStruct((M, N), out_dtype),
        compiler_params=_params(("parallel", "arbitrary")),
        name=name,
    )(a, b, *col_extras, *tile_extras, *const_extras)


def _group_mean_matrix(width, group):
    idx = np.arange(width) // group
    return jnp.asarray((idx[:, None] == idx[None, :]).astype(np.float32) / group, dtype=BF16)


def _epi_group_norm(acc, gain, mavg):
    w = mavg.shape[0]
    outs = []
    for c in range(acc.shape[1] // w):
        a = acc[:, c * w:(c + 1) * w]
        s = a * a
        hi = s.astype(BF16)
        lo = (s - hi.astype(F32)).astype(BF16)
        ms = (jnp.dot(hi, mavg, preferred_element_type=F32)
              + jnp.dot(lo, mavg, preferred_element_type=F32))
        outs.append(a * lax.rsqrt(ms + EPS))
    normed = outs[0] if len(outs) == 1 else jnp.concatenate(outs, axis=1)
    return normed * gain


def _epi_kv(acc, gain, mavg):
    normed = _epi_group_norm(acc, gain, mavg)
    return jnp.where(pl.program_id(1) == 0, normed, acc)


def _epi_row_norm(acc, gain):
    ms = jnp.mean(acc * acc, axis=-1, keepdims=True)
    return acc * lax.rsqrt(ms + EPS) * gain


def _epi_sigmoid(acc):
    return _sigmoid(acc)


def _epi_residual(acc, res):
    return acc + res


def _ssd_kernel(z_ref, xbc_ref, dt_ref, cprev_ref, h0_ref, cw_ref, cb_ref, dtb_ref, alog_ref,
                dskip_ref, gn_ref, y_ref, hout_ref, cout_ref, xp_s, h_s, conv_s, y_s, *, Lc):
    c = pl.program_id(1)
    n_chunks = pl.num_programs(1)
    P2 = 2 * SSD_HEAD_DIM
    L2 = 2 * Lc

    @pl.when(c == 0)
    def _init():
        xp_s[0:SUBLANES, :] = cprev_ref[0]
        h_s[...] = h0_ref[0]

    xp_s[SUBLANES:SUBLANES + Lc, :] = xbc_ref[...].astype(F32)
    cblk = 512
    row8 = lax.broadcasted_iota(jnp.int32, (SUBLANES, cblk), 0)
    for j in range(SSD_CONV_DIM // cblk):
        sl = slice(j * cblk, (j + 1) * cblk)
        cur = xp_s[SUBLANES:SUBLANES + Lc, sl]
        tail = xp_s[0:SUBLANES, sl]
        acc = cb_ref[:, sl] + cw_ref[SSD_CONV - 1:SSD_CONV, sl] * cur
        for k in range(SSD_CONV - 1):
            sh = SSD_CONV - 1 - k
            down = pltpu.roll(cur, sh, axis=0)
            top = jnp.where(row8 >= sh, down[0:SUBLANES], pltpu.roll(tail, sh, axis=0))
            acc = acc + cw_ref[k:k + 1, sl] * jnp.concatenate([top, down[SUBLANES:]], axis=0)
        conv_s[:, sl] = acc * _sigmoid(acc)
    xp_s[0:SUBLANES, :] = xp_s[Lc:Lc + SUBLANES, :]

    dtv = dt_ref[...] + dtb_ref[...]
    dt = jnp.maximum(dtv, 0.0) + jnp.log1p(jnp.exp(-jnp.abs(dtv)))
    adt = dt * (-jnp.exp(alog_ref[...]))
    row = lax.broadcasted_iota(jnp.int32, (Lc, Lc), 0)
    col = lax.broadcasted_iota(jnp.int32, (Lc, Lc), 1)
    tri = (col <= row).astype(BF16)
    row2 = lax.broadcasted_iota(jnp.int32, (Lc, L2), 0)
    col2 = lax.broadcasted_iota(jnp.int32, (Lc, L2), 1)
    col2m = jnp.where(col2 >= Lc, col2 - Lc, col2)
    tri_t2 = (row2 <= col2m).astype(BF16)
    causal2 = col2m <= row2
    a1, a2, a3 = _split3(adt)
    acs = (jnp.dot(tri, a1, preferred_element_type=F32)
           + jnp.dot(tri, a2, preferred_element_type=F32)
           + jnp.dot(tri, a3, preferred_element_type=F32))
    tdims = (((0,), (0,)), ((), ()))
    acs_t2 = (lax.dot_general(a1, tri_t2, tdims, preferred_element_type=F32)
              + lax.dot_general(a2, tri_t2, tdims, preferred_element_type=F32)
              + lax.dot_general(a3, tri_t2, tdims, preferred_element_type=F32))

    lane_p = lax.broadcasted_iota(jnp.int32, (Lc, P2), 1)
    first_p = lane_p < SSD_HEAD_DIM
    first_l = col2 < Lc
    first_l1 = first_l[0:1, :]
    ndims = (((1,), (1,)), ((), ()))

    for g in range(SSD_N_GROUPS):
        b_g = conv_s[:, SSD_D_INNER + g * SSD_D_STATE:SSD_D_INNER + (g + 1) * SSD_D_STATE].astype(BF16)
        c_g = conv_s[:, SSD_D_INNER + SSD_GN + g * SSD_D_STATE:
                     SSD_D_INNER + SSD_GN + (g + 1) * SSD_D_STATE].astype(BF16)
        b2 = jnp.concatenate([b_g, b_g], axis=0)
        cb2 = lax.dot_general(c_g, b2, ndims, preferred_element_type=F32)
        gw = SSD_HEAD_DIM * (SSD_N_HEADS // SSD_N_GROUPS)
        inter = jnp.dot(c_g, h_s[:, g * gw:(g + 1) * gw].astype(BF16), preferred_element_type=F32)
        for jj in range(gw // P2):
            j = g * (gw // P2) + jj
            sl = slice(j * P2, (j + 1) * P2)
            acs_a = acs[:, 2 * j:2 * j + 1]
            acs_b = acs[:, 2 * j + 1:2 * j + 2]
            col_l = jnp.where(first_l, acs_a, acs_b)
            row_l = jnp.where(first_l1, acs_t2[2 * j:2 * j + 1, :], acs_t2[2 * j + 1:2 * j + 2, :])
            dec = jnp.exp(jnp.where(causal2, col_l - row_l, NEG_INF))
            m_pair = (cb2 * dec).astype(BF16)
            col_p = col_l if L2 == P2 else jnp.where(first_p, acs_a, acs_b)
            dt_p = jnp.where(first_p, dt[:, 2 * j:2 * j + 1], dt[:, 2 * j + 1:2 * j + 2])
            xs_p = conv_s[:, sl]
            xdt = xs_p * dt_p
            rhs = jnp.concatenate([jnp.where(first_p, xdt, 0.0), jnp.where(first_p, 0.0, xdt)],
                                  axis=0).astype(BF16)
            y = jnp.dot(m_pair, rhs, preferred_element_type=F32)
            y = y + inter[:, jj * P2:(jj + 1) * P2] * jnp.exp(col_p) + dskip_ref[:, sl] * xs_p
            y_s[:, sl] = y
            a_end = col_p[Lc - 1:Lc, :]
            xw = (xdt * jnp.exp(a_end - col_p)).astype(BF16)
            h_s[:, sl] = (h_s[:, sl] * jnp.exp(a_end)
                          + lax.dot_general(b_g, xw, tdims, preferred_element_type=F32))

    gdim = SSD_D_INNER // SSD_N_GROUPS
    for g in range(SSD_N_GROUPS):
        sl = slice(g * gdim, (g + 1) * gdim)
        zz = z_ref[:, sl].astype(F32)
        yy = y_s[:, sl] * (zz * _sigmoid(zz))
        ms = jnp.mean(yy * yy, axis=-1, keepdims=True)
        y_ref[:, sl] = (yy * lax.rsqrt(ms + EPS) * gn_ref[:, sl]).astype(y_ref.dtype)

    @pl.when(c == n_chunks - 1)
    def _fin():
        hout_ref[0] = h_s[...]
        cout_ref[0] = xp_s[0:SUBLANES, :]


def _ssd(z, xbc, dt, conv_prev8, h0_t, p, B, L, Lc):
    T = B * L
    nc = L // Lc
    tok = lambda b, c: (b * nc + c, 0)
    per_b = lambda b, c: (b, 0, 0)
    whole = lambda b, c: (0, 0)
    y, h_t, conv8 = pl.pallas_call(
        functools.partial(_ssd_kernel, Lc=Lc),
        grid=(B, nc),
        in_specs=[pl.BlockSpec((Lc, SSD_D_INNER), tok),
                  pl.BlockSpec((Lc, SSD_CONV_DIM), tok),
                  pl.BlockSpec((Lc, PAD_HEADS), tok),
                  pl.BlockSpec((1, SUBLANES, SSD_CONV_DIM), per_b),
                  pl.BlockSpec((1, SSD_D_STATE, SSD_D_INNER), per_b),
                  pl.BlockSpec((SSD_CONV, SSD_CONV_DIM), whole),
                  pl.BlockSpec((1, SSD_CONV_DIM), whole),
                  pl.BlockSpec((1, PAD_HEADS), whole),
                  pl.BlockSpec((1, PAD_HEADS), whole),
                  pl.BlockSpec((1, SSD_D_INNER), whole),
                  pl.BlockSpec((1, SSD_D_INNER), whole)],
        out_specs=[pl.BlockSpec((Lc, SSD_D_INNER), tok),
                   pl.BlockSpec((1, SSD_D_STATE, SSD_D_INNER), per_b),
                   pl.BlockSpec((1, SUBLANES, SSD_CONV_DIM), per_b)],
        out_shape=[jax.ShapeDtypeStruct((T, SSD_D_INNER), BF16),
                   jax.ShapeDtypeStruct((B, SSD_D_STATE, SSD_D_INNER), F32),
                   jax.ShapeDtypeStruct((B, SUBLANES, SSD_CONV_DIM), F32)],
        scratch_shapes=[pltpu.VMEM((SUBLANES + Lc, SSD_CONV_DIM), F32),
                        pltpu.VMEM((SSD_D_STATE, SSD_D_INNER), F32),
                        pltpu.VMEM((Lc, SSD_CONV_DIM), F32),
                        pltpu.VMEM((Lc, SSD_D_INNER), F32)],
        compiler_params=_params(("arbitrary", "arbitrary")),
        name="ssd_scan",
    )(z, xbc, dt, conv_prev8, h0_t, p["conv_w"], p["conv_b"], p["dt_bias"], p["a_log"],
      p["d_skip"], p["g_ssd"])
    return y, h_t, conv8


def _bias_kernel(idx_ref, tab_ref, o_ref):
    h = pl.program_id(0)
    idx = idx_ref[...]
    acc = jnp.full(idx.shape, NEG_INF, F32)
    for b in range(N_BUCKETS):
        acc = jnp.where(idx == b, tab_ref[b, h], acc)
    o_ref[0] = acc


def _bias_from_table(idx, table):
    Q, K = idx.shape
    return pl.pallas_call(
        _bias_kernel,
        grid=(SWA_N_HEADS,),
        in_specs=[pl.BlockSpec((Q, K), lambda h: (0, 0)),
                  pl.BlockSpec(memory_space=pltpu.SMEM)],
        out_specs=pl.BlockSpec((1, Q, K), lambda h: (h, 0, 0)),
        out_shape=jax.ShapeDtypeStruct((SWA_N_HEADS, Q, K), F32),
        compiler_params=_params(("arbitrary",)),
        name="rel_bias",
    )(jnp.asarray(idx, jnp.int32), table.astype(F32))


def _t5_bucket_np(rel):
    nb = N_BUCKETS // 2
    max_exact = nb // 2
    ret = np.where(rel > 0, nb, 0)
    n = np.abs(rel)
    nf = np.maximum(n, 1).astype(np.float32)
    large = max_exact + (np.log(nf / np.float32(max_exact)) / np.float32(math.log(MAX_DISTANCE / max_exact))
                         * np.float32(nb - max_exact)).astype(np.int32)
    large = np.minimum(large, nb - 1)
    return (ret + np.where(n < max_exact, n, large)).astype(np.int32)


def _bucket_map(q_pos, k_pos):
    qc, kc = q_pos // CHUNK, k_pos // CHUNK
    valid = (kc[None, :] >= qc[:, None] - WINDOW_CHUNKS) & (kc[None, :] <= qc[:, None])
    return np.where(valid, _t5_bucket_np(k_pos[None, :] - q_pos[:, None]), -1).astype(np.int32)


def _swa_kernel(q_ref, kp_ref, vp_ref, kc_ref, vc_ref, bias_ref, sink_ref, o_ref, s_scr, p_scr, t_scr,
                *, mask_first):
    TQ = q_ref.shape[0]
    NP = kp_ref.shape[0]
    NK = NP + kc_ref.shape[0]
    HD = SWA_HEAD_DIM
    PW = 2 * HD
    pairs = SWA_REP // 2
    ndims = (((1,), (1,)), ((), ()))
    first_o = lax.broadcasted_iota(jnp.int32, (TQ, PW), 1) < HD
    first_s = lax.broadcasted_iota(jnp.int32, (TQ, 2 * NK), 1) < NK
    zero = jnp.zeros((NK, HD), BF16)
    one = jnp.ones((NK, HD), BF16)
    for g in range(SWA_N_KV):
        ks = slice(g * HD, (g + 1) * HD)
        k_g = jnp.concatenate([kp_ref[:, ks], kc_ref[:, ks]], axis=0).astype(BF16)
        v_g = jnp.concatenate([vp_ref[:, ks], vc_ref[:, ks]], axis=0).astype(BF16)
        kk = jnp.concatenate([jnp.concatenate([k_g, zero], axis=1),
                              jnp.concatenate([zero, k_g], axis=1)], axis=0)
        vv = jnp.concatenate([jnp.concatenate([v_g, zero, one, zero], axis=1),
                              jnp.concatenate([zero, v_g, zero, one], axis=1)], axis=0)
        for pr in range(pairs):
            pidx = g * pairs + pr
            s_scr[pr * TQ:(pr + 1) * TQ, :] = (
                lax.dot_general(q_ref[:, pidx * PW:(pidx + 1) * PW], kk, ndims, preferred_element_type=F32)
                + bias_ref[pidx])
        if mask_first:
            @pl.when(pl.program_id(1) == 0)
            def _mask_prev():
                col = lax.broadcasted_iota(jnp.int32, s_scr.shape, 1)
                s_scr[...] = jnp.where((col & (NK - 1)) < NP, NEG_INF, s_scr[...])
        for pr in range(pairs):
            pidx = g * pairs + pr
            rows = slice(pr * TQ, (pr + 1) * TQ)
            s = s_scr[rows, :]
            sink_a = sink_ref[2 * pidx]
            sink_b = sink_ref[2 * pidx + 1]
            ma = jnp.maximum(jnp.max(s[:, :NK], axis=-1, keepdims=True), sink_a)
            mb = jnp.maximum(jnp.max(s[:, NK:], axis=-1, keepdims=True), sink_b)
            p_scr[rows, :] = jnp.exp(s - jnp.where(first_s, ma, mb)).astype(BF16)
            t_scr[rows, :] = jnp.where(first_o, jnp.exp(sink_a - ma), jnp.exp(sink_b - mb))
        for pr in range(pairs):
            pidx = g * pairs + pr
            rows = slice(pr * TQ, (pr + 1) * TQ)
            ov = jnp.dot(p_scr[rows, :], vv, preferred_element_type=F32)
            o = ov[:, :PW] / (ov[:, PW:] + t_scr[rows, :])
            o_ref[:, pidx * PW:(pidx + 1) * PW] = o.astype(o_ref.dtype)


def _pair_bias(bias_prev, bias_cur):
    full = jnp.concatenate([bias_prev, bias_cur], axis=-1)
    H, Q, NK = full.shape
    return jnp.transpose(full.reshape(H // 2, 2, Q, NK), (0, 2, 1, 3)).reshape(H // 2, Q, 2 * NK)


def _swa(q, k_prev_arr, v_prev_arr, prev_map, k_cur_arr, v_cur_arr, cur_map, n_cur, bias, sinks,
         B, L, TQ, mask_first):
    T = B * L
    nblk = L // TQ
    kvw = SWA_N_KV * SWA_HEAD_DIM
    n_keys = bias.shape[2] // 2
    n_prev = n_keys - n_cur
    assert n_keys & (n_keys - 1) == 0
    pairs = SWA_REP // 2
    return pl.pallas_call(
        functools.partial(_swa_kernel, mask_first=mask_first),
        grid=(B, nblk),
        in_specs=[pl.BlockSpec((TQ, SWA_N_HEADS * SWA_HEAD_DIM), lambda b, i: (b * nblk + i, 0)),
                  pl.BlockSpec((n_prev, kvw), prev_map[0]),
                  pl.BlockSpec((n_prev, kvw), prev_map[1]),
                  pl.BlockSpec((n_cur, kvw), cur_map[0]),
                  pl.BlockSpec((n_cur, kvw), cur_map[1]),
                  pl.BlockSpec(bias.shape, lambda b, i: (0, 0, 0)),
                  pl.BlockSpec(memory_space=pltpu.SMEM)],
        out_specs=pl.BlockSpec((TQ, SWA_N_HEADS * SWA_HEAD_DIM), lambda b, i: (b * nblk + i, 0)),
        out_shape=jax.ShapeDtypeStruct((T, SWA_N_HEADS * SWA_HEAD_DIM), BF16),
        scratch_shapes=[pltpu.VMEM((pairs * TQ, 2 * n_keys), F32),
                        pltpu.VMEM((pairs * TQ, 2 * n_keys), BF16),
                        pltpu.VMEM((pairs * TQ, 2 * SWA_HEAD_DIM), F32)],
        compiler_params=_params(("parallel", "arbitrary")),
        name="swa_attention",
    )(q, k_prev_arr, v_prev_arr, k_cur_arr, v_cur_arr, bias, sinks.astype(F32))


def _mem_kernel(q_ref, mk_ref, mv_ref, o_ref):
    scale = MEM_HEAD_DIM ** -0.5
    ndims = (((1,), (1,)), ((), ()))
    for h in range(MEM_N_HEADS):
        hs = slice(h * MEM_HEAD_DIM, (h + 1) * MEM_HEAD_DIM)
        s = lax.dot_general(q_ref[:, hs], mk_ref[:, hs], ndims, preferred_element_type=F32) * scale
        m = jnp.max(s, axis=-1, keepdims=True)
        p = jnp.exp(s - m)
        den = jnp.sum(p, axis=-1, keepdims=True)
        o = jnp.dot(p.astype(BF16), mv_ref[:, hs], preferred_element_type=F32)
        o_ref[:, hs] = (o / den).astype(o_ref.dtype)


def _mem_attend(q, mk, mv, L, tm):
    T, W = q.shape
    M = mk.shape[0] // (T // L)
    return pl.pallas_call(
        _mem_kernel,
        grid=(T // tm,),
        in_specs=[pl.BlockSpec((tm, W), lambda i: (i, 0)),
                  pl.BlockSpec((M, W), lambda i: ((i * tm) // L, 0)),
                  pl.BlockSpec((M, W), lambda i: ((i * tm) // L, 0))],
        out_specs=pl.BlockSpec((tm, W), lambda i: (i, 0)),
        out_shape=jax.ShapeDtypeStruct((T, W), BF16),
        compiler_params=_params(("parallel",)),
        name="mem_attention",
    )(q, mk, mv)


def _merge_kernel(ys_ref, os_ref, om_ref, w1_ref, w2_ref, w3_ref, g0_ref, g1_ref, g2_ref, o_ref):
    a = jnp.dot(ys_ref[...], w1_ref[...], preferred_element_type=F32)
    b = jnp.dot(os_ref[...], w2_ref[...], preferred_element_type=F32)
    c = jnp.dot(om_ref[...], w3_ref[...], preferred_element_type=F32)
    o = (g0_ref[...].astype(F32) * a + g1_ref[...].astype(F32) * b + g2_ref[...].astype(F32) * c)
    o_ref[...] = o.astype(o_ref.dtype)


def _merge(y_ssd, o_s, o_m, w1, w2, w3, gates, tm, tn):
    T = y_ssd.shape[0]
    D = w1.shape[1]
    nj = D // tn
    row = lambda i, j: (i, 0)
    colw = lambda i, j: (0, j)
    return pl.pallas_call(
        _merge_kernel,
        grid=(T // tm, nj),
        in_specs=[pl.BlockSpec((tm, y_ssd.shape[1]), row),
                  pl.BlockSpec((tm, o_s.shape[1]), row),
                  pl.BlockSpec((tm, o_m.shape[1]), row),
                  pl.BlockSpec((w1.shape[0], tn), colw),
                  pl.BlockSpec((w2.shape[0], tn), colw),
                  pl.BlockSpec((w3.shape[0], tn), colw),
                  pl.BlockSpec((tm, tn), lambda i, j: (i, j)),
                  pl.BlockSpec((tm, tn), lambda i, j: (i, j + nj)),
                  pl.BlockSpec((tm, tn), lambda i, j: (i, j + 2 * nj))],
        out_specs=pl.BlockSpec((tm, tn), lambda i, j: (i, j)),
        out_shape=jax.ShapeDtypeStruct((T, D), BF16),
        compiler_params=_params(("parallel", "arbitrary")),
        name="gated_merge",
    )(y_ssd, o_s, o_m, w1, w2, w3, gates, gates, gates)


def _norm_route_kernel(xa_ref, xb_ref, g_ref, whi_ref, wlo_ref, h_ref, r_ref, *, n_a):
    x = jnp.where(pl.program_id(0) < n_a, xa_ref[...], xb_ref[...])
    ms = jnp.mean(x * x, axis=-1, keepdims=True)
    h = x * lax.rsqrt(ms + EPS) * g_ref[...]
    h_ref[:, 0, :] = h
    hb = h.astype(BF16)
    lo = (h - hb.astype(F32)).astype(BF16)
    lg = (jnp.dot(hb, whi_ref[...], preferred_element_type=F32)
          + jnp.dot(lo, whi_ref[...], preferred_element_type=F32)
          + jnp.dot(hb, wlo_ref[...], preferred_element_type=F32))
    lane = lax.broadcasted_iota(jnp.int32, lg.shape, 1)
    lane_f = lane.astype(F32)
    far = float(LANES)
    gl = jnp.where((lane >= N_EXPERTS) & (lane < N_EXPERTS + N_EXPERT_GROUPS), lg, NEG_INF)
    gmax = jnp.max(gl, axis=-1, keepdims=True)
    gidx = jnp.min(jnp.where(gl == gmax, lane_f - N_EXPERTS, far), axis=-1, keepdims=True)
    gw = 1.0 / jnp.sum(jnp.exp(gl - gmax), axis=-1, keepdims=True)
    lo_e = gidx * EXPERTS_PER_GROUP
    el = jnp.where((lane_f >= lo_e) & (lane_f < lo_e + EXPERTS_PER_GROUP), lg, NEG_INF)
    v1 = jnp.max(el, axis=-1, keepdims=True)
    i1 = jnp.min(jnp.where(el == v1, lane_f, far), axis=-1, keepdims=True)
    el2 = jnp.where(lane_f == i1, NEG_INF, el)
    v2 = jnp.max(el2, axis=-1, keepdims=True)
    i2 = jnp.min(jnp.where(el2 == v2, lane_f, far), axis=-1, keepdims=True)
    e = jnp.exp(v2 - v1)
    w1 = gw / (1.0 + e)
    w2 = gw * e / (1.0 + e)
    r_ref[...] = jnp.where(lane == 0, w1, jnp.where(lane == 1, w2,
                           jnp.where(lane == 2, i1, jnp.where(lane == 3, i2, 0.0))))


def _norm_route(xa, xb, g, w_hi, w_lo, tm):
    Ta, D = xa.shape
    Tb = xb.shape[0]
    T = Ta + Tb
    n_a, n_b = Ta // tm, Tb // tm
    assert n_a * tm == Ta and n_b * tm == Tb
    return pl.pallas_call(
        functools.partial(_norm_route_kernel, n_a=n_a),
        grid=(n_a + n_b,),
        in_specs=[pl.BlockSpec((tm, D), lambda i: (jnp.minimum(i, n_a - 1), 0)),
                  pl.BlockSpec((tm, D), lambda i: (jnp.maximum(i - n_a, 0), 0)),
                  pl.BlockSpec((1, D), lambda i: (0, 0)),
                  pl.BlockSpec((D, LANES), lambda i: (0, 0)),
                  pl.BlockSpec((D, LANES), lambda i: (0, 0))],
        out_specs=[pl.BlockSpec((tm, 1, D), lambda i: (i, 0, 0)),
                   pl.BlockSpec((tm, LANES), lambda i: (i, 0))],
        out_shape=[jax.ShapeDtypeStruct((T, 1, D), F32),
                   jax.ShapeDtypeStruct((T, LANES), F32)],
        compiler_params=_params(("parallel",)),
        name="ffn_norm_router",
    )(xa, xb, g.reshape(1, D).astype(F32), w_hi, w_lo)


def _moe_kernel(be_ref, nv_ref, s0_ref, ord_ref, h_hbm, w1_ref, w3_ref, w2_ref, yu_hbm,
                xbuf, ybuf, w1b, w3b, w2b, sem_in, sem_out, *, n_tok):
    i = pl.program_id(0)
    nb = pl.num_programs(0)
    rows = xbuf.shape[1]
    n_assign = TOP_K * n_tok
    nv = nv_ref[i]
    slot = i % 2
    nxt = jnp.minimum(i + 1, nb - 1)
    next_valid = (i + 1 < nb) & (nv_ref[nxt] > 0)

    def assignment(blk, r):
        return ord_ref[jnp.minimum(s0_ref[blk] + r, n_assign - 1)]

    def row_in(s, r, tok):
        return pltpu.make_async_copy(h_hbm.at[tok], xbuf.at[s, pl.ds(r, 1), :], sem_in.at[s])

    def row_out(s, r, dst):
        return pltpu.make_async_copy(ybuf.at[s, pl.ds(r, 1), :], yu_hbm.at[dst], sem_out.at[s])

    def gather_start(blk, s):
        for r in range(rows):
            row_in(s, r, lax.shift_right_logical(assignment(blk, r), 1)).start()

    def gather_wait(s):
        for r in range(rows):
            row_in(s, r, 0).wait()

    def scatter_start(blk, s):
        n_valid = nv_ref[blk]
        for r in range(rows):
            m = assignment(blk, r)
            dst = jnp.where(r < n_valid, (m & 1) * n_tok + lax.shift_right_logical(m, 1),
                            n_assign + s * rows + r)
            row_out(s, r, dst).start()

    def scatter_wait(s):
        for r in range(rows):
            row_out(s, r, 0).wait()

    @pl.when(i == 0)
    def _clear_spare_rows():
        ybuf[...] = jnp.zeros(ybuf.shape, ybuf.dtype)
        for s in range(2):
            for r in range(rows):
                row_out(s, r, n_assign + s * rows + r).start()
            scatter_wait(s)

    @pl.when((i == 0) & (nv > 0))
    def _prologue():
        gather_start(i, 0)

    @pl.when(next_valid)
    def _prefetch():
        gather_start(nxt, 1 - slot)

    prev = be_ref[jnp.maximum(i - 1, 0)]

    @pl.when((i == 0) | (be_ref[i] != prev))
    def _load_expert():
        w1b[...] = w1_ref[0].astype(BF16)
        w3b[...] = w3_ref[0].astype(BF16)
        w2b[...] = w2_ref[0].astype(BF16)

    @pl.when(nv > 0)
    def _compute():
        gather_wait(slot)

        @pl.when(i >= 2)
        def _free_ybuf():
            scatter_wait(slot)

        x = xbuf[slot].astype(BF16)
        a = jnp.dot(x, w1b[...], preferred_element_type=F32)
        b = jnp.dot(x, w3b[...], preferred_element_type=F32)
        mid = ((a * _sigmoid(a)) * b).astype(BF16)
        y = jnp.dot(mid, w2b[...], preferred_element_type=F32)
        ybuf[slot] = y
        scatter_start(i, slot)

        @pl.when(jnp.logical_not(next_valid))
        def _drain():
            scatter_wait(slot)

            @pl.when(i >= 1)
            def _drain_prev():
                scatter_wait(1 - slot)


def _moe(h2, route, w_gate, w_up, w_down):
    T, _, D = h2.shape
    F = w_gate.shape[2]
    assert TOP_K == 2
    M = T * TOP_K
    BM = MOE_ROWS
    nb = (M + N_EXPERTS * (BM - 1) + BM - 1) // BM
    e_flat = route[:, TOP_K:2 * TOP_K].astype(jnp.int32).reshape(M)
    _, order = lax.sort((e_flat, jnp.arange(M, dtype=jnp.int32)), num_keys=1, is_stable=True)
    experts = jnp.arange(N_EXPERTS, dtype=jnp.int32)
    counts = jnp.sum((experts[:, None] == e_flat[None, :]).astype(jnp.int32), axis=1)
    padded = (counts + BM - 1) // BM * BM
    pad_end = jnp.cumsum(padded)
    pad_start = pad_end - padded
    start = jnp.cumsum(counts) - counts
    blk0 = jnp.arange(nb, dtype=jnp.int32) * BM
    blk_exp = jnp.minimum(jnp.sum((pad_end[None, :] <= blk0[:, None]).astype(jnp.int32), axis=1), N_EXPERTS - 1)
    pick = (blk_exp[:, None] == experts[None, :]).astype(jnp.int32)
    off0 = blk0 - jnp.sum(pick * pad_start[None, :], axis=1)
    blk_nv = jnp.clip(jnp.sum(pick * counts[None, :], axis=1) - off0, 0, BM).astype(jnp.int32)
    blk_s0 = jnp.clip(jnp.sum(pick * start[None, :], axis=1) + off0, 0, M - 1).astype(jnp.int32)

    grid_spec = pltpu.PrefetchScalarGridSpec(
        num_scalar_prefetch=4,
        grid=(nb,),
        in_specs=[pl.BlockSpec(memory_space=pl.ANY),
                  pl.BlockSpec((1, D, F), lambda i, be, nv, s0, od: (be[i], 0, 0)),
                  pl.BlockSpec((1, D, F), lambda i, be, nv, s0, od: (be[i], 0, 0)),
                  pl.BlockSpec((1, F, D), lambda i, be, nv, s0, od: (be[i], 0, 0))],
        out_specs=pl.BlockSpec(memory_space=pl.ANY),
        scratch_shapes=[pltpu.VMEM((2, BM, D), F32),
                        pltpu.VMEM((2, BM, D), F32),
                        pltpu.VMEM((D, F), BF16),
                        pltpu.VMEM((D, F), BF16),
                        pltpu.VMEM((F, D), BF16),
                        pltpu.SemaphoreType.DMA((2,)),
                        pltpu.SemaphoreType.DMA((2,))],
    )
    return pl.pallas_call(
        functools.partial(_moe_kernel, n_tok=T),
        grid_spec=grid_spec,
        out_shape=jax.ShapeDtypeStruct((M + 2 * BM, 1, D), F32),
        compiler_params=_params(("arbitrary",)),
        name="moe_experts",
    )(blk_exp.astype(jnp.int32), blk_nv, blk_s0, order, h2, w_gate, w_up, w_down)


def _combine_kernel(x_ref, r_ref, y0_ref, y1_ref, o_ref):
    w0 = r_ref[:, 0:1]
    w1 = r_ref[:, 1:2]
    o_ref[...] = x_ref[...] + (w0 * y0_ref[:, 0, :] + w1 * y1_ref[:, 0, :])


def _combine(x, route, yu, blk0, tm):
    T, D = x.shape
    n_all = route.shape[0] // tm
    return pl.pallas_call(
        _combine_kernel,
        grid=(T // tm,),
        in_specs=[pl.BlockSpec((tm, D), lambda i: (i, 0)),
                  pl.BlockSpec((tm, LANES), lambda i: (blk0 + i, 0)),
                  pl.BlockSpec((tm, 1, D), lambda i: (blk0 + i, 0, 0)),
                  pl.BlockSpec((tm, 1, D), lambda i: (n_all + blk0 + i, 0, 0))],
        out_specs=pl.BlockSpec((tm, D), lambda i: (i, 0)),
        out_shape=jax.ShapeDtypeStruct((T, D), F32),
        compiler_params=_params(("parallel",)),
        name="moe_combine",
    )(x, route, yu, yu)


def _layer(x, B, L, p, conv_prev8, h0_t, mk, mv, swa_prev, swa_bias, Lc, TQ):
    T, D = x.shape
    tm = min(512, T)
    tmm = min(MM_ROWS, T)
    h = _rmsnorm(x, p["g_mix"], tm)
    z = _matmul(h, p["wz"], BF16, tmm, MM_COLS, name="proj_z")
    xbc = _matmul(h, p["wxbc"], BF16, tmm, MM_COLS, name="proj_xbc")
    dt = _matmul(h, p["wdt"], F32, tmm, PAD_HEADS, name="proj_dt")
    q_s = _matmul(h, p["wqs"], BF16, tmm, 512, epi=_epi_group_norm, col_extras=(p["g_q_swa"],),
                  const_extras=(p["mavg"],), name="proj_q_swa")
    kv = _matmul(h, p["wkv"], F32, tmm, 256, epi=_epi_kv, col_extras=(p["g_kv"],),
                 const_extras=(p["mavg"],), name="proj_kv_swa")
    q_m = _matmul(h, p["wqm"], BF16, tmm, MEM_HEAD_DIM, epi=_epi_row_norm, col_extras=(p["g_q_mem"],),
                  name="proj_q_mem")
    gates = _matmul(h, p["wg"], BF16, tmm, MM_COLS, epi=_epi_sigmoid, name="proj_gates")

    y_ssd, h_t, conv8 = _ssd(z, xbc, dt, conv_prev8, h0_t, p, B, L, Lc)

    nblk = L // TQ
    kvw = SWA_N_KV * SWA_HEAD_DIM
    if swa_prev is None:
        prev_map = (lambda b, i: (b * nblk + jnp.maximum(i - 1, 0), 0),
                    lambda b, i: (b * nblk + jnp.maximum(i - 1, 0), 1))
        cur_map = (lambda b, i: (b * nblk + i, 0), lambda b, i: (b * nblk + i, 1))
        o_s = _swa(q_s, kv, kv, prev_map, kv, kv, cur_map, TQ, swa_bias, p["sinks"], B, L, TQ, True)
    else:
        n_cur = swa_bias.shape[2] // 2 - swa_prev[0].shape[0] // B
        kv3 = jnp.pad(kv.reshape(B, L, 2 * kvw), ((0, 0), (0, n_cur - L), (0, 0)))
        k_new = kv3[:, :, :kvw].reshape(B * n_cur, kvw)
        v_new = kv3[:, :, kvw:].reshape(B * n_cur, kvw)
        per_b = (lambda b, i: (b, 0), lambda b, i: (b, 0))
        o_s = _swa(q_s, swa_prev[0], swa_prev[1], per_b, k_new, v_new, per_b, n_cur, swa_bias, p["sinks"],
                   B, L, TQ, False)

    o_m = _mem_attend(q_m, mk, mv, L, min(256, L))

    merged = _merge(y_ssd, o_s, o_m, p["w_o_ssd"], p["w_o_swa"], p["w_o_mem"], gates, tm, 512)
    x1 = _matmul(merged, p["w_out"], F32, tmm, MM_COLS, epi=_epi_residual, tile_extras=(x,), name="proj_out")

    return x1, conv8, h_t, kv


def _state_to_heads(h_t, B):
    return jnp.transpose(h_t.reshape(B, SSD_D_STATE, SSD_N_HEADS, SSD_HEAD_DIM), (0, 2, 3, 1))


def kernel(x_prompt, x_sample, cache_conv, state_ssd, cache_swa_k, cache_swa_v, cache_mem_k, cache_mem_v, mem_prompt, rel_bias_table, g_mix, w_in, conv_w, conv_b, dt_bias, a_log, d_skip, g_ssd, w_o_ssd, g_q_swa, g_k_swa, sinks, w_o_swa, g_mem, w_mem_k, w_mem_v, g_q_mem, g_k_mem, w_o_mem, w_out, g_ffn, w_router_grp, w_router_exp, w_exp_gate, w_exp_up, w_exp_down):
    B, S, D = x_prompt.shape
    Bd, Sd, _ = x_sample.shape
    depth = w_in.shape[0]
    assert depth == 1
    l = 0
    kvw = SWA_N_KV * SWA_HEAD_DIM
    qw = SWA_N_HEADS * SWA_HEAD_DIM
    mw = MEM_N_HEADS * MEM_HEAD_DIM

    sizes = (SSD_D_INNER, SSD_CONV_DIM, SSD_N_HEADS, qw, kvw, kvw, mw, 3 * D)
    offs = np.concatenate([[0], np.cumsum(sizes)])
    w = w_in[l]
    cols = [w[:, int(offs[k]):int(offs[k + 1])] for k in range(len(sizes))]
    pad_h = PAD_HEADS - SSD_N_HEADS
    w_r = jnp.pad(jnp.concatenate([w_router_exp[l], w_router_grp[l]], axis=1),
                  ((0, 0), (0, LANES - N_EXPERTS - N_EXPERT_GROUPS)))
    w_r_hi = w_r.astype(BF16)
    p = {
        "g_mix": g_mix[l],
        "wz": cols[0].astype(BF16),
        "wxbc": cols[1].astype(BF16),
        "wdt": jnp.pad(cols[2], ((0, 0), (0, pad_h))).astype(BF16),
        "wqs": cols[3].astype(BF16),
        "wkv": jnp.concatenate([cols[4], cols[5]], axis=1).astype(BF16),
        "wqm": cols[6].astype(BF16),
        "wg": cols[7].astype(BF16),
        "conv_w": conv_w[l].astype(F32),
        "conv_b": conv_b[l].reshape(1, SSD_CONV_DIM).astype(F32),
        "dt_bias": jnp.pad(dt_bias[l], (0, pad_h)).reshape(1, PAD_HEADS).astype(F32),
        "a_log": jnp.pad(a_log[l], (0, pad_h)).reshape(1, PAD_HEADS).astype(F32),
        "d_skip": jnp.repeat(d_skip[l], SSD_HEAD_DIM).reshape(1, SSD_D_INNER).astype(F32),
        "g_ssd": g_ssd[l].reshape(1, SSD_D_INNER).astype(F32),
        "g_q_swa": (jnp.tile(g_q_swa[l], SWA_N_HEADS) * SWA_HEAD_DIM ** -0.5).reshape(1, qw).astype(F32),
        "g_kv": jnp.concatenate([jnp.tile(g_k_swa[l], SWA_N_KV), jnp.ones((kvw,), F32)]).reshape(1, 2 * kvw),
        "g_q_mem": jnp.tile(g_q_mem[l], MEM_N_HEADS).reshape(1, mw).astype(F32),
        "mavg": _group_mean_matrix(256, SWA_HEAD_DIM),
        "sinks": sinks[l],
        "w_o_ssd": w_o_ssd[l].astype(BF16),
        "w_o_swa": w_o_swa[l].astype(BF16),
        "w_o_mem": w_o_mem[l].astype(BF16),
        "w_out": w_out[l].astype(BF16),
        "g_ffn": g_ffn[l],
        "w_r_hi": w_r_hi,
        "w_r_lo": (w_r - w_r_hi.astype(F32)).astype(BF16),
        "w_exp_gate": w_exp_gate[l],
        "w_exp_up": w_exp_up[l],
        "w_exp_down": w_exp_down[l],
    }

    M = mem_prompt.shape[1]
    mn = _rmsnorm(mem_prompt.reshape(B * M, D), g_mem[l], min(256, B * M))
    mk_p = _matmul(mn, w_mem_k[l].astype(BF16), F32, min(256, B * M), MEM_HEAD_DIM, epi=_epi_row_norm,
                   col_extras=(jnp.tile(g_k_mem[l], MEM_N_HEADS).reshape(1, mw).astype(F32),), name="mem_k")
    mv_p = _matmul(mn, w_mem_v[l].astype(BF16), F32, min(256, B * M), MEM_HEAD_DIM, name="mem_v")

    TQ = 2 * CHUNK
    qpos = np.arange(TQ)
    bias_p = _pair_bias(_bias_from_table(_bucket_map(qpos, np.arange(TQ) - TQ), rel_bias_table),
                        _bias_from_table(_bucket_map(qpos, np.arange(TQ)), rel_bias_table))
    C = cache_swa_k.shape[2]
    qpos_s = PAST_LEN + np.arange(Sd)
    cur_map_s = _bucket_map(qpos_s, PAST_LEN + np.arange(C))
    cur_map_s[:, Sd:] = -1
    bias_s = _pair_bias(_bias_from_table(_bucket_map(qpos_s, PAST_LEN - C + np.arange(C)), rel_bias_table),
                        _bias_from_table(cur_map_s, rel_bias_table))

    conv0 = jnp.zeros((B, SUBLANES, SSD_CONV_DIM), F32)
    h0 = jnp.zeros((B, SSD_D_STATE, SSD_D_INNER), F32)
    x1_p, conv8_p, ht_p, kv_p = _layer(x_prompt.reshape(B * S, D), B, S, p, conv0, h0,
                                     mk_p.astype(BF16), mv_p.astype(BF16), None,
                                     bias_p, CHUNK, TQ)
    conv_prev = jnp.pad(cache_conv[l], ((0, 0), (SUBLANES - (SSD_CONV - 1), 0), (0, 0)))
    h0_s = jnp.transpose(state_ssd[l], (0, 3, 1, 2)).reshape(Bd, SSD_D_STATE, SSD_D_INNER)
    x1_s, conv8_s, ht_s, kv_s = _layer(x_sample.reshape(Bd * Sd, D), Bd, Sd, p, conv_prev, h0_s,
                                     cache_mem_k[l].reshape(Bd * M, mw).astype(BF16),
                                     cache_mem_v[l].reshape(Bd * M, mw).astype(BF16),
                                     (cache_swa_k[l].reshape(Bd * C, kvw), cache_swa_v[l].reshape(Bd * C, kvw)),
                                     bias_s, Sd, Sd)

    tr = math.gcd(B * S, Bd * Sd, MOE_ROWS)
    h2, route = _norm_route(x1_p, x1_s, p["g_ffn"], p["w_r_hi"], p["w_r_lo"], tr)
    yu = _moe(h2, route, p["w_exp_gate"], p["w_exp_up"], p["w_exp_down"])
    yp = _combine(x1_p, route, yu, 0, tr)
    ys = _combine(x1_s, route, yu, (B * S) // tr, tr)

    keep = min(WINDOW, S)
    kv_p = kv_p.reshape(B, S, 2, SWA_N_KV, SWA_HEAD_DIM)[:, S - keep:]
    kv_s = kv_s.reshape(Bd, Sd, 2, SWA_N_KV, SWA_HEAD_DIM)
    tail = SUBLANES - (SSD_CONV - 1)
    return (yp.reshape(B, S, D), ys.reshape(Bd, Sd, D),
            conv8_p[None, :, tail:], _state_to_heads(ht_p, B)[None],
            kv_p[None, :, :, 0], kv_p[None, :, :, 1],
            mk_p.reshape(1, B, M, MEM_N_HEADS, MEM_HEAD_DIM), mv_p.reshape(1, B, M, MEM_N_HEADS, MEM_HEAD_DIM),
            conv8_s[None, :, tail:], _state_to_heads(ht_s, Bd)[None],
            kv_s[None, :, :, 0], kv_s[None, :, :, 1])
```

```python
import functools
import math

import numpy as np
import jax
import jax.numpy as jnp
from jax import lax
from jax.experimental import pallas as pl
from jax.experimental.pallas import tpu as pltpu

F32 = jnp.float32
BF16 = jnp.bfloat16
EPS = 1e-6
NEG_INF = float("-inf")

CHUNK = 64
SSD_HEAD_DIM = 64
SSD_N_HEADS = 64
SSD_N_GROUPS = 8
SSD_D_STATE = 128
SSD_D_INNER = SSD_N_HEADS * SSD_HEAD_DIM
SSD_GN = SSD_N_GROUPS * SSD_D_STATE
SSD_CONV_DIM = SSD_D_INNER + 2 * SSD_GN
SSD_CONV = 4
SWA_N_HEADS = 32
SWA_N_KV = 4
SWA_HEAD_DIM = 64
SWA_REP = SWA_N_HEADS // SWA_N_KV
WINDOW = 128
WINDOW_CHUNKS = WINDOW // CHUNK
MEM_N_HEADS = 4
MEM_HEAD_DIM = 512
N_BUCKETS = 32
MAX_DISTANCE = 128
N_EXPERT_GROUPS = 4
EXPERTS_PER_GROUP = 16
N_EXPERTS = N_EXPERT_GROUPS * EXPERTS_PER_GROUP
TOP_K = 2
PAST_LEN = 4096

LANES = 128
SUBLANES = 8
VMEM_LIMIT = 56 * 1024 * 1024
MOE_ROWS = 256
MM_ROWS = 1024
MM_COLS = 1024
MERGE_COLS = 256
PAD_HEADS = LANES


def _params(sem, vmem=VMEM_LIMIT):
    return pltpu.CompilerParams(dimension_semantics=sem, vmem_limit_bytes=vmem)


def _sigmoid(x):
    return 0.5 * (jnp.tanh(0.5 * x) + 1.0)


def _silu(x):
    u = 0.5 * x
    return u * (jnp.tanh(u) + 1.0)


def _split3(x):
    x1 = x.astype(BF16)
    r1 = x - x1.astype(F32)
    x2 = r1.astype(BF16)
    x3 = (r1 - x2.astype(F32)).astype(BF16)
    return x1, x2, x3


def _rmsnorm_kernel(x_ref, g_ref, o_ref):
    x = x_ref[...]
    ms = jnp.mean(x * x, axis=-1, keepdims=True)
    o_ref[...] = (x * lax.rsqrt(ms + EPS) * g_ref[...]).astype(o_ref.dtype)


def _rmsnorm(x, g, tm):
    T, D = x.shape
    return pl.pallas_call(
        _rmsnorm_kernel,
        grid=(T // tm,),
        in_specs=[pl.BlockSpec((tm, D), lambda i: (i, 0)),
                  pl.BlockSpec((1, D), lambda i: (0, 0))],
        out_specs=pl.BlockSpec((tm, D), lambda i: (i, 0)),
        out_shape=jax.ShapeDtypeStruct((T, D), BF16),
        compiler_params=_params(("parallel",)),
        name="rmsnorm",
    )(x, g.reshape(1, D).astype(F32))


def _mm_kernel(a_ref, b_ref, *refs, epi):
    o_ref = refs[-1]
    acc = jnp.dot(a_ref[...], b_ref[...], preferred_element_type=F32)
    if epi is not None:
        acc = epi(acc, *[r[...] for r in refs[:-1]])
    o_ref[...] = acc.astype(o_ref.dtype)


def _matmul(a, b, out_dtype, tm, tn, epi=None, col_extras=(), tile_extras=(), const_extras=(), name="matmul"):
    M, K = a.shape
    N = b.shape[1]
    assert M % tm == 0 and N % tn == 0, (M, N, tm, tn)
    in_specs = [pl.BlockSpec((tm, K), lambda i, j: (i, 0)),
                pl.BlockSpec((K, tn), lambda i, j: (0, j))]
    for _ in col_extras:
        in_specs.append(pl.BlockSpec((1, tn), lambda i, j: (0, j)))
    for _ in tile_extras:
        in_specs.append(pl.BlockSpec((tm, tn), lambda i, j: (i, j)))
    for c in const_extras:
        in_specs.append(pl.BlockSpec(c.shape, lambda i, j: (0, 0)))
    return pl.pallas_call(
        functools.partial(_mm_kernel, epi=epi),
        grid=(M // tm, N // tn),
        in_specs=in_specs,
        out_specs=pl.BlockSpec((tm, tn), lambda i, j: (i, j)),
        out_shape=jax.ShapeDtypeStruct((M, N), out_dtype),
        compiler_params=_params(("parallel", "arbitrary")),
        name=name,
    )(a, b, *col_extras, *tile_extras, *const_extras)


def _norm_mm_kernel(x_ref, g_ref, b_ref, o_ref, h_ref):
    @pl.when(pl.program_id(1) == 0)
    def _normalise():
        x = x_ref[...]
        ms = jnp.mean(x * x, axis=-1, keepdims=True)
        h_ref[...] = (x * lax.rsqrt(ms + EPS) * g_ref[...]).astype(h_ref.dtype)

    o_ref[...] = jnp.dot(h_ref[...], b_ref[...], preferred_element_type=F32).astype(o_ref.dtype)


def _norm_matmul(x, g, b, out_dtype, tm, tn, name):
    M, K = x.shape
    N = b.shape[1]
    assert M % tm == 0 and N % tn == 0, (M, N, tm, tn)
    return pl.pallas_call(
        _norm_mm_kernel,
        grid=(M // tm, N // tn),
        in_specs=[pl.BlockSpec((tm, K), lambda i, j: (i, 0)),
                  pl.BlockSpec((1, K), lambda i, j: (0, 0)),
                  pl.BlockSpec((K, tn), lambda i, j: (0, j))],
        out_specs=[pl.BlockSpec((tm, tn), lambda i, j: (i, j)),
                   pl.BlockSpec((tm, K), lambda i, j: (i, 0))],
        out_shape=[jax.ShapeDtypeStruct((M, N), out_dtype),
                   jax.ShapeDtypeStruct((M, K), BF16)],
        compiler_params=_params(("parallel", "arbitrary")),
        name=name,
    )(x, g.reshape(1, K).astype(F32), b)


def _group_mean_matrix(width, group):
    idx = np.arange(width) // group
    return jnp.asarray((idx[:, None] == idx[None, :]).astype(np.float32) / group, dtype=BF16)


def _epi_group_norm(acc, gain, mavg):
    w = mavg.shape[0]
    outs = []
    for c in range(acc.shape[1] // w):
        a = acc[:, c * w:(c + 1) * w]
        s = a * a
        hi = s.astype(BF16)
        lo = (s - hi.astype(F32)).astype(BF16)
        ms = (jnp.dot(hi, mavg, preferred_element_type=F32)
              + jnp.dot(lo, mavg, preferred_element_type=F32))
        outs.append(a * lax.rsqrt(ms + EPS))
    normed = outs[0] if len(outs) == 1 else jnp.concatenate(outs, axis=1)
    return normed * gain


def _epi_kv(acc, gain, mavg):
    normed = _epi_group_norm(acc, gain, mavg)
    return jnp.where(pl.program_id(1) == 0, normed, acc)


def _epi_row_norm(acc, gain):
    ms = jnp.mean(acc * acc, axis=-1, keepdims=True)
    return acc * lax.rsqrt(ms + EPS) * gain


def _epi_sigmoid(acc):
    return _sigmoid(acc)


def _epi_residual(acc, res):
    return acc + res


def _ssd_kernel(z_ref, xbc_ref, dt_ref, cprev_ref, h0_ref, cw_ref, cb_ref, dtb_ref, alog_ref,
                dskip_ref, gn_ref, y_ref, hout_ref, cout_ref, xp_s, h_s, conv_s, y_s, *, Lc):
    c = pl.program_id(1)
    n_chunks = pl.num_programs(1)
    P2 = 2 * SSD_HEAD_DIM
    L2 = 2 * Lc

    @pl.when(c == 0)
    def _init():
        xp_s[0:SUBLANES, :] = cprev_ref[0]
        h_s[...] = h0_ref[0]

    xp_s[SUBLANES:SUBLANES + Lc, :] = xbc_ref[...].astype(F32)
    cblk = 512
    row8 = lax.broadcasted_iota(jnp.int32, (SUBLANES, cblk), 0)
    for j in range(SSD_CONV_DIM // cblk):
        sl = slice(j * cblk, (j + 1) * cblk)
        cur = xp_s[SUBLANES:SUBLANES + Lc, sl]
        tail = xp_s[0:SUBLANES, sl]
        acc = cb_ref[:, sl] + cw_ref[SSD_CONV - 1:SSD_CONV, sl] * cur
        for k in range(SSD_CONV - 1):
            sh = SSD_CONV - 1 - k
            down = pltpu.roll(cur, sh, axis=0)
            top = jnp.where(row8 >= sh, down[0:SUBLANES], pltpu.roll(tail, sh, axis=0))
            acc = acc + cw_ref[k:k + 1, sl] * jnp.concatenate([top, down[SUBLANES:]], axis=0)
        conv_s[:, sl] = _silu(acc)
    xp_s[0:SUBLANES, :] = xp_s[Lc:Lc + SUBLANES, :]

    dtv = dt_ref[...] + dtb_ref[...]
    dt = jnp.maximum(dtv, 0.0) + jnp.log1p(jnp.exp(-jnp.abs(dtv)))
    adt = dt * (-jnp.exp(alog_ref[...]))
    row = lax.broadcasted_iota(jnp.int32, (Lc, Lc), 0)
    col = lax.broadcasted_iota(jnp.int32, (Lc, Lc), 1)
    tri = (col <= row).astype(BF16)
    row2 = lax.broadcasted_iota(jnp.int32, (Lc, L2), 0)
    col2 = lax.broadcasted_iota(jnp.int32, (Lc, L2), 1)
    col2m = jnp.where(col2 >= Lc, col2 - Lc, col2)
    tri_t2 = (row2 <= col2m).astype(BF16)
    causal2 = col2m <= row2
    a1, a2, a3 = _split3(adt)
    acs = (jnp.dot(tri, a1, preferred_element_type=F32)
           + jnp.dot(tri, a2, preferred_element_type=F32)
           + jnp.dot(tri, a3, preferred_element_type=F32))
    tdims = (((0,), (0,)), ((), ()))
    acs_t2 = (lax.dot_general(a1, tri_t2, tdims, preferred_element_type=F32)
              + lax.dot_general(a2, tri_t2, tdims, preferred_element_type=F32)
              + lax.dot_general(a3, tri_t2, tdims, preferred_element_type=F32))

    lane_p = lax.broadcasted_iota(jnp.int32, (Lc, P2), 1)
    first_p = lane_p < SSD_HEAD_DIM
    first_l = col2 < Lc
    first_l1 = first_l[0:1, :]
    ndims = (((1,), (1,)), ((), ()))

    for g in range(SSD_N_GROUPS):
        b_g = conv_s[:, SSD_D_INNER + g * SSD_D_STATE:SSD_D_INNER + (g + 1) * SSD_D_STATE].astype(BF16)
        c_g = conv_s[:, SSD_D_INNER + SSD_GN + g * SSD_D_STATE:
                     SSD_D_INNER + SSD_GN + (g + 1) * SSD_D_STATE].astype(BF16)
        b2 = jnp.concatenate([b_g, b_g], axis=0)
        cb2 = lax.dot_general(c_g, b2, ndims, preferred_element_type=F32)
        gw = SSD_HEAD_DIM * (SSD_N_HEADS // SSD_N_GROUPS)
        inter = jnp.dot(c_g, h_s[:, g * gw:(g + 1) * gw].astype(BF16), preferred_element_type=F32)
        for jj in range(gw // P2):
            j = g * (gw // P2) + jj
            sl = slice(j * P2, (j + 1) * P2)
            acs_a = acs[:, 2 * j:2 * j + 1]
            acs_b = acs[:, 2 * j + 1:2 * j + 2]
            col_l = jnp.where(first_l, acs_a, acs_b)
            row_l = jnp.where(first_l1, acs_t2[2 * j:2 * j + 1, :], acs_t2[2 * j + 1:2 * j + 2, :])
            dec = jnp.exp(jnp.where(causal2, col_l - row_l, NEG_INF))
            m_pair = (cb2 * dec).astype(BF16)
            col_p = col_l if L2 == P2 else jnp.where(first_p, acs_a, acs_b)
            dt_p = jnp.where(first_p, dt[:, 2 * j:2 * j + 1], dt[:, 2 * j + 1:2 * j + 2])
            xs_p = conv_s[:, sl]
            xdt = xs_p * dt_p
            rhs = jnp.concatenate([jnp.where(first_p, xdt, 0.0), jnp.where(first_p, 0.0, xdt)],
                                  axis=0).astype(BF16)
            y = jnp.dot(m_pair, rhs, preferred_element_type=F32)
            y = y + inter[:, jj * P2:(jj + 1) * P2] * jnp.exp(col_p) + dskip_ref[:, sl] * xs_p
            y_s[:, sl] = y
            a_end = col_p[Lc - 1:Lc, :]
            xw = (xdt * jnp.exp(a_end - col_p)).astype(BF16)
            h_s[:, sl] = (h_s[:, sl] * jnp.exp(a_end)
                          + lax.dot_general(b_g, xw, tdims, preferred_element_type=F32))

    gdim = SSD_D_INNER // SSD_N_GROUPS
    for g in range(SSD_N_GROUPS):
        sl = slice(g * gdim, (g + 1) * gdim)
        zz = z_ref[:, sl].astype(F32)
        yy = y_s[:, sl] * _silu(zz)
        ms = jnp.mean(yy * yy, axis=-1, keepdims=True)
        y_ref[:, sl] = (yy * lax.rsqrt(ms + EPS) * gn_ref[:, sl]).astype(y_ref.dtype)

    @pl.when(c == n_chunks - 1)
    def _fin():
        hout_ref[0] = h_s[...]
        cout_ref[0] = xp_s[0:SUBLANES, :]


def _ssd(z, xbc, dt, conv_prev8, h0_t, p, B, L, Lc):
    T = B * L
    nc = L // Lc
    tok = lambda b, c: (b * nc + c, 0)
    per_b = lambda b, c: (b, 0, 0)
    whole = lambda b, c: (0, 0)
    y, h_t, conv8 = pl.pallas_call(
        functools.partial(_ssd_kernel, Lc=Lc),
        grid=(B, nc),
        in_specs=[pl.BlockSpec((Lc, SSD_D_INNER), tok),
                  pl.BlockSpec((Lc, SSD_CONV_DIM), tok),
                  pl.BlockSpec((Lc, PAD_HEADS), tok),
                  pl.BlockSpec((1, SUBLANES, SSD_CONV_DIM), per_b),
                  pl.BlockSpec((1, SSD_D_STATE, SSD_D_INNER), per_b),
                  pl.BlockSpec((SSD_CONV, SSD_CONV_DIM), whole),
                  pl.BlockSpec((1, SSD_CONV_DIM), whole),
                  pl.BlockSpec((1, PAD_HEADS), whole),
                  pl.BlockSpec((1, PAD_HEADS), whole),
                  pl.BlockSpec((1, SSD_D_INNER), whole),
                  pl.BlockSpec((1, SSD_D_INNER), whole)],
        out_specs=[pl.BlockSpec((Lc, SSD_D_INNER), tok),
                   pl.BlockSpec((1, SSD_D_STATE, SSD_D_INNER), per_b),
                   pl.BlockSpec((1, SUBLANES, SSD_CONV_DIM), per_b)],
        out_shape=[jax.ShapeDtypeStruct((T, SSD_D_INNER), BF16),
                   jax.ShapeDtypeStruct((B, SSD_D_STATE, SSD_D_INNER), F32),
                   jax.ShapeDtypeStruct((B, SUBLANES, SSD_CONV_DIM), F32)],
        scratch_shapes=[pltpu.VMEM((SUBLANES + Lc, SSD_CONV_DIM), F32),
                        pltpu.VMEM((SSD_D_STATE, SSD_D_INNER), F32),
                        pltpu.VMEM((Lc, SSD_CONV_DIM), F32),
                        pltpu.VMEM((Lc, SSD_D_INNER), F32)],
        compiler_params=_params(("arbitrary", "arbitrary")),
        name="ssd_scan",
    )(z, xbc, dt, conv_prev8, h0_t, p["conv_w"], p["conv_b"], p["dt_bias"], p["a_log"],
      p["d_skip"], p["g_ssd"])
    return y, h_t, conv8


def _bias_kernel(idx_ref, tab_ref, o_ref):
    h = pl.program_id(0)
    idx = idx_ref[...]
    acc = jnp.full(idx.shape, NEG_INF, F32)
    for b in range(N_BUCKETS):
        acc = jnp.where(idx == b, tab_ref[b, h], acc)
    o_ref[0] = acc


def _bias_from_table(idx, table):
    Q, K = idx.shape
    return pl.pallas_call(
        _bias_kernel,
        grid=(SWA_N_HEADS,),
        in_specs=[pl.BlockSpec((Q, K), lambda h: (0, 0)),
                  pl.BlockSpec(memory_space=pltpu.SMEM)],
        out_specs=pl.BlockSpec((1, Q, K), lambda h: (h, 0, 0)),
        out_shape=jax.ShapeDtypeStruct((SWA_N_HEADS, Q, K), F32),
        compiler_params=_params(("arbitrary",)),
        name="rel_bias",
    )(jnp.asarray(idx, jnp.int32), table.astype(F32))


def _t5_bucket_np(rel):
    nb = N_BUCKETS // 2
    max_exact = nb // 2
    ret = np.where(rel > 0, nb, 0)
    n = np.abs(rel)
    nf = np.maximum(n, 1).astype(np.float32)
    large = max_exact + (np.log(nf / np.float32(max_exact)) / np.float32(math.log(MAX_DISTANCE / max_exact))
                         * np.float32(nb - max_exact)).astype(np.int32)
    large = np.minimum(large, nb - 1)
    return (ret + np.where(n < max_exact, n, large)).astype(np.int32)


def _bucket_map(q_pos, k_pos):
    qc, kc = q_pos // CHUNK, k_pos // CHUNK
    valid = (kc[None, :] >= qc[:, None] - WINDOW_CHUNKS) & (kc[None, :] <= qc[:, None])
    return np.where(valid, _t5_bucket_np(k_pos[None, :] - q_pos[:, None]), -1).astype(np.int32)


def _swa_kernel(q_ref, kp_ref, vp_ref, kc_ref, vc_ref, bias_ref, sink_ref, o_ref, s_scr, p_scr, t_scr,
                *, mask_first):
    TQ = q_ref.shape[0]
    NP = kp_ref.shape[0]
    NK = NP + kc_ref.shape[0]
    HD = SWA_HEAD_DIM
    PW = 2 * HD
    pairs = SWA_REP // 2
    ndims = (((1,), (1,)), ((), ()))
    first_o = lax.broadcasted_iota(jnp.int32, (TQ, PW), 1) < HD
    first_s = lax.broadcasted_iota(jnp.int32, (TQ, 2 * NK), 1) < NK
    zero = jnp.zeros((NK, HD), BF16)
    one = jnp.ones((NK, HD), BF16)
    for g in range(SWA_N_KV):
        ks = slice(g * HD, (g + 1) * HD)
        k_g = jnp.concatenate([kp_ref[:, ks], kc_ref[:, ks]], axis=0).astype(BF16)
        v_g = jnp.concatenate([vp_ref[:, ks], vc_ref[:, ks]], axis=0).astype(BF16)
        kk = jnp.concatenate([jnp.concatenate([k_g, zero], axis=1),
                              jnp.concatenate([zero, k_g], axis=1)], axis=0)
        vv = jnp.concatenate([jnp.concatenate([v_g, zero, one, zero], axis=1),
                              jnp.concatenate([zero, v_g, zero, one], axis=1)], axis=0)
        for pr in range(pairs):
            pidx = g * pairs + pr
            s_scr[pr * TQ:(pr + 1) * TQ, :] = (
                lax.dot_general(q_ref[:, pidx * PW:(pidx + 1) * PW], kk, ndims, preferred_element_type=F32)
                + bias_ref[pidx])
        if mask_first:
            @pl.when(pl.program_id(1) == 0)
            def _mask_prev():
                col = lax.broadcasted_iota(jnp.int32, s_scr.shape, 1)
                s_scr[...] = jnp.where((col & (NK - 1)) < NP, NEG_INF, s_scr[...])
        for pr in range(pairs):
            pidx = g * pairs + pr
            rows = slice(pr * TQ, (pr + 1) * TQ)
            s = s_scr[rows, :]
            sink_a = sink_ref[2 * pidx]
            sink_b = sink_ref[2 * pidx + 1]
            ma = jnp.maximum(jnp.max(s[:, :NK], axis=-1, keepdims=True), sink_a)
            mb = jnp.maximum(jnp.max(s[:, NK:], axis=-1, keepdims=True), sink_b)
            p_scr[rows, :] = jnp.exp(s - jnp.where(first_s, ma, mb)).astype(BF16)
            t_scr[rows, :] = jnp.where(first_o, jnp.exp(sink_a - ma), jnp.exp(sink_b - mb))
        for pr in range(pairs):
            pidx = g * pairs + pr
            rows = slice(pr * TQ, (pr + 1) * TQ)
            ov = jnp.dot(p_scr[rows, :], vv, preferred_element_type=F32)
            o = ov[:, :PW] / (ov[:, PW:] + t_scr[rows, :])
            o_ref[:, pidx * PW:(pidx + 1) * PW] = o.astype(o_ref.dtype)


def _pair_bias(bias_prev, bias_cur):
    full = jnp.concatenate([bias_prev, bias_cur], axis=-1)
    H, Q, NK = full.shape
    return jnp.transpose(full.reshape(H // 2, 2, Q, NK), (0, 2, 1, 3)).reshape(H // 2, Q, 2 * NK)


def _swa(q, k_prev_arr, v_prev_arr, prev_map, k_cur_arr, v_cur_arr, cur_map, n_cur, bias, sinks,
         B, L, TQ, mask_first):
    T = B * L
    nblk = L // TQ
    kvw = SWA_N_KV * SWA_HEAD_DIM
    n_keys = bias.shape[2] // 2
    n_prev = n_keys - n_cur
    assert n_keys & (n_keys - 1) == 0
    pairs = SWA_REP // 2
    return pl.pallas_call(
        functools.partial(_swa_kernel, mask_first=mask_first),
        grid=(B, nblk),
        in_specs=[pl.BlockSpec((TQ, SWA_N_HEADS * SWA_HEAD_DIM), lambda b, i: (b * nblk + i, 0)),
                  pl.BlockSpec((n_prev, kvw), prev_map[0]),
                  pl.BlockSpec((n_prev, kvw), prev_map[1]),
                  pl.BlockSpec((n_cur, kvw), cur_map[0]),
                  pl.BlockSpec((n_cur, kvw), cur_map[1]),
                  pl.BlockSpec(bias.shape, lambda b, i: (0, 0, 0)),
                  pl.BlockSpec(memory_space=pltpu.SMEM)],
        out_specs=pl.BlockSpec((TQ, SWA_N_HEADS * SWA_HEAD_DIM), lambda b, i: (b * nblk + i, 0)),
        out_shape=jax.ShapeDtypeStruct((T, SWA_N_HEADS * SWA_HEAD_DIM), BF16),
        scratch_shapes=[pltpu.VMEM((pairs * TQ, 2 * n_keys), F32),
                        pltpu.VMEM((pairs * TQ, 2 * n_keys), BF16),
                        pltpu.VMEM((pairs * TQ, 2 * SWA_HEAD_DIM), F32)],
        compiler_params=_params(("parallel", "arbitrary")),
        name="swa_attention",
    )(q, k_prev_arr, v_prev_arr, k_cur_arr, v_cur_arr, bias, sinks.astype(F32))


def _mem_kernel(q_ref, mk_ref, mv_ref, o_ref):
    scale = MEM_HEAD_DIM ** -0.5
    ndims = (((1,), (1,)), ((), ()))
    for h in range(MEM_N_HEADS):
        hs = slice(h * MEM_HEAD_DIM, (h + 1) * MEM_HEAD_DIM)
        s = lax.dot_general(q_ref[:, hs], mk_ref[:, hs], ndims, preferred_element_type=F32) * scale
        m = jnp.max(s, axis=-1, keepdims=True)
        p = jnp.exp(s - m)
        den = jnp.sum(p, axis=-1, keepdims=True)
        o = jnp.dot(p.astype(BF16), mv_ref[:, hs], preferred_element_type=F32)
        o_ref[:, hs] = (o / den).astype(o_ref.dtype)


def _mem_attend(q, mk, mv, L, tm):
    T, W = q.shape
    M = mk.shape[0] // (T // L)
    return pl.pallas_call(
        _mem_kernel,
        grid=(T // tm,),
        in_specs=[pl.BlockSpec((tm, W), lambda i: (i, 0)),
                  pl.BlockSpec((M, W), lambda i: ((i * tm) // L, 0)),
                  pl.BlockSpec((M, W), lambda i: ((i * tm) // L, 0))],
        out_specs=pl.BlockSpec((tm, W), lambda i: (i, 0)),
        out_shape=jax.ShapeDtypeStruct((T, W), BF16),
        compiler_params=_params(("parallel",)),
        name="mem_attention",
    )(q, mk, mv)


def _merge_kernel(ys_ref, os_ref, om_ref, w1_ref, w2_ref, w3_ref, g0_ref, g1_ref, g2_ref, o_ref):
    a = jnp.dot(ys_ref[...], w1_ref[...], preferred_element_type=F32)
    b = jnp.dot(os_ref[...], w2_ref[...], preferred_element_type=F32)
    c = jnp.dot(om_ref[...], w3_ref[...], preferred_element_type=F32)
    o = (g0_ref[...].astype(F32) * a + g1_ref[...].astype(F32) * b + g2_ref[...].astype(F32) * c)
    o_ref[...] = o.astype(o_ref.dtype)


def _merge(y_ssd, o_s, o_m, w1, w2, w3, gates, tm, tn):
    T = y_ssd.shape[0]
    D = w1.shape[1]
    nj = D // tn
    row = lambda i, j: (i, 0)
    colw = lambda i, j: (0, j)
    return pl.pallas_call(
        _merge_kernel,
        grid=(T // tm, nj),
        in_specs=[pl.BlockSpec((tm, y_ssd.shape[1]), row),
                  pl.BlockSpec((tm, o_s.shape[1]), row),
                  pl.BlockSpec((tm, o_m.shape[1]), row),
                  pl.BlockSpec((w1.shape[0], tn), colw),
                  pl.BlockSpec((w2.shape[0], tn), colw),
                  pl.BlockSpec((w3.shape[0], tn), colw),
                  pl.BlockSpec((tm, tn), lambda i, j: (i, j)),
                  pl.BlockSpec((tm, tn), lambda i, j: (i, j + nj)),
                  pl.BlockSpec((tm, tn), lambda i, j: (i, j + 2 * nj))],
        out_specs=pl.BlockSpec((tm, tn), lambda i, j: (i, j)),
        out_shape=jax.ShapeDtypeStruct((T, D), BF16),
        compiler_params=_params(("parallel", "arbitrary")),
        name="gated_merge",
    )(y_ssd, o_s, o_m, w1, w2, w3, gates, gates, gates)


def _norm_route_kernel(xa_ref, xb_ref, g_ref, whi_ref, wlo_ref, h_ref, r_ref, *, n_a):
    x = jnp.where(pl.program_id(0) < n_a, xa_ref[...], xb_ref[...])
    ms = jnp.mean(x * x, axis=-1, keepdims=True)
    h = x * lax.rsqrt(ms + EPS) * g_ref[...]
    h_ref[:, 0, :] = h
    hb = h.astype(BF16)
    lo = (h - hb.astype(F32)).astype(BF16)
    lg = (jnp.dot(hb, whi_ref[...], preferred_element_type=F32)
          + jnp.dot(lo, whi_ref[...], preferred_element_type=F32)
          + jnp.dot(hb, wlo_ref[...], preferred_element_type=F32))
    lane = lax.broadcasted_iota(jnp.int32, lg.shape, 1)
    lane_f = lane.astype(F32)
    far = float(LANES)
    gl = jnp.where((lane >= N_EXPERTS) & (lane < N_EXPERTS + N_EXPERT_GROUPS), lg, NEG_INF)
    gmax = jnp.max(gl, axis=-1, keepdims=True)
    gidx = jnp.min(jnp.where(gl == gmax, lane_f - N_EXPERTS, far), axis=-1, keepdims=True)
    gw = 1.0 / jnp.sum(jnp.exp(gl - gmax), axis=-1, keepdims=True)
    lo_e = gidx * EXPERTS_PER_GROUP
    el = jnp.where((lane_f >= lo_e) & (lane_f < lo_e + EXPERTS_PER_GROUP), lg, NEG_INF)
    v1 = jnp.max(el, axis=-1, keepdims=True)
    i1 = jnp.min(jnp.where(el == v1, lane_f, far), axis=-1, keepdims=True)
    el2 = jnp.where(lane_f == i1, NEG_INF, el)
    v2 = jnp.max(el2, axis=-1, keepdims=True)
    i2 = jnp.min(jnp.where(el2 == v2, lane_f, far), axis=-1, keepdims=True)
    e = jnp.exp(v2 - v1)
    w1 = gw / (1.0 + e)
    w2 = gw * e / (1.0 + e)
    r_ref[...] = jnp.where(lane == 0, w1, jnp.where(lane == 1, w2,
                           jnp.where(lane == 2, i1, jnp.where(lane == 3, i2, 0.0))))


def _norm_route(xa, xb, g, w_hi, w_lo, tm):
    Ta, D = xa.shape
    Tb = xb.shape[0]
    T = Ta + Tb
    n_a, n_b = Ta // tm, Tb // tm
    assert n_a * tm == Ta and n_b * tm == Tb
    return pl.pallas_call(
        functools.partial(_norm_route_kernel, n_a=n_a),
        grid=(n_a + n_b,),
        in_specs=[pl.BlockSpec((tm, D), lambda i: (jnp.minimum(i, n_a - 1), 0)),
                  pl.BlockSpec((tm, D), lambda i: (jnp.maximum(i - n_a, 0), 0)),
                  pl.BlockSpec((1, D), lambda i: (0, 0)),
                  pl.BlockSpec((D, LANES), lambda i: (0, 0)),
                  pl.BlockSpec((D, LANES), lambda i: (0, 0))],
        out_specs=[pl.BlockSpec((tm, 1, D), lambda i: (i, 0, 0)),
                   pl.BlockSpec((tm, LANES), lambda i: (i, 0))],
        out_shape=[jax.ShapeDtypeStruct((T, 1, D), F32),
                   jax.ShapeDtypeStruct((T, LANES), F32)],
        compiler_params=_params(("parallel",)),
        name="ffn_norm_router",
    )(xa, xb, g.reshape(1, D).astype(F32), w_hi, w_lo)


def _moe_kernel(be_ref, nv_ref, s0_ref, ord_ref, h_hbm, w1_ref, w3_ref, w2_ref, yu_hbm,
                xbuf, ybuf, w1b, w3b, w2b, sem_in, sem_out, *, n_tok):
    i = pl.program_id(0)
    nb = pl.num_programs(0)
    rows = xbuf.shape[1]
    n_assign = TOP_K * n_tok
    nv = nv_ref[i]
    slot = i % 2
    nxt = jnp.minimum(i + 1, nb - 1)
    next_valid = (i + 1 < nb) & (nv_ref[nxt] > 0)

    def assignment(blk, r):
        return ord_ref[s0_ref[blk] + r]

    def row_in(s, r, tok):
        return pltpu.make_async_copy(h_hbm.at[tok], xbuf.at[s, pl.ds(r, 1), :], sem_in.at[s])

    def row_out(s, r, dst):
        return pltpu.make_async_copy(ybuf.at[s, pl.ds(r, 1), :], yu_hbm.at[dst], sem_out.at[s])

    def gather_start(blk, s):
        for r in range(rows):
            row_in(s, r, lax.shift_right_logical(assignment(blk, r), 1)).start()

    def gather_wait(s):
        for r in range(rows):
            row_in(s, r, 0).wait()

    def scatter_start(blk, s):
        n_valid = nv_ref[blk]
        for r in range(rows):
            m = assignment(blk, r)
            dst = jnp.where(r < n_valid, (m & 1) * n_tok + lax.shift_right_logical(m, 1),
                            n_assign + s * rows + r)
            row_out(s, r, dst).start()

    def scatter_wait(s):
        for r in range(rows):
            row_out(s, r, 0).wait()

    @pl.when(i == 0)
    def _clear_spare_rows():
        ybuf[...] = jnp.zeros(ybuf.shape, ybuf.dtype)
        for s in range(2):
            for r in range(rows):
                row_out(s, r, n_assign + s * rows + r).start()
            scatter_wait(s)

    @pl.when((i == 0) & (nv > 0))
    def _prologue():
        gather_start(i, 0)

    @pl.when(next_valid)
    def _prefetch():
        gather_start(nxt, 1 - slot)

    prev = be_ref[jnp.maximum(i - 1, 0)]

    @pl.when((i == 0) | (be_ref[i] != prev))
    def _load_expert():
        w1b[...] = w1_ref[0].astype(BF16)
        w3b[...] = w3_ref[0].astype(BF16)
        w2b[...] = w2_ref[0].astype(BF16)

    @pl.when(nv > 0)
    def _compute():
        gather_wait(slot)

        @pl.when(i >= 2)
        def _free_ybuf():
            scatter_wait(slot)

        x = xbuf[slot].astype(BF16)
        a = jnp.dot(x, w1b[...], preferred_element_type=F32)
        b = jnp.dot(x, w3b[...], preferred_element_type=F32)
        mid = (_silu(a) * b).astype(BF16)
        y = jnp.dot(mid, w2b[...], preferred_element_type=F32)
        ybuf[slot] = y
        scatter_start(i, slot)

        @pl.when(jnp.logical_not(next_valid))
        def _drain():
            scatter_wait(slot)

            @pl.when(i >= 1)
            def _drain_prev():
                scatter_wait(1 - slot)


def _moe(h2, route, w_gate, w_up, w_down):
    T, _, D = h2.shape
    F = w_gate.shape[2]
    assert TOP_K == 2
    M = T * TOP_K
    BM = MOE_ROWS
    nb = (M + N_EXPERTS * (BM - 1) + BM - 1) // BM
    e_flat = route[:, TOP_K:2 * TOP_K].astype(jnp.int32).reshape(M)
    idx_bits = (M - 1).bit_length()
    assert (N_EXPERTS << idx_bits) < 2 ** 31
    packed = lax.sort(e_flat * (1 << idx_bits) + jnp.arange(M, dtype=jnp.int32))
    order = jnp.pad(packed & ((1 << idx_bits) - 1), (0, BM))
    experts = jnp.arange(N_EXPERTS, dtype=jnp.int32)
    counts = jnp.sum((experts[:, None] == e_flat[None, :]).astype(jnp.int32), axis=1)
    padded = (counts + BM - 1) // BM * BM
    pad_end = jnp.cumsum(padded)
    pad_start = pad_end - padded
    start = jnp.cumsum(counts) - counts
    blk0 = jnp.arange(nb, dtype=jnp.int32) * BM
    blk_exp = jnp.minimum(jnp.sum((pad_end[None, :] <= blk0[:, None]).astype(jnp.int32), axis=1), N_EXPERTS - 1)
    pick = (blk_exp[:, None] == experts[None, :]).astype(jnp.int32)
    off0 = blk0 - jnp.sum(pick * pad_start[None, :], axis=1)
    blk_nv = jnp.clip(jnp.sum(pick * counts[None, :], axis=1) - off0, 0, BM).astype(jnp.int32)
    blk_s0 = jnp.clip(jnp.sum(pick * start[None, :], axis=1) + off0, 0, M - 1).astype(jnp.int32)

    grid_spec = pltpu.PrefetchScalarGridSpec(
        num_scalar_prefetch=4,
        grid=(nb,),
        in_specs=[pl.BlockSpec(memory_space=pl.ANY),
                  pl.BlockSpec((1, D, F), lambda i, be, nv, s0, od: (be[i], 0, 0)),
                  pl.BlockSpec((1, D, F), lambda i, be, nv, s0, od: (be[i], 0, 0)),
                  pl.BlockSpec((1, F, D), lambda i, be, nv, s0, od: (be[i], 0, 0))],
        out_specs=pl.BlockSpec(memory_space=pl.ANY),
        scratch_shapes=[pltpu.VMEM((2, BM, D), F32),
                        pltpu.VMEM((2, BM, D), F32),
                        pltpu.VMEM((D, F), BF16),
                        pltpu.VMEM((D, F), BF16),
                        pltpu.VMEM((F, D), BF16),
                        pltpu.SemaphoreType.DMA((2,)),
                        pltpu.SemaphoreType.DMA((2,))],
    )
    return pl.pallas_call(
        functools.partial(_moe_kernel, n_tok=T),
        grid_spec=grid_spec,
        out_shape=jax.ShapeDtypeStruct((M + 2 * BM, 1, D), F32),
        compiler_params=_params(("arbitrary",)),
        name="moe_experts",
    )(blk_exp.astype(jnp.int32), blk_nv, blk_s0, order, h2, w_gate, w_up, w_down)


def _combine_kernel(x_ref, r_ref, y0_ref, y1_ref, o_ref):
    w0 = r_ref[:, 0:1]
    w1 = r_ref[:, 1:2]
    o_ref[...] = x_ref[...] + (w0 * y0_ref[:, 0, :] + w1 * y1_ref[:, 0, :])


def _combine(x, route, yu, blk0, tm):
    T, D = x.shape
    n_all = route.shape[0] // tm
    return pl.pallas_call(
        _combine_kernel,
        grid=(T // tm,),
        in_specs=[pl.BlockSpec((tm, D), lambda i: (i, 0)),
                  pl.BlockSpec((tm, LANES), lambda i: (blk0 + i, 0)),
                  pl.BlockSpec((tm, 1, D), lambda i: (blk0 + i, 0, 0)),
                  pl.BlockSpec((tm, 1, D), lambda i: (n_all + blk0 + i, 0, 0))],
        out_specs=pl.BlockSpec((tm, D), lambda i: (i, 0)),
        out_shape=jax.ShapeDtypeStruct((T, D), F32),
        compiler_params=_params(("parallel",)),
        name="moe_combine",
    )(x, route, yu, yu)


def _layer(x, B, L, p, conv_prev8, h0_t, mk, mv, swa_prev, swa_bias, Lc, TQ):
    T, D = x.shape
    tmm = min(MM_ROWS, T)
    z, h = _norm_matmul(x, p["g_mix"], p["wz"], BF16, tmm, MM_COLS, name="norm_proj_z")
    xbc = _matmul(h, p["wxbc"], BF16, tmm, MM_COLS, name="proj_xbc")
    dt = _matmul(h, p["wdt"], F32, tmm, PAD_HEADS, name="proj_dt")
    q_s = _matmul(h, p["wqs"], BF16, tmm, 512, epi=_epi_group_norm, col_extras=(p["g_q_swa"],),
                  const_extras=(p["mavg"],), name="proj_q_swa")
    kv = _matmul(h, p["wkv"], F32, tmm, 256, epi=_epi_kv, col_extras=(p["g_kv"],),
                 const_extras=(p["mavg"],), name="proj_kv_swa")
    q_m = _matmul(h, p["wqm"], BF16, tmm, MEM_HEAD_DIM, epi=_epi_row_norm, col_extras=(p["g_q_mem"],),
                  name="proj_q_mem")
    gates = _matmul(h, p["wg"], BF16, tmm, MM_COLS, epi=_epi_sigmoid, name="proj_gates")

    y_ssd, h_t, conv8 = _ssd(z, xbc, dt, conv_prev8, h0_t, p, B, L, Lc)

    nblk = L // TQ
    kvw = SWA_N_KV * SWA_HEAD_DIM
    if swa_prev is None:
        prev_map = (lambda b, i: (b * nblk + jnp.maximum(i - 1, 0), 0),
                    lambda b, i: (b * nblk + jnp.maximum(i - 1, 0), 1))
        cur_map = (lambda b, i: (b * nblk + i, 0), lambda b, i: (b * nblk + i, 1))
        o_s = _swa(q_s, kv, kv, prev_map, kv, kv, cur_map, TQ, swa_bias, p["sinks"], B, L, TQ, True)
    else:
        n_cur = swa_bias.shape[2] // 2 - swa_prev[0].shape[0] // B
        kv3 = jnp.pad(kv.reshape(B, L, 2 * kvw), ((0, 0), (0, n_cur - L), (0, 0)))
        k_new = kv3[:, :, :kvw].reshape(B * n_cur, kvw)
        v_new = kv3[:, :, kvw:].reshape(B * n_cur, kvw)
        per_b = (lambda b, i: (b, 0), lambda b, i: (b, 0))
        o_s = _swa(q_s, swa_prev[0], swa_prev[1], per_b, k_new, v_new, per_b, n_cur, swa_bias, p["sinks"],
                   B, L, TQ, False)

    o_m = _mem_attend(q_m, mk, mv, L, min(256, L))

    merged = _merge(y_ssd, o_s, o_m, p["w_o_ssd"], p["w_o_swa"], p["w_o_mem"], gates, tmm, MERGE_COLS)
    x1 = _matmul(merged, p["w_out"], F32, tmm, MM_COLS, epi=_epi_residual, tile_extras=(x,), name="proj_out")

    return x1, conv8, h_t, kv


def _state_to_heads(h_t, B):
    return jnp.transpose(h_t.reshape(B, SSD_D_STATE, SSD_N_HEADS, SSD_HEAD_DIM), (0, 2, 3, 1))


def kernel(x_prompt, x_sample, cache_conv, state_ssd, cache_swa_k, cache_swa_v, cache_mem_k, cache_mem_v, mem_prompt, rel_bias_table, g_mix, w_in, conv_w, conv_b, dt_bias, a_log, d_skip, g_ssd, w_o_ssd, g_q_swa, g_k_swa, sinks, w_o_swa, g_mem, w_mem_k, w_mem_v, g_q_mem, g_k_mem, w_o_mem, w_out, g_ffn, w_router_grp, w_router_exp, w_exp_gate, w_exp_up, w_exp_down):
    B, S, D = x_prompt.shape
    Bd, Sd, _ = x_sample.shape
    depth = w_in.shape[0]
    assert depth == 1
    l = 0
    kvw = SWA_N_KV * SWA_HEAD_DIM
    qw = SWA_N_HEADS * SWA_HEAD_DIM
    mw = MEM_N_HEADS * MEM_HEAD_DIM

    sizes = (SSD_D_INNER, SSD_CONV_DIM, SSD_N_HEADS, qw, kvw, kvw, mw, 3 * D)
    offs = np.concatenate([[0], np.cumsum(sizes)])
    w = w_in[l]
    cols = [w[:, int(offs[k]):int(offs[k + 1])] for k in range(len(sizes))]
    pad_h = PAD_HEADS - SSD_N_HEADS
    w_r = jnp.pad(jnp.concatenate([w_router_exp[l], w_router_grp[l]], axis=1),
                  ((0, 0), (0, LANES - N_EXPERTS - N_EXPERT_GROUPS)))
    w_r_hi = w_r.astype(BF16)
    p = {
        "g_mix": g_mix[l],
        "wz": cols[0].astype(BF16),
        "wxbc": cols[1].astype(BF16),
        "wdt": jnp.pad(cols[2], ((0, 0), (0, pad_h))).astype(BF16),
        "wqs": cols[3].astype(BF16),
        "wkv": jnp.concatenate([cols[4], cols[5]], axis=1).astype(BF16),
        "wqm": cols[6].astype(BF16),
        "wg": cols[7].astype(BF16),
        "conv_w": conv_w[l].astype(F32),
        "conv_b": conv_b[l].reshape(1, SSD_CONV_DIM).astype(F32),
        "dt_bias": jnp.pad(dt_bias[l], (0, pad_h)).reshape(1, PAD_HEADS).astype(F32),
        "a_log": jnp.pad(a_log[l], (0, pad_h)).reshape(1, PAD_HEADS).astype(F32),
        "d_skip": jnp.repeat(d_skip[l], SSD_HEAD_DIM).reshape(1, SSD_D_INNER).astype(F32),
        "g_ssd": g_ssd[l].reshape(1, SSD_D_INNER).astype(F32),
        "g_q_swa": (jnp.tile(g_q_swa[l], SWA_N_HEADS) * SWA_HEAD_DIM ** -0.5).reshape(1, qw).astype(F32),
        "g_kv": jnp.concatenate([jnp.tile(g_k_swa[l], SWA_N_KV), jnp.ones((kvw,), F32)]).reshape(1, 2 * kvw),
        "g_q_mem": jnp.tile(g_q_mem[l], MEM_N_HEADS).reshape(1, mw).astype(F32),
        "mavg": _group_mean_matrix(256, SWA_HEAD_DIM),
        "sinks": sinks[l],
        "w_o_ssd": w_o_ssd[l].astype(BF16),
        "w_o_swa": w_o_swa[l].astype(BF16),
        "w_o_mem": w_o_mem[l].astype(BF16),
        "w_out": w_out[l].astype(BF16),
        "g_ffn": g_ffn[l],
        "w_r_hi": w_r_hi,
        "w_r_lo": (w_r - w_r_hi.astype(F32)).astype(BF16),
        "w_exp_gate": w_exp_gate[l],
        "w_exp_up": w_exp_up[l],
        "w_exp_down": w_exp_down[l],
    }

    M = mem_prompt.shape[1]
    mn = _rmsnorm(mem_prompt.reshape(B * M, D), g_mem[l], min(256, B * M))
    mk_p = _matmul(mn, w_mem_k[l].astype(BF16), F32, min(256, B * M), MEM_HEAD_DIM, epi=_epi_row_norm,
                   col_extras=(jnp.tile(g_k_mem[l], MEM_N_HEADS).reshape(1, mw).astype(F32),), name="mem_k")
    mv_p = _matmul(mn, w_mem_v[l].astype(BF16), F32, min(256, B * M), MEM_HEAD_DIM, name="mem_v")

    TQ = 2 * CHUNK
    qpos = np.arange(TQ)
    bias_p = _pair_bias(_bias_from_table(_bucket_map(qpos, np.arange(TQ) - TQ), rel_bias_table),
                        _bias_from_table(_bucket_map(qpos, np.arange(TQ)), rel_bias_table))
    C = cache_swa_k.shape[2]
    qpos_s = PAST_LEN + np.arange(Sd)
    cur_map_s = _bucket_map(qpos_s, PAST_LEN + np.arange(C))
    cur_map_s[:, Sd:] = -1
    bias_s = _pair_bias(_bias_from_table(_bucket_map(qpos_s, PAST_LEN - C + np.arange(C)), rel_bias_table),
                        _bias_from_table(cur_map_s, rel_bias_table))

    conv0 = jnp.zeros((B, SUBLANES, SSD_CONV_DIM), F32)
    h0 = jnp.zeros((B, SSD_D_STATE, SSD_D_INNER), F32)
    x1_p, conv8_p, ht_p, kv_p = _layer(x_prompt.reshape(B * S, D), B, S, p, conv0, h0,
                                     mk_p.astype(BF16), mv_p.astype(BF16), None,
                                     bias_p, CHUNK, TQ)
    conv_prev = jnp.pad(cache_conv[l], ((0, 0), (SUBLANES - (SSD_CONV - 1), 0), (0, 0)))
    h0_s = jnp.transpose(state_ssd[l], (0, 3, 1, 2)).reshape(Bd, SSD_D_STATE, SSD_D_INNER)
    x1_s, conv8_s, ht_s, kv_s = _layer(x_sample.reshape(Bd * Sd, D), Bd, Sd, p, conv_prev, h0_s,
                                     cache_mem_k[l].reshape(Bd * M, mw).astype(BF16),
                                     cache_mem_v[l].reshape(Bd * M, mw).astype(BF16),
                                     (cache_swa_k[l].reshape(Bd * C, kvw), cache_swa_v[l].reshape(Bd * C, kvw)),
                                     bias_s, Sd, Sd)

    tr = math.gcd(B * S, Bd * Sd, MOE_ROWS)
    h2, route = _norm_route(x1_p, x1_s, p["g_ffn"], p["w_r_hi"], p["w_r_lo"], tr)
    yu = _moe(h2, route, p["w_exp_gate"], p["w_exp_up"], p["w_exp_down"])
    yp = _combine(x1_p, route, yu, 0, tr)
    ys = _combine(x1_s, route, yu, (B * S) // tr, tr)

    keep = min(WINDOW, S)
    kv_p = kv_p.reshape(B, S, 2, SWA_N_KV, SWA_HEAD_DIM)[:, S - keep:]
    kv_s = kv_s.reshape(Bd, Sd, 2, SWA_N_KV, SWA_HEAD_DIM)
    tail = SUBLANES - (SSD_CONV - 1)
    return (yp.reshape(B, S, D), ys.reshape(Bd, Sd, D),
            conv8_p[None, :, tail:], _state_to_heads(ht_p, B)[None],
            kv_p[None, :, :, 0], kv_p[None, :, :, 1],
            mk_p.reshape(1, B, M, MEM_N_HEADS, MEM_HEAD_DIM), mv_p.reshape(1, B, M, MEM_N_HEADS, MEM_HEAD_DIM),
            conv8_s[None, :, tail:], _state_to_heads(ht_s, Bd)[None],
            kv_s[None, :, :, 0], kv_s[None, :, :, 1])
```

```python
import functools
import math

import numpy as np
import jax
import jax.numpy as jnp
from jax import lax
from jax.experimental import pallas as pl
from jax.experimental.pallas import tpu as pltpu

F32 = jnp.float32
BF16 = jnp.bfloat16
EPS = 1e-6
NEG_INF = float("-inf")

CHUNK = 64
SSD_HEAD_DIM = 64
SSD_N_HEADS = 64
SSD_N_GROUPS = 8
SSD_D_STATE = 128
SSD_D_INNER = SSD_N_HEADS * SSD_HEAD_DIM
SSD_GN = SSD_N_GROUPS * SSD_D_STATE
SSD_CONV_DIM = SSD_D_INNER + 2 * SSD_GN
SSD_CONV = 4
SWA_N_HEADS = 32
SWA_N_KV = 4
SWA_HEAD_DIM = 64
SWA_REP = SWA_N_HEADS // SWA_N_KV
WINDOW = 128
WINDOW_CHUNKS = WINDOW // CHUNK
MEM_N_HEADS = 4
MEM_HEAD_DIM = 512
N_BUCKETS = 32
MAX_DISTANCE = 128
N_EXPERT_GROUPS = 4
EXPERTS_PER_GROUP = 16
N_EXPERTS = N_EXPERT_GROUPS * EXPERTS_PER_GROUP
TOP_K = 2
PAST_LEN = 4096

LANES = 128
SUBLANES = 8
VMEM_LIMIT = 56 * 1024 * 1024
MOE_ROWS = 256
MM_ROWS = 1024
MM_COLS = 1024
MM_COLS_WIDE = 1536
MERGE_COLS = 256
PAD_HEADS = LANES


def _params(sem, vmem=VMEM_LIMIT):
    return pltpu.CompilerParams(dimension_semantics=sem, vmem_limit_bytes=vmem)


def _sigmoid(x):
    return 0.5 * (jnp.tanh(0.5 * x) + 1.0)


def _silu(x):
    u = 0.5 * x
    return u * (jnp.tanh(u) + 1.0)


def _split3(x):
    x1 = x.astype(BF16)
    r1 = x - x1.astype(F32)
    x2 = r1.astype(BF16)
    x3 = (r1 - x2.astype(F32)).astype(BF16)
    return x1, x2, x3


def _rmsnorm_kernel(x_ref, g_ref, o_ref):
    x = x_ref[...]
    ms = jnp.mean(x * x, axis=-1, keepdims=True)
    o_ref[...] = (x * lax.rsqrt(ms + EPS) * g_ref[...]).astype(o_ref.dtype)


def _rmsnorm(x, g, tm):
    T, D = x.shape
    return pl.pallas_call(
        _rmsnorm_kernel,
        grid=(T // tm,),
        in_specs=[pl.BlockSpec((tm, D), lambda i: (i, 0)),
                  pl.BlockSpec((1, D), lambda i: (0, 0))],
        out_specs=pl.BlockSpec((tm, D), lambda i: (i, 0)),
        out_shape=jax.ShapeDtypeStruct((T, D), BF16),
        compiler_params=_params(("parallel",)),
        name="rmsnorm",
    )(x, g.reshape(1, D).astype(F32))


def _mm_kernel(a_ref, b_ref, *refs, epi):
    o_ref = refs[-1]
    acc = jnp.dot(a_ref[...], b_ref[...], preferred_element_type=F32)
    if epi is not None:
        acc = epi(acc, *[r[...] for r in refs[:-1]])
    o_ref[...] = acc.astype(o_ref.dtype)


def _matmul(a, b, out_dtype, tm, tn, epi=None, col_extras=(), tile_extras=(), const_extras=(), name="matmul"):
    M, K = a.shape
    N = b.shape[1]
    assert M % tm == 0 and N % tn == 0, (M, N, tm, tn)
    in_specs = [pl.BlockSpec((tm, K), lambda i, j: (i, 0)),
                pl.BlockSpec((K, tn), lambda i, j: (0, j))]
    for _ in col_extras:
        in_specs.append(pl.BlockSpec((1, tn), lambda i, j: (0, j)))
    for _ in tile_extras:
        in_specs.append(pl.BlockSpec((tm, tn), lambda i, j: (i, j)))
    for c in const_extras:
        in_specs.append(pl.BlockSpec(c.shape, lambda i, j: (0, 0)))
    return pl.pallas_call(
        functools.partial(_mm_kernel, epi=epi),
        grid=(M // tm, N // tn),
        in_specs=in_specs,
        out_specs=pl.BlockSpec((tm, tn), lambda i, j: (i, j)),
        out_shape=jax.ShapeDtypeStruct((M, N), out_dtype),
        compiler_params=_params(("parallel", "arbitrary")),
        name=name,
    )(a, b, *col_extras, *tile_extras, *const_extras)


def _norm_mm_kernel(x_ref, g_ref, b_ref, o_ref, h_ref):
    @pl.when(pl.program_id(1) == 0)
    def _normalise():
        x = x_ref[...]
        ms = jnp.mean(x * x, axis=-1, keepdims=True)
        h_ref[...] = (x * lax.rsqrt(ms + EPS) * g_ref[...]).astype(h_ref.dtype)

    o_ref[...] = jnp.dot(h_ref[...], b_ref[...], preferred_element_type=F32).astype(o_ref.dtype)


def _norm_matmul(x, g, b, out_dtype, tm, tn, name):
    M, K = x.shape
    N = b.shape[1]
    assert M % tm == 0 and N % tn == 0, (M, N, tm, tn)
    return pl.pallas_call(
        _norm_mm_kernel,
        grid=(M // tm, N // tn),
        in_specs=[pl.BlockSpec((tm, K), lambda i, j: (i, 0)),
                  pl.BlockSpec((1, K), lambda i, j: (0, 0)),
                  pl.BlockSpec((K, tn), lambda i, j: (0, j))],
        out_specs=[pl.BlockSpec((tm, tn), lambda i, j: (i, j)),
                   pl.BlockSpec((tm, K), lambda i, j: (i, 0))],
        out_shape=[jax.ShapeDtypeStruct((M, N), out_dtype),
                   jax.ShapeDtypeStruct((M, K), BF16)],
        compiler_params=_params(("parallel", "arbitrary")),
        name=name,
    )(x, g.reshape(1, K).astype(F32), b)


def _group_mean_matrix(width, group):
    idx = np.arange(width) // group
    return jnp.asarray((idx[:, None] == idx[None, :]).astype(np.float32) / group, dtype=BF16)


def _epi_group_norm(acc, gain, mavg):
    w = mavg.shape[0]
    outs = []
    for c in range(acc.shape[1] // w):
        a = acc[:, c * w:(c + 1) * w]
        s = a * a
        hi = s.astype(BF16)
        lo = (s - hi.astype(F32)).astype(BF16)
        ms = (jnp.dot(hi, mavg, preferred_element_type=F32)
              + jnp.dot(lo, mavg, preferred_element_type=F32))
        outs.append(a * lax.rsqrt(ms + EPS))
    normed = outs[0] if len(outs) == 1 else jnp.concatenate(outs, axis=1)
    return normed * gain


def _epi_kv(acc, gain, mavg):
    normed = _epi_group_norm(acc, gain, mavg)
    return jnp.where(pl.program_id(1) == 0, normed, acc)


def _epi_row_norm(acc, gain):
    outs = []
    for c in range(acc.shape[1] // MEM_HEAD_DIM):
        a = acc[:, c * MEM_HEAD_DIM:(c + 1) * MEM_HEAD_DIM]
        outs.append(a * lax.rsqrt(jnp.mean(a * a, axis=-1, keepdims=True) + EPS))
    normed = outs[0] if len(outs) == 1 else jnp.concatenate(outs, axis=1)
    return normed * gain


def _epi_sigmoid(acc):
    return _sigmoid(acc)


def _epi_residual(acc, res):
    return acc + res


def _ssd_kernel(z_ref, xbc_ref, dt_ref, cprev_ref, h0_ref, cw_ref, cb_ref, dtb_ref, alog_ref,
                dskip_ref, gn_ref, y_ref, hout_ref, cout_ref, xp_s, h_s, conv_s, y_s, *, Lc):
    c = pl.program_id(1)
    n_chunks = pl.num_programs(1)
    P2 = 2 * SSD_HEAD_DIM
    L2 = 2 * Lc

    @pl.when(c == 0)
    def _init():
        xp_s[0:SUBLANES, :] = cprev_ref[0]
        h_s[...] = h0_ref[0]

    xp_s[SUBLANES:SUBLANES + Lc, :] = xbc_ref[...].astype(F32)
    cblk = 512
    row8 = lax.broadcasted_iota(jnp.int32, (SUBLANES, cblk), 0)
    for j in range(SSD_CONV_DIM // cblk):
        sl = slice(j * cblk, (j + 1) * cblk)
        cur = xp_s[SUBLANES:SUBLANES + Lc, sl]
        tail = xp_s[0:SUBLANES, sl]
        acc = cb_ref[:, sl] + cw_ref[SSD_CONV - 1:SSD_CONV, sl] * cur
        for k in range(SSD_CONV - 1):
            sh = SSD_CONV - 1 - k
            down = pltpu.roll(cur, sh, axis=0)
            top = jnp.where(row8 >= sh, down[0:SUBLANES], pltpu.roll(tail, sh, axis=0))
            acc = acc + cw_ref[k:k + 1, sl] * jnp.concatenate([top, down[SUBLANES:]], axis=0)
        conv_s[:, sl] = _silu(acc)
    xp_s[0:SUBLANES, :] = xp_s[Lc:Lc + SUBLANES, :]

    dtv = dt_ref[...] + dtb_ref[...]
    dt = jnp.maximum(dtv, 0.0) + jnp.log1p(jnp.exp(-jnp.abs(dtv)))
    adt = dt * (-jnp.exp(alog_ref[...]))
    row = lax.broadcasted_iota(jnp.int32, (Lc, Lc), 0)
    col = lax.broadcasted_iota(jnp.int32, (Lc, Lc), 1)
    tri = (col <= row).astype(BF16)
    row2 = lax.broadcasted_iota(jnp.int32, (Lc, L2), 0)
    col2 = lax.broadcasted_iota(jnp.int32, (Lc, L2), 1)
    col2m = jnp.where(col2 >= Lc, col2 - Lc, col2)
    tri_t2 = (row2 <= col2m).astype(BF16)
    causal2 = col2m <= row2
    a1, a2, a3 = _split3(adt)
    acs = (jnp.dot(tri, a1, preferred_element_type=F32)
           + jnp.dot(tri, a2, preferred_element_type=F32)
           + jnp.dot(tri, a3, preferred_element_type=F32))
    tdims = (((0,), (0,)), ((), ()))
    acs_t2 = (lax.dot_general(a1, tri_t2, tdims, preferred_element_type=F32)
              + lax.dot_general(a2, tri_t2, tdims, preferred_element_type=F32)
              + lax.dot_general(a3, tri_t2, tdims, preferred_element_type=F32))

    lane_p = lax.broadcasted_iota(jnp.int32, (Lc, P2), 1)
    first_p = lane_p < SSD_HEAD_DIM
    first_l = col2 < Lc
    first_l1 = first_l[0:1, :]
    ndims = (((1,), (1,)), ((), ()))

    for g in range(SSD_N_GROUPS):
        b_g = conv_s[:, SSD_D_INNER + g * SSD_D_STATE:SSD_D_INNER + (g + 1) * SSD_D_STATE].astype(BF16)
        c_g = conv_s[:, SSD_D_INNER + SSD_GN + g * SSD_D_STATE:
                     SSD_D_INNER + SSD_GN + (g + 1) * SSD_D_STATE].astype(BF16)
        b2 = jnp.concatenate([b_g, b_g], axis=0)
        cb2 = lax.dot_general(c_g, b2, ndims, preferred_element_type=F32)
        gw = SSD_HEAD_DIM * (SSD_N_HEADS // SSD_N_GROUPS)
        inter = jnp.dot(c_g, h_s[:, g * gw:(g + 1) * gw].astype(BF16), preferred_element_type=F32)
        for jj in range(gw // P2):
            j = g * (gw // P2) + jj
            sl = slice(j * P2, (j + 1) * P2)
            acs_a = acs[:, 2 * j:2 * j + 1]
            acs_b = acs[:, 2 * j + 1:2 * j + 2]
            col_l = jnp.where(first_l, acs_a, acs_b)
            row_l = jnp.where(first_l1, acs_t2[2 * j:2 * j + 1, :], acs_t2[2 * j + 1:2 * j + 2, :])
            dec = jnp.exp(jnp.where(causal2, col_l - row_l, NEG_INF))
            m_pair = (cb2 * dec).astype(BF16)
            col_p = col_l if L2 == P2 else jnp.where(first_p, acs_a, acs_b)
            dt_p = jnp.where(first_p, dt[:, 2 * j:2 * j + 1], dt[:, 2 * j + 1:2 * j + 2])
            xs_p = conv_s[:, sl]
            xdt = xs_p * dt_p
            rhs = jnp.concatenate([jnp.where(first_p, xdt, 0.0), jnp.where(first_p, 0.0, xdt)],
                                  axis=0).astype(BF16)
            y = jnp.dot(m_pair, rhs, preferred_element_type=F32)
            y = y + inter[:, jj * P2:(jj + 1) * P2] * jnp.exp(col_p) + dskip_ref[:, sl] * xs_p
            y_s[:, sl] = y
            a_end = col_p[Lc - 1:Lc, :]
            xw = (xdt * jnp.exp(a_end - col_p)).astype(BF16)
            h_s[:, sl] = (h_s[:, sl] * jnp.exp(a_end)
                          + lax.dot_general(b_g, xw, tdims, preferred_element_type=F32))

    gdim = SSD_D_INNER // SSD_N_GROUPS
    for g in range(SSD_N_GROUPS):
        sl = slice(g * gdim, (g + 1) * gdim)
        zz = z_ref[:, sl].astype(F32)
        yy = y_s[:, sl] * _silu(zz)
        ms = jnp.mean(yy * yy, axis=-1, keepdims=True)
        y_ref[:, sl] = (yy * lax.rsqrt(ms + EPS) * gn_ref[:, sl]).astype(y_ref.dtype)

    @pl.when(c == n_chunks - 1)
    def _fin():
        hout_ref[0] = h_s[...]
        cout_ref[0] = xp_s[0:SUBLANES, :]


def _ssd(z, xbc, dt, conv_prev8, h0_t, p, B, L, Lc):
    T = B * L
    nc = L // Lc
    tok = lambda b, c: (b * nc + c, 0)
    per_b = lambda b, c: (b, 0, 0)
    whole = lambda b, c: (0, 0)
    y, h_t, conv8 = pl.pallas_call(
        functools.partial(_ssd_kernel, Lc=Lc),
        grid=(B, nc),
        in_specs=[pl.BlockSpec((Lc, SSD_D_INNER), tok),
                  pl.BlockSpec((Lc, SSD_CONV_DIM), tok),
                  pl.BlockSpec((Lc, PAD_HEADS), tok),
                  pl.BlockSpec((1, SUBLANES, SSD_CONV_DIM), per_b),
                  pl.BlockSpec((1, SSD_D_STATE, SSD_D_INNER), per_b),
                  pl.BlockSpec((SSD_CONV, SSD_CONV_DIM), whole),
                  pl.BlockSpec((1, SSD_CONV_DIM), whole),
                  pl.BlockSpec((1, PAD_HEADS), whole),
                  pl.BlockSpec((1, PAD_HEADS), whole),
                  pl.BlockSpec((1, SSD_D_INNER), whole),
                  pl.BlockSpec((1, SSD_D_INNER), whole)],
        out_specs=[pl.BlockSpec((Lc, SSD_D_INNER), tok),
                   pl.BlockSpec((1, SSD_D_STATE, SSD_D_INNER), per_b),
                   pl.BlockSpec((1, SUBLANES, SSD_CONV_DIM), per_b)],
        out_shape=[jax.ShapeDtypeStruct((T, SSD_D_INNER), BF16),
                   jax.ShapeDtypeStruct((B, SSD_D_STATE, SSD_D_INNER), F32),
                   jax.ShapeDtypeStruct((B, SUBLANES, SSD_CONV_DIM), F32)],
        scratch_shapes=[pltpu.VMEM((SUBLANES + Lc, SSD_CONV_DIM), F32),
                        pltpu.VMEM((SSD_D_STATE, SSD_D_INNER), F32),
                        pltpu.VMEM((Lc, SSD_CONV_DIM), F32),
                        pltpu.VMEM((Lc, SSD_D_INNER), F32)],
        compiler_params=_params(("arbitrary", "arbitrary")),
        name="ssd_scan",
    )(z, xbc, dt, conv_prev8, h0_t, p["conv_w"], p["conv_b"], p["dt_bias"], p["a_log"],
      p["d_skip"], p["g_ssd"])
    return y, h_t, conv8


def _bias_kernel(idx_ref, tab_ref, o_ref):
    h = pl.program_id(0)
    idx = idx_ref[...]
    acc = jnp.full(idx.shape, NEG_INF, F32)
    for b in range(N_BUCKETS):
        acc = jnp.where(idx == b, tab_ref[b, h], acc)
    o_ref[0] = acc


def _bias_from_table(idx, table):
    Q, K = idx.shape
    return pl.pallas_call(
        _bias_kernel,
        grid=(SWA_N_HEADS,),
        in_specs=[pl.BlockSpec((Q, K), lambda h: (0, 0)),
                  pl.BlockSpec(memory_space=pltpu.SMEM)],
        out_specs=pl.BlockSpec((1, Q, K), lambda h: (h, 0, 0)),
        out_shape=jax.ShapeDtypeStruct((SWA_N_HEADS, Q, K), F32),
        compiler_params=_params(("arbitrary",)),
        name="rel_bias",
    )(jnp.asarray(idx, jnp.int32), table.astype(F32))


def _t5_bucket_np(rel):
    nb = N_BUCKETS // 2
    max_exact = nb // 2
    ret = np.where(rel > 0, nb, 0)
    n = np.abs(rel)
    nf = np.maximum(n, 1).astype(np.float32)
    large = max_exact + (np.log(nf / np.float32(max_exact)) / np.float32(math.log(MAX_DISTANCE / max_exact))
                         * np.float32(nb - max_exact)).astype(np.int32)
    large = np.minimum(large, nb - 1)
    return (ret + np.where(n < max_exact, n, large)).astype(np.int32)


def _bucket_map(q_pos, k_pos):
    qc, kc = q_pos // CHUNK, k_pos // CHUNK
    valid = (kc[None, :] >= qc[:, None] - WINDOW_CHUNKS) & (kc[None, :] <= qc[:, None])
    return np.where(valid, _t5_bucket_np(k_pos[None, :] - q_pos[:, None]), -1).astype(np.int32)


def _swa_kernel(q_ref, kp_ref, vp_ref, kc_ref, vc_ref, bias_ref, sink_ref, o_ref, s_scr, p_scr, t_scr,
                *, mask_first):
    TQ = q_ref.shape[0]
    NP = kp_ref.shape[0]
    NK = NP + kc_ref.shape[0]
    HD = SWA_HEAD_DIM
    PW = 2 * HD
    pairs = SWA_REP // 2
    ndims = (((1,), (1,)), ((), ()))
    first_o = lax.broadcasted_iota(jnp.int32, (TQ, PW), 1) < HD
    first_s = lax.broadcasted_iota(jnp.int32, (TQ, 2 * NK), 1) < NK
    zero = jnp.zeros((NK, HD), BF16)
    one = jnp.ones((NK, HD), BF16)
    for g in range(SWA_N_KV):
        ks = slice(g * HD, (g + 1) * HD)
        k_g = jnp.concatenate([kp_ref[:, ks], kc_ref[:, ks]], axis=0).astype(BF16)
        v_g = jnp.concatenate([vp_ref[:, ks], vc_ref[:, ks]], axis=0).astype(BF16)
        kk = jnp.concatenate([jnp.concatenate([k_g, zero], axis=1),
                              jnp.concatenate([zero, k_g], axis=1)], axis=0)
        vv = jnp.concatenate([jnp.concatenate([v_g, zero, one, zero], axis=1),
                              jnp.concatenate([zero, v_g, zero, one], axis=1)], axis=0)
        for pr in range(pairs):
            pidx = g * pairs + pr
            s_scr[pr * TQ:(pr + 1) * TQ, :] = (
                lax.dot_general(q_ref[:, pidx * PW:(pidx + 1) * PW], kk, ndims, preferred_element_type=F32)
                + bias_ref[pidx])
        if mask_first:
            @pl.when(pl.program_id(1) == 0)
            def _mask_prev():
                col = lax.broadcasted_iota(jnp.int32, s_scr.shape, 1)
                s_scr[...] = jnp.where((col & (NK - 1)) < NP, NEG_INF, s_scr[...])
        for pr in range(pairs):
            pidx = g * pairs + pr
            rows = slice(pr * TQ, (pr + 1) * TQ)
            s = s_scr[rows, :]
            sink_a = sink_ref[2 * pidx]
            sink_b = sink_ref[2 * pidx + 1]
            ma = jnp.maximum(jnp.max(s[:, :NK], axis=-1, keepdims=True), sink_a)
            mb = jnp.maximum(jnp.max(s[:, NK:], axis=-1, keepdims=True), sink_b)
            p_scr[rows, :] = jnp.exp(s - jnp.where(first_s, ma, mb)).astype(BF16)
            t_scr[rows, :] = jnp.where(first_o, jnp.exp(sink_a - ma), jnp.exp(sink_b - mb))
        for pr in range(pairs):
            pidx = g * pairs + pr
            rows = slice(pr * TQ, (pr + 1) * TQ)
            ov = jnp.dot(p_scr[rows, :], vv, preferred_element_type=F32)
            o = ov[:, :PW] / (ov[:, PW:] + t_scr[rows, :])
            o_ref[:, pidx * PW:(pidx + 1) * PW] = o.astype(o_ref.dtype)


def _pair_bias(bias_prev, bias_cur):
    full = jnp.concatenate([bias_prev, bias_cur], axis=-1)
    H, Q, NK = full.shape
    return jnp.transpose(full.reshape(H // 2, 2, Q, NK), (0, 2, 1, 3)).reshape(H // 2, Q, 2 * NK)


def _swa(q, k_prev_arr, v_prev_arr, prev_map, k_cur_arr, v_cur_arr, cur_map, n_cur, bias, sinks,
         B, L, TQ, mask_first):
    T = B * L
    nblk = L // TQ
    kvw = SWA_N_KV * SWA_HEAD_DIM
    n_keys = bias.shape[2] // 2
    n_prev = n_keys - n_cur
    assert n_keys & (n_keys - 1) == 0
    pairs = SWA_REP // 2
    return pl.pallas_call(
        functools.partial(_swa_kernel, mask_first=mask_first),
        grid=(B, nblk),
        in_specs=[pl.BlockSpec((TQ, SWA_N_HEADS * SWA_HEAD_DIM), lambda b, i: (b * nblk + i, 0)),
                  pl.BlockSpec((n_prev, kvw), prev_map[0]),
                  pl.BlockSpec((n_prev, kvw), prev_map[1]),
                  pl.BlockSpec((n_cur, kvw), cur_map[0]),
                  pl.BlockSpec((n_cur, kvw), cur_map[1]),
                  pl.BlockSpec(bias.shape, lambda b, i: (0, 0, 0)),
                  pl.BlockSpec(memory_space=pltpu.SMEM)],
        out_specs=pl.BlockSpec((TQ, SWA_N_HEADS * SWA_HEAD_DIM), lambda b, i: (b * nblk + i, 0)),
        out_shape=jax.ShapeDtypeStruct((T, SWA_N_HEADS * SWA_HEAD_DIM), BF16),
        scratch_shapes=[pltpu.VMEM((pairs * TQ, 2 * n_keys), F32),
                        pltpu.VMEM((pairs * TQ, 2 * n_keys), BF16),
                        pltpu.VMEM((pairs * TQ, 2 * SWA_HEAD_DIM), F32)],
        compiler_params=_params(("parallel", "arbitrary")),
        name="swa_attention",
    )(q, k_prev_arr, v_prev_arr, k_cur_arr, v_cur_arr, bias, sinks.astype(F32))


def _mem_kernel(q_ref, mk_ref, mv_ref, o_ref):
    scale = MEM_HEAD_DIM ** -0.5
    ndims = (((1,), (1,)), ((), ()))
    for h in range(MEM_N_HEADS):
        hs = slice(h * MEM_HEAD_DIM, (h + 1) * MEM_HEAD_DIM)
        s = lax.dot_general(q_ref[:, hs], mk_ref[:, hs], ndims, preferred_element_type=F32) * scale
        m = jnp.max(s, axis=-1, keepdims=True)
        p = jnp.exp(s - m)
        den = jnp.sum(p, axis=-1, keepdims=True)
        o = jnp.dot(p.astype(BF16), mv_ref[:, hs], preferred_element_type=F32)
        o_ref[:, hs] = (o / den).astype(o_ref.dtype)


def _mem_attend(q, mk, mv, L, tm):
    T, W = q.shape
    M = mk.shape[0] // (T // L)
    return pl.pallas_call(
        _mem_kernel,
        grid=(T // tm,),
        in_specs=[pl.BlockSpec((tm, W), lambda i: (i, 0)),
                  pl.BlockSpec((M, W), lambda i: ((i * tm) // L, 0)),
                  pl.BlockSpec((M, W), lambda i: ((i * tm) // L, 0))],
        out_specs=pl.BlockSpec((tm, W), lambda i: (i, 0)),
        out_shape=jax.ShapeDtypeStruct((T, W), BF16),
        compiler_params=_params(("parallel",)),
        name="mem_attention",
    )(q, mk, mv)


def _merge_kernel(ys_ref, os_ref, om_ref, w1_ref, w2_ref, w3_ref, g0_ref, g1_ref, g2_ref, o_ref):
    a = jnp.dot(ys_ref[...], w1_ref[...], preferred_element_type=F32)
    b = jnp.dot(os_ref[...], w2_ref[...], preferred_element_type=F32)
    c = jnp.dot(om_ref[...], w3_ref[...], preferred_element_type=F32)
    o = (g0_ref[...].astype(F32) * a + g1_ref[...].astype(F32) * b + g2_ref[...].astype(F32) * c)
    o_ref[...] = o.astype(o_ref.dtype)


def _merge(y_ssd, o_s, o_m, w1, w2, w3, gates, tm, tn):
    T = y_ssd.shape[0]
    D = w1.shape[1]
    nj = D // tn
    row = lambda i, j: (i, 0)
    colw = lambda i, j: (0, j)
    return pl.pallas_call(
        _merge_kernel,
        grid=(T // tm, nj),
        in_specs=[pl.BlockSpec((tm, y_ssd.shape[1]), row),
                  pl.BlockSpec((tm, o_s.shape[1]), row),
                  pl.BlockSpec((tm, o_m.shape[1]), row),
                  pl.BlockSpec((w1.shape[0], tn), colw),
                  pl.BlockSpec((w2.shape[0], tn), colw),
                  pl.BlockSpec((w3.shape[0], tn), colw),
                  pl.BlockSpec((tm, tn), lambda i, j: (i, j)),
                  pl.BlockSpec((tm, tn), lambda i, j: (i, j + nj)),
                  pl.BlockSpec((tm, tn), lambda i, j: (i, j + 2 * nj))],
        out_specs=pl.BlockSpec((tm, tn), lambda i, j: (i, j)),
        out_shape=jax.ShapeDtypeStruct((T, D), BF16),
        compiler_params=_params(("parallel", "arbitrary")),
        name="gated_merge",
    )(y_ssd, o_s, o_m, w1, w2, w3, gates, gates, gates)


def _norm_route_kernel(xa_ref, xb_ref, g_ref, whi_ref, wlo_ref, h_ref, r_ref, *, n_a):
    x = jnp.where(pl.program_id(0) < n_a, xa_ref[...], xb_ref[...])
    ms = jnp.mean(x * x, axis=-1, keepdims=True)
    h = x * lax.rsqrt(ms + EPS) * g_ref[...]
    h_ref[:, 0, :] = h
    hb = h.astype(BF16)
    lo = (h - hb.astype(F32)).astype(BF16)
    lg = (jnp.dot(hb, whi_ref[...], preferred_element_type=F32)
          + jnp.dot(lo, whi_ref[...], preferred_element_type=F32)
          + jnp.dot(hb, wlo_ref[...], preferred_element_type=F32))
    lane = lax.broadcasted_iota(jnp.int32, lg.shape, 1)
    lane_f = lane.astype(F32)
    far = float(LANES)
    gl = jnp.where((lane >= N_EXPERTS) & (lane < N_EXPERTS + N_EXPERT_GROUPS), lg, NEG_INF)
    gmax = jnp.max(gl, axis=-1, keepdims=True)
    gidx = jnp.min(jnp.where(gl == gmax, lane_f - N_EXPERTS, far), axis=-1, keepdims=True)
    gw = 1.0 / jnp.sum(jnp.exp(gl - gmax), axis=-1, keepdims=True)
    lo_e = gidx * EXPERTS_PER_GROUP
    el = jnp.where((lane_f >= lo_e) & (lane_f < lo_e + EXPERTS_PER_GROUP), lg, NEG_INF)
    v1 = jnp.max(el, axis=-1, keepdims=True)
    i1 = jnp.min(jnp.where(el == v1, lane_f, far), axis=-1, keepdims=True)
    el2 = jnp.where(lane_f == i1, NEG_INF, el)
    v2 = jnp.max(el2, axis=-1, keepdims=True)
    i2 = jnp.min(jnp.where(el2 == v2, lane_f, far), axis=-1, keepdims=True)
    e = jnp.exp(v2 - v1)
    w1 = gw / (1.0 + e)
    w2 = gw * e / (1.0 + e)
    r_ref[...] = jnp.where(lane == 0, w1, jnp.where(lane == 1, w2,
                           jnp.where(lane == 2, i1, jnp.where(lane == 3, i2, 0.0))))


def _norm_route(xa, xb, g, w_hi, w_lo, tm):
    Ta, D = xa.shape
    Tb = xb.shape[0]
    T = Ta + Tb
    n_a, n_b = Ta // tm, Tb // tm
    assert n_a * tm == Ta and n_b * tm == Tb
    return pl.pallas_call(
        functools.partial(_norm_route_kernel, n_a=n_a),
        grid=(n_a + n_b,),
        in_specs=[pl.BlockSpec((tm, D), lambda i: (jnp.minimum(i, n_a - 1), 0)),
                  pl.BlockSpec((tm, D), lambda i: (jnp.maximum(i - n_a, 0), 0)),
                  pl.BlockSpec((1, D), lambda i: (0, 0)),
                  pl.BlockSpec((D, LANES), lambda i: (0, 0)),
                  pl.BlockSpec((D, LANES), lambda i: (0, 0))],
        out_specs=[pl.BlockSpec((tm, 1, D), lambda i: (i, 0, 0)),
                   pl.BlockSpec((tm, LANES), lambda i: (i, 0))],
        out_shape=[jax.ShapeDtypeStruct((T, 1, D), F32),
                   jax.ShapeDtypeStruct((T, LANES), F32)],
        compiler_params=_params(("parallel",)),
        name="ffn_norm_router",
    )(xa, xb, g.reshape(1, D).astype(F32), w_hi, w_lo)


def _moe_kernel(be_ref, nv_ref, s0_ref, ord_ref, h_hbm, w1_ref, w3_ref, w2_ref, yu_hbm,
                xbuf, ybuf, w1b, w3b, w2b, sem_in, sem_out, *, n_tok):
    i = pl.program_id(0)
    nb = pl.num_programs(0)
    rows = xbuf.shape[1]
    n_assign = TOP_K * n_tok
    nv = nv_ref[i]
    slot = i % 2
    nxt = jnp.minimum(i + 1, nb - 1)
    next_valid = (i + 1 < nb) & (nv_ref[nxt] > 0)

    def assignment(blk, r):
        return ord_ref[s0_ref[blk] + r]

    def row_in(s, r, tok):
        return pltpu.make_async_copy(h_hbm.at[tok], xbuf.at[s, pl.ds(r, 1), :], sem_in.at[s])

    def row_out(s, r, dst):
        return pltpu.make_async_copy(ybuf.at[s, pl.ds(r, 1), :], yu_hbm.at[dst], sem_out.at[s])

    def gather_start(blk, s):
        for r in range(rows):
            row_in(s, r, lax.shift_right_logical(assignment(blk, r), 1)).start()

    def gather_wait(s):
        for r in range(rows):
            row_in(s, r, 0).wait()

    def scatter_start(blk, s):
        n_valid = nv_ref[blk]
        for r in range(rows):
            m = assignment(blk, r)
            dst = jnp.where(r < n_valid, (m & 1) * n_tok + lax.shift_right_logical(m, 1),
                            n_assign + s * rows + r)
            row_out(s, r, dst).start()

    def scatter_wait(s):
        for r in range(rows):
            row_out(s, r, 0).wait()

    @pl.when(i == 0)
    def _clear_spare_rows():
        ybuf[...] = jnp.zeros(ybuf.shape, ybuf.dtype)
        for s in range(2):
            for r in range(rows):
                row_out(s, r, n_assign + s * rows + r).start()
            scatter_wait(s)

    @pl.when((i == 0) & (nv > 0))
    def _prologue():
        gather_start(i, 0)

    @pl.when(next_valid)
    def _prefetch():
        gather_start(nxt, 1 - slot)

    prev = be_ref[jnp.maximum(i - 1, 0)]

    @pl.when((i == 0) | (be_ref[i] != prev))
    def _load_expert():
        w1b[...] = w1_ref[0].astype(BF16)
        w3b[...] = w3_ref[0].astype(BF16)
        w2b[...] = w2_ref[0].astype(BF16)

    @pl.when(nv > 0)
    def _compute():
        gather_wait(slot)

        @pl.when(i >= 2)
        def _free_ybuf():
            scatter_wait(slot)

        x = xbuf[slot].astype(BF16)
        a = jnp.dot(x, w1b[...], preferred_element_type=F32)
        b = jnp.dot(x, w3b[...], preferred_element_type=F32)
        mid = (_silu(a) * b).astype(BF16)
        y = jnp.dot(mid, w2b[...], preferred_element_type=F32)
        ybuf[slot] = y
        scatter_start(i, slot)

        @pl.when(jnp.logical_not(next_valid))
        def _drain():
            scatter_wait(slot)

            @pl.when(i >= 1)
            def _drain_prev():
                scatter_wait(1 - slot)


def _moe(h2, route, w_gate, w_up, w_down):
    T, _, D = h2.shape
    F = w_gate.shape[2]
    assert TOP_K == 2
    M = T * TOP_K
    BM = MOE_ROWS
    nb = (M + N_EXPERTS * (BM - 1) + BM - 1) // BM
    e_flat = route[:, TOP_K:2 * TOP_K].astype(jnp.int32).reshape(M)
    idx_bits = (M - 1).bit_length()
    assert (N_EXPERTS << idx_bits) < 2 ** 31
    packed = lax.sort(e_flat * (1 << idx_bits) + jnp.arange(M, dtype=jnp.int32), is_stable=False)
    order = jnp.pad(packed & ((1 << idx_bits) - 1), (0, BM))
    experts = jnp.arange(N_EXPERTS, dtype=jnp.int32)
    counts = jnp.sum((experts[:, None] == e_flat[None, :]).astype(jnp.int32), axis=1)
    padded = (counts + BM - 1) // BM * BM
    pad_end = jnp.cumsum(padded)
    pad_start = pad_end - padded
    start = jnp.cumsum(counts) - counts
    blk0 = jnp.arange(nb, dtype=jnp.int32) * BM
    blk_exp = jnp.minimum(jnp.sum((pad_end[None, :] <= blk0[:, None]).astype(jnp.int32), axis=1), N_EXPERTS - 1)
    pick = (blk_exp[:, None] == experts[None, :]).astype(jnp.int32)
    off0 = blk0 - jnp.sum(pick * pad_start[None, :], axis=1)
    blk_nv = jnp.clip(jnp.sum(pick * counts[None, :], axis=1) - off0, 0, BM).astype(jnp.int32)
    blk_s0 = jnp.clip(jnp.sum(pick * start[None, :], axis=1) + off0, 0, M - 1).astype(jnp.int32)

    grid_spec = pltpu.PrefetchScalarGridSpec(
        num_scalar_prefetch=4,
        grid=(nb,),
        in_specs=[pl.BlockSpec(memory_space=pl.ANY),
                  pl.BlockSpec((1, D, F), lambda i, be, nv, s0, od: (be[i], 0, 0)),
                  pl.BlockSpec((1, D, F), lambda i, be, nv, s0, od: (be[i], 0, 0)),
                  pl.BlockSpec((1, F, D), lambda i, be, nv, s0, od: (be[i], 0, 0))],
        out_specs=pl.BlockSpec(memory_space=pl.ANY),
        scratch_shapes=[pltpu.VMEM((2, BM, D), F32),
                        pltpu.VMEM((2, BM, D), F32),
                        pltpu.VMEM((D, F), BF16),
                        pltpu.VMEM((D, F), BF16),
                        pltpu.VMEM((F, D), BF16),
                        pltpu.SemaphoreType.DMA((2,)),
                        pltpu.SemaphoreType.DMA((2,))],
    )
    return pl.pallas_call(
        functools.partial(_moe_kernel, n_tok=T),
        grid_spec=grid_spec,
        out_shape=jax.ShapeDtypeStruct((M + 2 * BM, 1, D), F32),
        compiler_params=_params(("arbitrary",)),
        name="moe_experts",
    )(blk_exp.astype(jnp.int32), blk_nv, blk_s0, order, h2, w_gate, w_up, w_down)


def _combine_kernel(x_ref, r_ref, y0_ref, y1_ref, o_ref):
    w0 = r_ref[:, 0:1]
    w1 = r_ref[:, 1:2]
    o_ref[...] = x_ref[...] + (w0 * y0_ref[:, 0, :] + w1 * y1_ref[:, 0, :])


def _combine(x, route, yu, blk0, tm):
    T, D = x.shape
    n_all = route.shape[0] // tm
    return pl.pallas_call(
        _combine_kernel,
        grid=(T // tm,),
        in_specs=[pl.BlockSpec((tm, D), lambda i: (i, 0)),
                  pl.BlockSpec((tm, LANES), lambda i: (blk0 + i, 0)),
                  pl.BlockSpec((tm, 1, D), lambda i: (blk0 + i, 0, 0)),
                  pl.BlockSpec((tm, 1, D), lambda i: (n_all + blk0 + i, 0, 0))],
        out_specs=pl.BlockSpec((tm, D), lambda i: (i, 0)),
        out_shape=jax.ShapeDtypeStruct((T, D), F32),
        compiler_params=_params(("parallel",)),
        name="moe_combine",
    )(x, route, yu, yu)


def _layer(x, B, L, p, conv_prev8, h0_t, mk, mv, swa_prev, swa_bias, Lc, TQ):
    T, D = x.shape
    tmm = min(MM_ROWS, T)
    z, h = _norm_matmul(x, p["g_mix"], p["wz"], BF16, tmm, MM_COLS, name="norm_proj_z")
    xbc = _matmul(h, p["wxbc"], BF16, tmm, MM_COLS_WIDE, name="proj_xbc")
    dt = _matmul(h, p["wdt"], F32, tmm, PAD_HEADS, name="proj_dt")
    q_s = _matmul(h, p["wqs"], BF16, tmm, MM_COLS, epi=_epi_group_norm, col_extras=(p["g_q_swa"],),
                  const_extras=(p["mavg"],), name="proj_q_swa")
    kv = _matmul(h, p["wkv"], F32, tmm, 256, epi=_epi_kv, col_extras=(p["g_kv"],),
                 const_extras=(p["mavg"],), name="proj_kv_swa")
    q_m = _matmul(h, p["wqm"], BF16, tmm, MM_COLS, epi=_epi_row_norm, col_extras=(p["g_q_mem"],),
                  name="proj_q_mem")
    gates = _matmul(h, p["wg"], BF16, tmm, MM_COLS_WIDE, epi=_epi_sigmoid, name="proj_gates")

    y_ssd, h_t, conv8 = _ssd(z, xbc, dt, conv_prev8, h0_t, p, B, L, Lc)

    nblk = L // TQ
    kvw = SWA_N_KV * SWA_HEAD_DIM
    if swa_prev is None:
        prev_map = (lambda b, i: (b * nblk + jnp.maximum(i - 1, 0), 0),
                    lambda b, i: (b * nblk + jnp.maximum(i - 1, 0), 1))
        cur_map = (lambda b, i: (b * nblk + i, 0), lambda b, i: (b * nblk + i, 1))
        o_s = _swa(q_s, kv, kv, prev_map, kv, kv, cur_map, TQ, swa_bias, p["sinks"], B, L, TQ, True)
    else:
        n_cur = swa_bias.shape[2] // 2 - swa_prev[0].shape[0] // B
        kv3 = jnp.pad(kv.reshape(B, L, 2 * kvw), ((0, 0), (0, n_cur - L), (0, 0)))
        k_new = kv3[:, :, :kvw].reshape(B * n_cur, kvw)
        v_new = kv3[:, :, kvw:].reshape(B * n_cur, kvw)
        per_b = (lambda b, i: (b, 0), lambda b, i: (b, 0))
        o_s = _swa(q_s, swa_prev[0], swa_prev[1], per_b, k_new, v_new, per_b, n_cur, swa_bias, p["sinks"],
                   B, L, TQ, False)

    o_m = _mem_attend(q_m, mk, mv, L, min(256, L))

    merged = _merge(y_ssd, o_s, o_m, p["w_o_ssd"], p["w_o_swa"], p["w_o_mem"], gates, tmm, MERGE_COLS)
    x1 = _matmul(merged, p["w_out"], F32, tmm, MM_COLS, epi=_epi_residual, tile_extras=(x,), name="proj_out")

    return x1, conv8, h_t, kv


def _state_to_heads(h_t, B):
    return jnp.transpose(h_t.reshape(B, SSD_D_STATE, SSD_N_HEADS, SSD_HEAD_DIM), (0, 2, 3, 1))


def kernel(x_prompt, x_sample, cache_conv, state_ssd, cache_swa_k, cache_swa_v, cache_mem_k, cache_mem_v, mem_prompt, rel_bias_table, g_mix, w_in, conv_w, conv_b, dt_bias, a_log, d_skip, g_ssd, w_o_ssd, g_q_swa, g_k_swa, sinks, w_o_swa, g_mem, w_mem_k, w_mem_v, g_q_mem, g_k_mem, w_o_mem, w_out, g_ffn, w_router_grp, w_router_exp, w_exp_gate, w_exp_up, w_exp_down):
    B, S, D = x_prompt.shape
    Bd, Sd, _ = x_sample.shape
    depth = w_in.shape[0]
    assert depth == 1
    l = 0
    kvw = SWA_N_KV * SWA_HEAD_DIM
    qw = SWA_N_HEADS * SWA_HEAD_DIM
    mw = MEM_N_HEADS * MEM_HEAD_DIM

    sizes = (SSD_D_INNER, SSD_CONV_DIM, SSD_N_HEADS, qw, kvw, kvw, mw, 3 * D)
    offs = np.concatenate([[0], np.cumsum(sizes)])
    w = w_in[l]
    cols = [w[:, int(offs[k]):int(offs[k + 1])] for k in range(len(sizes))]
    pad_h = PAD_HEADS - SSD_N_HEADS
    w_r = jnp.pad(jnp.concatenate([w_router_exp[l], w_router_grp[l]], axis=1),
                  ((0, 0), (0, LANES - N_EXPERTS - N_EXPERT_GROUPS)))
    w_r_hi = w_r.astype(BF16)
    p = {
        "g_mix": g_mix[l],
        "wz": cols[0].astype(BF16),
        "wxbc": cols[1].astype(BF16),
        "wdt": jnp.pad(cols[2], ((0, 0), (0, pad_h))).astype(BF16),
        "wqs": cols[3].astype(BF16),
        "wkv": jnp.concatenate([cols[4], cols[5]], axis=1).astype(BF16),
        "wqm": cols[6].astype(BF16),
        "wg": cols[7].astype(BF16),
        "conv_w": conv_w[l].astype(F32),
        "conv_b": conv_b[l].reshape(1, SSD_CONV_DIM).astype(F32),
        "dt_bias": jnp.pad(dt_bias[l], (0, pad_h)).reshape(1, PAD_HEADS).astype(F32),
        "a_log": jnp.pad(a_log[l], (0, pad_h)).reshape(1, PAD_HEADS).astype(F32),
        "d_skip": jnp.repeat(d_skip[l], SSD_HEAD_DIM).reshape(1, SSD_D_INNER).astype(F32),
        "g_ssd": g_ssd[l].reshape(1, SSD_D_INNER).astype(F32),
        "g_q_swa": (jnp.tile(g_q_swa[l], SWA_N_HEADS) * SWA_HEAD_DIM ** -0.5).reshape(1, qw).astype(F32),
        "g_kv": jnp.concatenate([jnp.tile(g_k_swa[l], SWA_N_KV), jnp.ones((kvw,), F32)]).reshape(1, 2 * kvw),
        "g_q_mem": jnp.tile(g_q_mem[l], MEM_N_HEADS).reshape(1, mw).astype(F32),
        "mavg": _group_mean_matrix(256, SWA_HEAD_DIM),
        "sinks": sinks[l],
        "w_o_ssd": w_o_ssd[l].astype(BF16),
        "w_o_swa": w_o_swa[l].astype(BF16),
        "w_o_mem": w_o_mem[l].astype(BF16),
        "w_out": w_out[l].astype(BF16),
        "g_ffn": g_ffn[l],
        "w_r_hi": w_r_hi,
        "w_r_lo": (w_r - w_r_hi.astype(F32)).astype(BF16),
        "w_exp_gate": w_exp_gate[l],
        "w_exp_up": w_exp_up[l],
        "w_exp_down": w_exp_down[l],
    }

    M = mem_prompt.shape[1]
    mn = _rmsnorm(mem_prompt.reshape(B * M, D), g_mem[l], min(256, B * M))
    mk_p = _matmul(mn, w_mem_k[l].astype(BF16), F32, min(256, B * M), MEM_HEAD_DIM, epi=_epi_row_norm,
                   col_extras=(jnp.tile(g_k_mem[l], MEM_N_HEADS).reshape(1, mw).astype(F32),), name="mem_k")
    mv_p = _matmul(mn, w_mem_v[l].astype(BF16), F32, min(256, B * M), MEM_HEAD_DIM, name="mem_v")

    TQ = 2 * CHUNK
    qpos = np.arange(TQ)
    bias_p = _pair_bias(_bias_from_table(_bucket_map(qpos, np.arange(TQ) - TQ), rel_bias_table),
                        _bias_from_table(_bucket_map(qpos, np.arange(TQ)), rel_bias_table))
    C = cache_swa_k.shape[2]
    qpos_s = PAST_LEN + np.arange(Sd)
    cur_map_s = _bucket_map(qpos_s, PAST_LEN + np.arange(C))
    cur_map_s[:, Sd:] = -1
    bias_s = _pair_bias(_bias_from_table(_bucket_map(qpos_s, PAST_LEN - C + np.arange(C)), rel_bias_table),
                        _bias_from_table(cur_map_s, rel_bias_table))

    conv0 = jnp.zeros((B, SUBLANES, SSD_CONV_DIM), F32)
    h0 = jnp.zeros((B, SSD_D_STATE, SSD_D_INNER), F32)
    x1_p, conv8_p, ht_p, kv_p = _layer(x_prompt.reshape(B * S, D), B, S, p, conv0, h0,
                                     mk_p.astype(BF16), mv_p.astype(BF16), None,
                                     bias_p, CHUNK, TQ)
    conv_prev = jnp.pad(cache_conv[l], ((0, 0), (SUBLANES - (SSD_CONV - 1), 0), (0, 0)))
    h0_s = jnp.transpose(state_ssd[l], (0, 3, 1, 2)).reshape(Bd, SSD_D_STATE, SSD_D_INNER)
    x1_s, conv8_s, ht_s, kv_s = _layer(x_sample.reshape(Bd * Sd, D), Bd, Sd, p, conv_prev, h0_s,
                                     cache_mem_k[l].reshape(Bd * M, mw).astype(BF16),
                                     cache_mem_v[l].reshape(Bd * M, mw).astype(BF16),
                                     (cache_swa_k[l].reshape(Bd * C, kvw), cache_swa_v[l].reshape(Bd * C, kvw)),
                                     bias_s, Sd, Sd)

    tr = math.gcd(B * S, Bd * Sd, MOE_ROWS)
    h2, route = _norm_route(x1_p, x1_s, p["g_ffn"], p["w_r_hi"], p["w_r_lo"], tr)
    yu = _moe(h2, route, p["w_exp_gate"], p["w_exp_up"], p["w_exp_down"])
    yp = _combine(x1_p, route, yu, 0, tr)
    ys = _combine(x1_s, route, yu, (B * S) // tr, tr)

    keep = min(WINDOW, S)
    kv_p = kv_p.reshape(B, S, 2 * kvw)[:, S - keep:].reshape(B, keep, 2, SWA_N_KV, SWA_HEAD_DIM)
    kv_s = kv_s.reshape(Bd, Sd, 2, SWA_N_KV, SWA_HEAD_DIM)
    tail = SUBLANES - (SSD_CONV - 1)
    return (yp.reshape(B, S, D), ys.reshape(Bd, Sd, D),
            conv8_p[None, :, tail:], _state_to_heads(ht_p, B)[None],
            kv_p[None, :, :, 0], kv_p[None, :, :, 1],
            mk_p.reshape(1, B, M, MEM_N_HEADS, MEM_HEAD_DIM), mv_p.reshape(1, B, M, MEM_N_HEADS, MEM_HEAD_DIM),
            conv8_s[None, :, tail:], _state_to_heads(ht_s, Bd)[None],
            kv_s[None, :, :, 0], kv_s[None, :, :, 1])
```

```python
import functools
import math

import numpy as np
import jax
import jax.numpy as jnp
from jax import lax
from jax.experimental import pallas as pl
from jax.experimental.pallas import tpu as pltpu

F32 = jnp.float32
BF16 = jnp.bfloat16
EPS = 1e-6
NEG_INF = float("-inf")

CHUNK = 64
SSD_HEAD_DIM = 64
SSD_N_HEADS = 64
SSD_N_GROUPS = 8
SSD_D_STATE = 128
SSD_D_INNER = SSD_N_HEADS * SSD_HEAD_DIM
SSD_GN = SSD_N_GROUPS * SSD_D_STATE
SSD_CONV_DIM = SSD_D_INNER + 2 * SSD_GN
SSD_CONV = 4
SWA_N_HEADS = 32
SWA_N_KV = 4
SWA_HEAD_DIM = 64
SWA_REP = SWA_N_HEADS // SWA_N_KV
WINDOW = 128
WINDOW_CHUNKS = WINDOW // CHUNK
MEM_N_HEADS = 4
MEM_HEAD_DIM = 512
N_BUCKETS = 32
MAX_DISTANCE = 128
N_EXPERT_GROUPS = 4
EXPERTS_PER_GROUP = 16
N_EXPERTS = N_EXPERT_GROUPS * EXPERTS_PER_GROUP
TOP_K = 2
PAST_LEN = 4096

LANES = 128
SUBLANES = 8
VMEM_LIMIT = 56 * 1024 * 1024
MOE_ROWS = 256
MM_ROWS = 1024
MM_COLS = 1024
MM_COLS_WIDE = 1536
MERGE_COLS = 256
PAD_HEADS = LANES


def _params(sem, vmem=VMEM_LIMIT):
    return pltpu.CompilerParams(dimension_semantics=sem, vmem_limit_bytes=vmem)


def _sigmoid(x):
    return 0.5 * (jnp.tanh(0.5 * x) + 1.0)


def _silu(x):
    u = 0.5 * x
    return u * (jnp.tanh(u) + 1.0)


def _split3(x):
    x1 = x.astype(BF16)
    r1 = x - x1.astype(F32)
    x2 = r1.astype(BF16)
    x3 = (r1 - x2.astype(F32)).astype(BF16)
    return x1, x2, x3


def _rmsnorm_kernel(x_ref, g_ref, o_ref):
    x = x_ref[...]
    ms = jnp.mean(x * x, axis=-1, keepdims=True)
    o_ref[...] = (x * lax.rsqrt(ms + EPS) * g_ref[...]).astype(o_ref.dtype)


def _rmsnorm(x, g, tm):
    T, D = x.shape
    return pl.pallas_call(
        _rmsnorm_kernel,
        grid=(T // tm,),
        in_specs=[pl.BlockSpec((tm, D), lambda i: (i, 0)),
                  pl.BlockSpec((1, D), lambda i: (0, 0))],
        out_specs=pl.BlockSpec((tm, D), lambda i: (i, 0)),
        out_shape=jax.ShapeDtypeStruct((T, D), BF16),
        compiler_params=_params(("parallel",)),
        name="rmsnorm",
    )(x, g.reshape(1, D).astype(F32))


def _mm_kernel(a_ref, b_ref, *refs, epi):
    o_ref = refs[-1]
    acc = jnp.dot(a_ref[...], b_ref[...], preferred_element_type=F32)
    if epi is not None:
        acc = epi(acc, *[r[...] for r in refs[:-1]])
    o_ref[...] = acc.astype(o_ref.dtype)


def _matmul(a, b, out_dtype, tm, tn, epi=None, col_extras=(), tile_extras=(), const_extras=(), name="matmul"):
    M, K = a.shape
    N = b.shape[1]
    assert M % tm == 0 and N % tn == 0, (M, N, tm, tn)
    in_specs = [pl.BlockSpec((tm, K), lambda i, j: (i, 0)),
                pl.BlockSpec((K, tn), lambda i, j: (0, j))]
    for _ in col_extras:
        in_specs.append(pl.BlockSpec((1, tn), lambda i, j: (0, j)))
    for _ in tile_extras:
        in_specs.append(pl.BlockSpec((tm, tn), lambda i, j: (i, j)))
    for c in const_extras:
        in_specs.append(pl.BlockSpec(c.shape, lambda i, j: (0, 0)))
    return pl.pallas_call(
        functools.partial(_mm_kernel, epi=epi),
        grid=(M // tm, N // tn),
        in_specs=in_specs,
        out_specs=pl.BlockSpec((tm, tn), lambda i, j: (i, j)),
        out_shape=jax.ShapeDtypeStruct((M, N), out_dtype),
        compiler_params=_params(("parallel", "arbitrary")),
        name=name,
    )(a, b, *col_extras, *tile_extras, *const_extras)


def _norm_mm_kernel(x_ref, g_ref, b_ref, o_ref, h_ref):
    @pl.when(pl.program_id(1) == 0)
    def _normalise():
        x = x_ref[...]
        ms = jnp.mean(x * x, axis=-1, keepdims=True)
        h_ref[...] = (x * lax.rsqrt(ms + EPS) * g_ref[...]).astype(h_ref.dtype)

    o_ref[...] = jnp.dot(h_ref[...], b_ref[...], preferred_element_type=F32).astype(o_ref.dtype)


def _norm_matmul(x, g, b, out_dtype, tm, tn, name):
    M, K = x.shape
    N = b.shape[1]
    assert M % tm == 0 and N % tn == 0, (M, N, tm, tn)
    return pl.pallas_call(
        _norm_mm_kernel,
        grid=(M // tm, N // tn),
        in_specs=[pl.BlockSpec((tm, K), lambda i, j: (i, 0)),
                  pl.BlockSpec((1, K), lambda i, j: (0, 0)),
                  pl.BlockSpec((K, tn), lambda i, j: (0, j))],
        out_specs=[pl.BlockSpec((tm, tn), lambda i, j: (i, j)),
                   pl.BlockSpec((tm, K), lambda i, j: (i, 0))],
        out_shape=[jax.ShapeDtypeStruct((M, N), out_dtype),
                   jax.ShapeDtypeStruct((M, K), BF16)],
        compiler_params=_params(("parallel", "arbitrary")),
        name=name,
    )(x, g.reshape(1, K).astype(F32), b)


def _group_mean_matrix(width, group):
    idx = np.arange(width) // group
    return jnp.asarray((idx[:, None] == idx[None, :]).astype(np.float32) / group, dtype=BF16)


def _epi_group_norm(acc, gain, mavg):
    w = mavg.shape[0]
    outs = []
    for c in range(acc.shape[1] // w):
        a = acc[:, c * w:(c + 1) * w]
        s = a * a
        hi = s.astype(BF16)
        lo = (s - hi.astype(F32)).astype(BF16)
        ms = (jnp.dot(hi, mavg, preferred_element_type=F32)
              + jnp.dot(lo, mavg, preferred_element_type=F32))
        outs.append(a * lax.rsqrt(ms + EPS))
    normed = outs[0] if len(outs) == 1 else jnp.concatenate(outs, axis=1)
    return normed * gain


def _epi_kv_dt(acc, gain, mavg):
    kw = mavg.shape[0]
    return jnp.concatenate([_epi_group_norm(acc[:, :kw], gain, mavg), acc[:, kw:]], axis=1)


def _epi_row_norm(acc, gain):
    outs = []
    for c in range(acc.shape[1] // MEM_HEAD_DIM):
        a = acc[:, c * MEM_HEAD_DIM:(c + 1) * MEM_HEAD_DIM]
        outs.append(a * lax.rsqrt(jnp.mean(a * a, axis=-1, keepdims=True) + EPS))
    normed = outs[0] if len(outs) == 1 else jnp.concatenate(outs, axis=1)
    return normed * gain


def _epi_sigmoid(acc):
    return _sigmoid(acc)


def _epi_residual(acc, res):
    return acc + res


def _ssd_kernel(z_ref, xbc_ref, dt_ref, cprev_ref, h0_ref, cw_ref, cb_ref, dtb_ref, alog_ref,
                dskip_ref, gn_ref, y_ref, hout_ref, cout_ref, xp_s, h_s, conv_s, y_s, *, Lc):
    c = pl.program_id(1)
    n_chunks = pl.num_programs(1)
    P2 = 2 * SSD_HEAD_DIM
    L2 = 2 * Lc

    @pl.when(c == 0)
    def _init():
        xp_s[0:SUBLANES, :] = cprev_ref[0]
        h_s[...] = h0_ref[0]

    xp_s[SUBLANES:SUBLANES + Lc, :] = xbc_ref[...].astype(F32)
    cblk = 512
    row8 = lax.broadcasted_iota(jnp.int32, (SUBLANES, cblk), 0)
    for j in range(SSD_CONV_DIM // cblk):
        sl = slice(j * cblk, (j + 1) * cblk)
        cur = xp_s[SUBLANES:SUBLANES + Lc, sl]
        tail = xp_s[0:SUBLANES, sl]
        acc = cb_ref[:, sl] + cw_ref[SSD_CONV - 1:SSD_CONV, sl] * cur
        for k in range(SSD_CONV - 1):
            sh = SSD_CONV - 1 - k
            down = pltpu.roll(cur, sh, axis=0)
            top = jnp.where(row8 >= sh, down[0:SUBLANES], pltpu.roll(tail, sh, axis=0))
            acc = acc + cw_ref[k:k + 1, sl] * jnp.concatenate([top, down[SUBLANES:]], axis=0)
        conv_s[:, sl] = _silu(acc)
    xp_s[0:SUBLANES, :] = xp_s[Lc:Lc + SUBLANES, :]

    dtv = dt_ref[...] + dtb_ref[...]
    dt = jnp.maximum(dtv, 0.0) + jnp.log1p(jnp.exp(-jnp.abs(dtv)))
    adt = dt * (-jnp.exp(alog_ref[...]))
    row = lax.broadcasted_iota(jnp.int32, (Lc, Lc), 0)
    col = lax.broadcasted_iota(jnp.int32, (Lc, Lc), 1)
    tri = (col <= row).astype(BF16)
    row2 = lax.broadcasted_iota(jnp.int32, (Lc, L2), 0)
    col2 = lax.broadcasted_iota(jnp.int32, (Lc, L2), 1)
    col2m = jnp.where(col2 >= Lc, col2 - Lc, col2)
    tri_t2 = (row2 <= col2m).astype(BF16)
    causal2 = col2m <= row2
    a1, a2, a3 = _split3(adt)
    acs = (jnp.dot(tri, a1, preferred_element_type=F32)
           + jnp.dot(tri, a2, preferred_element_type=F32)
           + jnp.dot(tri, a3, preferred_element_type=F32))
    tdims = (((0,), (0,)), ((), ()))
    acs_t2 = (lax.dot_general(a1, tri_t2, tdims, preferred_element_type=F32)
              + lax.dot_general(a2, tri_t2, tdims, preferred_element_type=F32)
              + lax.dot_general(a3, tri_t2, tdims, preferred_element_type=F32))

    lane_p = lax.broadcasted_iota(jnp.int32, (Lc, P2), 1)
    first_p = lane_p < SSD_HEAD_DIM
    first_l = col2 < Lc
    first_l1 = first_l[0:1, :]
    ndims = (((1,), (1,)), ((), ()))

    for g in range(SSD_N_GROUPS):
        b_g = conv_s[:, SSD_D_INNER + g * SSD_D_STATE:SSD_D_INNER + (g + 1) * SSD_D_STATE].astype(BF16)
        c_g = conv_s[:, SSD_D_INNER + SSD_GN + g * SSD_D_STATE:
                     SSD_D_INNER + SSD_GN + (g + 1) * SSD_D_STATE].astype(BF16)
        b2 = jnp.concatenate([b_g, b_g], axis=0)
        cb2 = lax.dot_general(c_g, b2, ndims, preferred_element_type=F32)
        gw = SSD_HEAD_DIM * (SSD_N_HEADS // SSD_N_GROUPS)
        inter = jnp.dot(c_g, h_s[:, g * gw:(g + 1) * gw].astype(BF16), preferred_element_type=F32)
        for jj in range(gw // P2):
            j = g * (gw // P2) + jj
            sl = slice(j * P2, (j + 1) * P2)
            acs_a = acs[:, 2 * j:2 * j + 1]
            acs_b = acs[:, 2 * j + 1:2 * j + 2]
            col_l = jnp.where(first_l, acs_a, acs_b)
            row_l = jnp.where(first_l1, acs_t2[2 * j:2 * j + 1, :], acs_t2[2 * j + 1:2 * j + 2, :])
            dec = jnp.exp(jnp.where(causal2, col_l - row_l, NEG_INF))
            m_pair = (cb2 * dec).astype(BF16)
            col_p = col_l if L2 == P2 else jnp.where(first_p, acs_a, acs_b)
            dt_p = jnp.where(first_p, dt[:, 2 * j:2 * j + 1], dt[:, 2 * j + 1:2 * j + 2])
            xs_p = conv_s[:, sl]
            xdt = xs_p * dt_p
            rhs = jnp.concatenate([jnp.where(first_p, xdt, 0.0), jnp.where(first_p, 0.0, xdt)],
                                  axis=0).astype(BF16)
            y = jnp.dot(m_pair, rhs, preferred_element_type=F32)
            y = y + inter[:, jj * P2:(jj + 1) * P2] * jnp.exp(col_p) + dskip_ref[:, sl] * xs_p
            y_s[:, sl] = y
            a_end = col_p[Lc - 1:Lc, :]
            xw = (xdt * jnp.exp(a_end - col_p)).astype(BF16)
            h_s[:, sl] = (h_s[:, sl] * jnp.exp(a_end)
                          + lax.dot_general(b_g, xw, tdims, preferred_element_type=F32))

    gdim = SSD_D_INNER // SSD_N_GROUPS
    for g in range(SSD_N_GROUPS):
        sl = slice(g * gdim, (g + 1) * gdim)
        zz = z_ref[:, sl].astype(F32)
        yy = y_s[:, sl] * _silu(zz)
        ms = jnp.mean(yy * yy, axis=-1, keepdims=True)
        y_ref[:, sl] = (yy * lax.rsqrt(ms + EPS) * gn_ref[:, sl]).astype(y_ref.dtype)

    @pl.when(c == n_chunks - 1)
    def _fin():
        hout_ref[0] = h_s[...]
        cout_ref[0] = xp_s[0:SUBLANES, :]


def _ssd(z, xbc, dt, dt_col, conv_prev8, h0_t, p, B, L, Lc):
    T = B * L
    nc = L // Lc
    tok = lambda b, c: (b * nc + c, 0)
    tok_dt = lambda b, c: (b * nc + c, dt_col)
    per_b = lambda b, c: (b, 0, 0)
    whole = lambda b, c: (0, 0)
    y, h_t, conv8 = pl.pallas_call(
        functools.partial(_ssd_kernel, Lc=Lc),
        grid=(B, nc),
        in_specs=[pl.BlockSpec((Lc, SSD_D_INNER), tok),
                  pl.BlockSpec((Lc, SSD_CONV_DIM), tok),
                  pl.BlockSpec((Lc, PAD_HEADS), tok_dt),
                  pl.BlockSpec((1, SUBLANES, SSD_CONV_DIM), per_b),
                  pl.BlockSpec((1, SSD_D_STATE, SSD_D_INNER), per_b),
                  pl.BlockSpec((SSD_CONV, SSD_CONV_DIM), whole),
                  pl.BlockSpec((1, SSD_CONV_DIM), whole),
                  pl.BlockSpec((1, PAD_HEADS), whole),
                  pl.BlockSpec((1, PAD_HEADS), whole),
                  pl.BlockSpec((1, SSD_D_INNER), whole),
                  pl.BlockSpec((1, SSD_D_INNER), whole)],
        out_specs=[pl.BlockSpec((Lc, SSD_D_INNER), tok),
                   pl.BlockSpec((1, SSD_D_STATE, SSD_D_INNER), per_b),
                   pl.BlockSpec((1, SUBLANES, SSD_CONV_DIM), per_b)],
        out_shape=[jax.ShapeDtypeStruct((T, SSD_D_INNER), BF16),
                   jax.ShapeDtypeStruct((B, SSD_D_STATE, SSD_D_INNER), F32),
                   jax.ShapeDtypeStruct((B, SUBLANES, SSD_CONV_DIM), F32)],
        scratch_shapes=[pltpu.VMEM((SUBLANES + Lc, SSD_CONV_DIM), F32),
                        pltpu.VMEM((SSD_D_STATE, SSD_D_INNER), F32),
                        pltpu.VMEM((Lc, SSD_CONV_DIM), F32),
                        pltpu.VMEM((Lc, SSD_D_INNER), F32)],
        compiler_params=_params(("arbitrary", "arbitrary")),
        name="ssd_scan",
    )(z, xbc, dt, conv_prev8, h0_t, p["conv_w"], p["conv_b"], p["dt_bias"], p["a_log"],
      p["d_skip"], p["g_ssd"])
    return y, h_t, conv8


def _bias_kernel(idx_ref, tab_ref, o_ref):
    h = pl.program_id(0)
    idx = idx_ref[...]
    acc = jnp.full(idx.shape, NEG_INF, F32)
    for b in range(N_BUCKETS):
        acc = jnp.where(idx == b, tab_ref[b, h], acc)
    o_ref[0] = acc


def _bias_from_table(idx, table):
    Q, K = idx.shape
    return pl.pallas_call(
        _bias_kernel,
        grid=(SWA_N_HEADS,),
        in_specs=[pl.BlockSpec((Q, K), lambda h: (0, 0)),
                  pl.BlockSpec(memory_space=pltpu.SMEM)],
        out_specs=pl.BlockSpec((1, Q, K), lambda h: (h, 0, 0)),
        out_shape=jax.ShapeDtypeStruct((SWA_N_HEADS, Q, K), F32),
        compiler_params=_params(("arbitrary",)),
        name="rel_bias",
    )(jnp.asarray(idx, jnp.int32), table.astype(F32))


def _t5_bucket_np(rel):
    nb = N_BUCKETS // 2
    max_exact = nb // 2
    ret = np.where(rel > 0, nb, 0)
    n = np.abs(rel)
    nf = np.maximum(n, 1).astype(np.float32)
    large = max_exact + (np.log(nf / np.float32(max_exact)) / np.float32(math.log(MAX_DISTANCE / max_exact))
                         * np.float32(nb - max_exact)).astype(np.int32)
    large = np.minimum(large, nb - 1)
    return (ret + np.where(n < max_exact, n, large)).astype(np.int32)


def _bucket_map(q_pos, k_pos):
    qc, kc = q_pos // CHUNK, k_pos // CHUNK
    valid = (kc[None, :] >= qc[:, None] - WINDOW_CHUNKS) & (kc[None, :] <= qc[:, None])
    return np.where(valid, _t5_bucket_np(k_pos[None, :] - q_pos[:, None]), -1).astype(np.int32)


def _swa_kernel(q_ref, kp_ref, vp_ref, kc_ref, vc_ref, bias_ref, sink_ref, o_ref, s_scr, p_scr, t_scr):
    TQ = q_ref.shape[0]
    NP = kp_ref.shape[0]
    NK = NP + kc_ref.shape[0]
    HD = SWA_HEAD_DIM
    PW = 2 * HD
    pairs = SWA_REP // 2
    ndims = (((1,), (1,)), ((), ()))
    first_o = lax.broadcasted_iota(jnp.int32, (TQ, PW), 1) < HD
    first_s = lax.broadcasted_iota(jnp.int32, (TQ, 2 * NK), 1) < NK
    zero = jnp.zeros((NK, HD), BF16)
    one = jnp.ones((NK, HD), BF16)
    for g in range(SWA_N_KV):
        ks = slice(g * HD, (g + 1) * HD)
        k_g = jnp.concatenate([kp_ref[:, ks], kc_ref[:, ks]], axis=0).astype(BF16)
        v_g = jnp.concatenate([vp_ref[:, ks], vc_ref[:, ks]], axis=0).astype(BF16)
        kk = jnp.concatenate([jnp.concatenate([k_g, zero], axis=1),
                              jnp.concatenate([zero, k_g], axis=1)], axis=0)
        vv = jnp.concatenate([jnp.concatenate([v_g, zero, one, zero], axis=1),
                              jnp.concatenate([zero, v_g, zero, one], axis=1)], axis=0)
        for pr in range(pairs):
            pidx = g * pairs + pr
            s_scr[pr * TQ:(pr + 1) * TQ, :] = (
                lax.dot_general(q_ref[:, pidx * PW:(pidx + 1) * PW], kk, ndims, preferred_element_type=F32)
                + bias_ref[0, pidx])
        for pr in range(pairs):
            pidx = g * pairs + pr
            rows = slice(pr * TQ, (pr + 1) * TQ)
            s = s_scr[rows, :]
            sink_a = sink_ref[2 * pidx]
            sink_b = sink_ref[2 * pidx + 1]
            ma = jnp.maximum(jnp.max(s[:, :NK], axis=-1, keepdims=True), sink_a)
            mb = jnp.maximum(jnp.max(s[:, NK:], axis=-1, keepdims=True), sink_b)
            p_scr[rows, :] = jnp.exp(s - jnp.where(first_s, ma, mb)).astype(BF16)
            t_scr[rows, :] = jnp.where(first_o, jnp.exp(sink_a - ma), jnp.exp(sink_b - mb))
        for pr in range(pairs):
            pidx = g * pairs + pr
            rows = slice(pr * TQ, (pr + 1) * TQ)
            ov = jnp.dot(p_scr[rows, :], vv, preferred_element_type=F32)
            o = ov[:, :PW] / (ov[:, PW:] + t_scr[rows, :])
            o_ref[:, pidx * PW:(pidx + 1) * PW] = o.astype(o_ref.dtype)


def _pair_bias(bias_prev, bias_cur):
    full = jnp.concatenate([bias_prev, bias_cur], axis=-1)
    H, Q, NK = full.shape
    return jnp.transpose(full.reshape(H // 2, 2, Q, NK), (0, 2, 1, 3)).reshape(H // 2, Q, 2 * NK)


def _swa(q, k_prev_arr, v_prev_arr, prev_map, k_cur_arr, v_cur_arr, cur_map, n_cur, bias, sinks,
         B, L, TQ):
    T = B * L
    nblk = L // TQ
    kvw = SWA_N_KV * SWA_HEAD_DIM
    n_var = bias.shape[0]
    n_keys = bias.shape[3] // 2
    n_prev = n_keys - n_cur
    pairs = SWA_REP // 2
    return pl.pallas_call(
        _swa_kernel,
        grid=(B, nblk),
        in_specs=[pl.BlockSpec((TQ, SWA_N_HEADS * SWA_HEAD_DIM), lambda b, i: (b * nblk + i, 0)),
                  pl.BlockSpec((n_prev, kvw), prev_map[0]),
                  pl.BlockSpec((n_prev, kvw), prev_map[1]),
                  pl.BlockSpec((n_cur, kvw), cur_map[0]),
                  pl.BlockSpec((n_cur, kvw), cur_map[1]),
                  pl.BlockSpec((1,) + bias.shape[1:], lambda b, i: (jnp.where(i == 0, n_var - 1, 0), 0, 0, 0)),
                  pl.BlockSpec(memory_space=pltpu.SMEM)],
        out_specs=pl.BlockSpec((TQ, SWA_N_HEADS * SWA_HEAD_DIM), lambda b, i: (b * nblk + i, 0)),
        out_shape=jax.ShapeDtypeStruct((T, SWA_N_HEADS * SWA_HEAD_DIM), BF16),
        scratch_shapes=[pltpu.VMEM((pairs * TQ, 2 * n_keys), F32),
                        pltpu.VMEM((pairs * TQ, 2 * n_keys), BF16),
                        pltpu.VMEM((pairs * TQ, 2 * SWA_HEAD_DIM), F32)],
        compiler_params=_params(("parallel", "arbitrary")),
        name="swa_attention",
    )(q, k_prev_arr, v_prev_arr, k_cur_arr, v_cur_arr, bias, sinks.astype(F32))


def _mem_kernel(q_ref, mk_ref, mv_ref, o_ref):
    scale = MEM_HEAD_DIM ** -0.5
    ndims = (((1,), (1,)), ((), ()))
    for h in range(MEM_N_HEADS):
        hs = slice(h * MEM_HEAD_DIM, (h + 1) * MEM_HEAD_DIM)
        s = lax.dot_general(q_ref[:, hs], mk_ref[:, hs], ndims, preferred_element_type=F32) * scale
        m = jnp.max(s, axis=-1, keepdims=True)
        p = jnp.exp(s - m)
        den = jnp.sum(p, axis=-1, keepdims=True)
        o = jnp.dot(p.astype(BF16), mv_ref[:, hs], preferred_element_type=F32)
        o_ref[:, hs] = (o / den).astype(o_ref.dtype)


def _mem_attend(q, mk, mv, L, tm):
    T, W = q.shape
    M = mk.shape[0] // (T // L)
    return pl.pallas_call(
        _mem_kernel,
        grid=(T // tm,),
        in_specs=[pl.BlockSpec((tm, W), lambda i: (i, 0)),
                  pl.BlockSpec((M, W), lambda i: ((i * tm) // L, 0)),
                  pl.BlockSpec((M, W), lambda i: ((i * tm) // L, 0))],
        out_specs=pl.BlockSpec((tm, W), lambda i: (i, 0)),
        out_shape=jax.ShapeDtypeStruct((T, W), BF16),
        compiler_params=_params(("parallel",)),
        name="mem_attention",
    )(q, mk, mv)


def _merge_kernel(ys_ref, os_ref, om_ref, w1_ref, w2_ref, w3_ref, g0_ref, g1_ref, g2_ref, o_ref):
    a = jnp.dot(ys_ref[...], w1_ref[...], preferred_element_type=F32)
    b = jnp.dot(os_ref[...], w2_ref[...], preferred_element_type=F32)
    c = jnp.dot(om_ref[...], w3_ref[...], preferred_element_type=F32)
    o = (g0_ref[...].astype(F32) * a + g1_ref[...].astype(F32) * b + g2_ref[...].astype(F32) * c)
    o_ref[...] = o.astype(o_ref.dtype)


def _merge(y_ssd, o_s, o_m, w1, w2, w3, gates, tm, tn):
    T = y_ssd.shape[0]
    D = w1.shape[1]
    nj = D // tn
    row = lambda i, j: (i, 0)
    colw = lambda i, j: (0, j)
    return pl.pallas_call(
        _merge_kernel,
        grid=(T // tm, nj),
        in_specs=[pl.BlockSpec((tm, y_ssd.shape[1]), row),
                  pl.BlockSpec((tm, o_s.shape[1]), row),
                  pl.BlockSpec((tm, o_m.shape[1]), row),
                  pl.BlockSpec((w1.shape[0], tn), colw),
                  pl.BlockSpec((w2.shape[0], tn), colw),
                  pl.BlockSpec((w3.shape[0], tn), colw),
                  pl.BlockSpec((tm, tn), lambda i, j: (i, j)),
                  pl.BlockSpec((tm, tn), lambda i, j: (i, j + nj)),
                  pl.BlockSpec((tm, tn), lambda i, j: (i, j + 2 * nj))],
        out_specs=pl.BlockSpec((tm, tn), lambda i, j: (i, j)),
        out_shape=jax.ShapeDtypeStruct((T, D), BF16),
        compiler_params=_params(("parallel", "arbitrary")),
        name="gated_merge",
    )(y_ssd, o_s, o_m, w1, w2, w3, gates, gates, gates)


def _norm_route_kernel(xa_ref, xb_ref, g_ref, whi_ref, wlo_ref, h_ref, r_ref, *, n_a):
    x = jnp.where(pl.program_id(0) < n_a, xa_ref[...], xb_ref[...])
    ms = jnp.mean(x * x, axis=-1, keepdims=True)
    h = x * lax.rsqrt(ms + EPS) * g_ref[...]
    h_ref[:, 0, :] = h
    hb = h.astype(BF16)
    lo = (h - hb.astype(F32)).astype(BF16)
    lg = (jnp.dot(hb, whi_ref[...], preferred_element_type=F32)
          + jnp.dot(lo, whi_ref[...], preferred_element_type=F32)
          + jnp.dot(hb, wlo_ref[...], preferred_element_type=F32))
    lane = lax.broadcasted_iota(jnp.int32, lg.shape, 1)
    lane_f = lane.astype(F32)
    far = float(LANES)
    gl = jnp.where((lane >= N_EXPERTS) & (lane < N_EXPERTS + N_EXPERT_GROUPS), lg, NEG_INF)
    gmax = jnp.max(gl, axis=-1, keepdims=True)
    gidx = jnp.min(jnp.where(gl == gmax, lane_f - N_EXPERTS, far), axis=-1, keepdims=True)
    gw = 1.0 / jnp.sum(jnp.exp(gl - gmax), axis=-1, keepdims=True)
    lo_e = gidx * EXPERTS_PER_GROUP
    el = jnp.where((lane_f >= lo_e) & (lane_f < lo_e + EXPERTS_PER_GROUP), lg, NEG_INF)
    v1 = jnp.max(el, axis=-1, keepdims=True)
    i1 = jnp.min(jnp.where(el == v1, lane_f, far), axis=-1, keepdims=True)
    el2 = jnp.where(lane_f == i1, NEG_INF, el)
    v2 = jnp.max(el2, axis=-1, keepdims=True)
    i2 = jnp.min(jnp.where(el2 == v2, lane_f, far), axis=-1, keepdims=True)
    e = jnp.exp(v2 - v1)
    w1 = gw / (1.0 + e)
    w2 = gw * e / (1.0 + e)
    r_ref[...] = jnp.where(lane == 0, w1, jnp.where(lane == 1, w2,
                           jnp.where(lane == 2, i1, jnp.where(lane == 3, i2, 0.0))))


def _norm_route(xa, xb, g, w_hi, w_lo, tm):
    Ta, D = xa.shape
    Tb = xb.shape[0]
    T = Ta + Tb
    n_a, n_b = Ta // tm, Tb // tm
    assert n_a * tm == Ta and n_b * tm == Tb
    return pl.pallas_call(
        functools.partial(_norm_route_kernel, n_a=n_a),
        grid=(n_a + n_b,),
        in_specs=[pl.BlockSpec((tm, D), lambda i: (jnp.minimum(i, n_a - 1), 0)),
                  pl.BlockSpec((tm, D), lambda i: (jnp.maximum(i - n_a, 0), 0)),
                  pl.BlockSpec((1, D), lambda i: (0, 0)),
                  pl.BlockSpec((D, LANES), lambda i: (0, 0)),
                  pl.BlockSpec((D, LANES), lambda i: (0, 0))],
        out_specs=[pl.BlockSpec((tm, 1, D), lambda i: (i, 0, 0)),
                   pl.BlockSpec((tm, LANES), lambda i: (i, 0))],
        out_shape=[jax.ShapeDtypeStruct((T, 1, D), F32),
                   jax.ShapeDtypeStruct((T, LANES), F32)],
        compiler_params=_params(("parallel",)),
        name="ffn_norm_router",
    )(xa, xb, g.reshape(1, D).astype(F32), w_hi, w_lo)


def _moe_kernel(be_ref, nv_ref, s0_ref, ord_ref, h_hbm, w1_ref, w3_ref, w2_ref, yu_hbm,
                xbuf, ybuf, w1b, w3b, w2b, sem_in, sem_out, *, n_tok):
    i = pl.program_id(0)
    nb = pl.num_programs(0)
    rows = xbuf.shape[1]
    n_assign = TOP_K * n_tok
    nv = nv_ref[i]
    slot = i % 2
    nxt = jnp.minimum(i + 1, nb - 1)
    next_valid = (i + 1 < nb) & (nv_ref[nxt] > 0)

    def assignment(blk, r):
        return ord_ref[s0_ref[blk] + r]

    def row_in(s, r, tok):
        return pltpu.make_async_copy(h_hbm.at[tok], xbuf.at[s, pl.ds(r, 1), :], sem_in.at[s])

    def row_out(s, r, dst):
        return pltpu.make_async_copy(ybuf.at[s, pl.ds(r, 1), :], yu_hbm.at[dst], sem_out.at[s])

    def gather_start(blk, s):
        for r in range(rows):
            row_in(s, r, lax.shift_right_logical(assignment(blk, r), 1)).start()

    def gather_wait(s):
        for r in range(rows):
            row_in(s, r, 0).wait()

    def scatter_start(blk, s):
        n_valid = nv_ref[blk]
        for r in range(rows):
            m = assignment(blk, r)
            dst = jnp.where(r < n_valid, (m & 1) * n_tok + lax.shift_right_logical(m, 1),
                            n_assign + s * rows + r)
            row_out(s, r, dst).start()

    def scatter_wait(s):
        for r in range(rows):
            row_out(s, r, 0).wait()

    @pl.when(i == 0)
    def _clear_spare_rows():
        ybuf[...] = jnp.zeros(ybuf.shape, ybuf.dtype)
        for s in range(2):
            for r in range(rows):
                row_out(s, r, n_assign + s * rows + r).start()
            scatter_wait(s)

    @pl.when((i == 0) & (nv > 0))
    def _prologue():
        gather_start(i, 0)

    @pl.when(next_valid)
    def _prefetch():
        gather_start(nxt, 1 - slot)

    prev = be_ref[jnp.maximum(i - 1, 0)]

    @pl.when((i == 0) | (be_ref[i] != prev))
    def _load_expert():
        w1b[...] = w1_ref[0].astype(BF16)
        w3b[...] = w3_ref[0].astype(BF16)
        w2b[...] = w2_ref[0].astype(BF16)

    @pl.when(nv > 0)
    def _compute():
        gather_wait(slot)

        @pl.when(i >= 2)
        def _free_ybuf():
            scatter_wait(slot)

        x = xbuf[slot].astype(BF16)
        a = jnp.dot(x, w1b[...], preferred_element_type=F32)
        b = jnp.dot(x, w3b[...], preferred_element_type=F32)
        mid = (_silu(a) * b).astype(BF16)
        y = jnp.dot(mid, w2b[...], preferred_element_type=F32)
        ybuf[slot] = y
        scatter_start(i, slot)

        @pl.when(jnp.logical_not(next_valid))
        def _drain():
            scatter_wait(slot)

            @pl.when(i >= 1)
            def _drain_prev():
                scatter_wait(1 - slot)


def _moe(h2, route, w_gate, w_up, w_down):
    T, _, D = h2.shape
    F = w_gate.shape[2]
    assert TOP_K == 2
    M = T * TOP_K
    BM = MOE_ROWS
    nb = (M + N_EXPERTS * (BM - 1) + BM - 1) // BM
    e_flat = route[:, TOP_K:2 * TOP_K].astype(jnp.int32).reshape(M)
    idx_bits = (M - 1).bit_length()
    assert (N_EXPERTS << idx_bits) < 2 ** 31
    packed = lax.sort(e_flat * (1 << idx_bits) + jnp.arange(M, dtype=jnp.int32), is_stable=False)
    order = jnp.pad(packed & ((1 << idx_bits) - 1), (0, BM))
    experts = jnp.arange(N_EXPERTS, dtype=jnp.int32)
    counts = jnp.sum((experts[:, None] == e_flat[None, :]).astype(jnp.int32), axis=1)
    padded = (counts + BM - 1) // BM * BM
    pad_end = jnp.cumsum(padded)
    pad_start = pad_end - padded
    start = jnp.cumsum(counts) - counts
    blk0 = jnp.arange(nb, dtype=jnp.int32) * BM
    blk_exp = jnp.minimum(jnp.sum((pad_end[None, :] <= blk0[:, None]).astype(jnp.int32), axis=1), N_EXPERTS - 1)
    pick = (blk_exp[:, None] == experts[None, :]).astype(jnp.int32)
    off0 = blk0 - jnp.sum(pick * pad_start[None, :], axis=1)
    blk_nv = jnp.clip(jnp.sum(pick * counts[None, :], axis=1) - off0, 0, BM).astype(jnp.int32)
    blk_s0 = jnp.clip(jnp.sum(pick * start[None, :], axis=1) + off0, 0, M - 1).astype(jnp.int32)

    grid_spec = pltpu.PrefetchScalarGridSpec(
        num_scalar_prefetch=4,
        grid=(nb,),
        in_specs=[pl.BlockSpec(memory_space=pl.ANY),
                  pl.BlockSpec((1, D, F), lambda i, be, nv, s0, od: (be[i], 0, 0)),
                  pl.BlockSpec((1, D, F), lambda i, be, nv, s0, od: (be[i], 0, 0)),
                  pl.BlockSpec((1, F, D), lambda i, be, nv, s0, od: (be[i], 0, 0))],
        out_specs=pl.BlockSpec(memory_space=pl.ANY),
        scratch_shapes=[pltpu.VMEM((2, BM, D), F32),
                        pltpu.VMEM((2, BM, D), F32),
                        pltpu.VMEM((D, F), BF16),
                        pltpu.VMEM((D, F), BF16),
                        pltpu.VMEM((F, D), BF16),
                        pltpu.SemaphoreType.DMA((2,)),
                        pltpu.SemaphoreType.DMA((2,))],
    )
    return pl.pallas_call(
        functools.partial(_moe_kernel, n_tok=T),
        grid_spec=grid_spec,
        out_shape=jax.ShapeDtypeStruct((M + 2 * BM, 1, D), F32),
        compiler_params=_params(("arbitrary",)),
        name="moe_experts",
    )(blk_exp.astype(jnp.int32), blk_nv, blk_s0, order, h2, w_gate, w_up, w_down)


def _combine_kernel(x_ref, r_ref, y0_ref, y1_ref, o_ref):
    w0 = r_ref[:, 0:1]
    w1 = r_ref[:, 1:2]
    o_ref[...] = x_ref[...] + (w0 * y0_ref[:, 0, :] + w1 * y1_ref[:, 0, :])


def _combine(x, route, yu, blk0, tm):
    T, D = x.shape
    n_all = route.shape[0] // tm
    return pl.pallas_call(
        _combine_kernel,
        grid=(T // tm,),
        in_specs=[pl.BlockSpec((tm, D), lambda i: (i, 0)),
                  pl.BlockSpec((tm, LANES), lambda i: (blk0 + i, 0)),
                  pl.BlockSpec((tm, 1, D), lambda i: (blk0 + i, 0, 0)),
                  pl.BlockSpec((tm, 1, D), lambda i: (n_all + blk0 + i, 0, 0))],
        out_specs=pl.BlockSpec((tm, D), lambda i: (i, 0)),
        out_shape=jax.ShapeDtypeStruct((T, D), F32),
        compiler_params=_params(("parallel",)),
        name="moe_combine",
    )(x, route, yu, yu)


def _layer(x, B, L, p, conv_prev8, h0_t, mk, mv, swa_prev, swa_bias, Lc, TQ):
    T, D = x.shape
    tmm = min(MM_ROWS, T)
    z, h = _norm_matmul(x, p["g_mix"], p["wz"], BF16, tmm, MM_COLS, name="norm_proj_z")
    xbc = _matmul(h, p["wxbc"], BF16, tmm, MM_COLS_WIDE, name="proj_xbc")
    q_s = _matmul(h, p["wqs"], BF16, tmm, MM_COLS, epi=_epi_group_norm, col_extras=(p["g_q_swa"],),
                  const_extras=(p["mavg"],), name="proj_q_swa")
    kvw = SWA_N_KV * SWA_HEAD_DIM
    kvd = _matmul(h, p["wkvdt"], F32, tmm, 2 * kvw + PAD_HEADS, epi=_epi_kv_dt, const_extras=(p["g_k"], p["mavg"]),
                  name="proj_kv_dt")
    kv = kvd
    q_m = _matmul(h, p["wqm"], BF16, tmm, MM_COLS, epi=_epi_row_norm, col_extras=(p["g_q_mem"],),
                  name="proj_q_mem")
    gates = _matmul(h, p["wg"], BF16, tmm, MM_COLS_WIDE, epi=_epi_sigmoid, name="proj_gates")

    y_ssd, h_t, conv8 = _ssd(z, xbc, kvd, (2 * kvw) // PAD_HEADS, conv_prev8, h0_t, p, B, L, Lc)

    nblk = L // TQ
    if swa_prev is None:
        prev_map = (lambda b, i: (b * nblk + jnp.maximum(i - 1, 0), 0),
                    lambda b, i: (b * nblk + jnp.maximum(i - 1, 0), 1))
        cur_map = (lambda b, i: (b * nblk + i, 0), lambda b, i: (b * nblk + i, 1))
        o_s = _swa(q_s, kv, kv, prev_map, kv, kv, cur_map, TQ, swa_bias, p["sinks"], B, L, TQ)
    else:
        n_cur = swa_bias.shape[3] // 2 - swa_prev[0].shape[0] // B
        kv3 = jnp.pad(kv[:, :2 * kvw].reshape(B, L, 2 * kvw), ((0, 0), (0, n_cur - L), (0, 0)))
        k_new = kv3[:, :, :kvw].reshape(B * n_cur, kvw)
        v_new = kv3[:, :, kvw:].reshape(B * n_cur, kvw)
        per_b = (lambda b, i: (b, 0), lambda b, i: (b, 0))
        o_s = _swa(q_s, swa_prev[0], swa_prev[1], per_b, k_new, v_new, per_b, n_cur, swa_bias, p["sinks"],
                   B, L, TQ)

    o_m = _mem_attend(q_m, mk, mv, L, min(256, L))

    merged = _merge(y_ssd, o_s, o_m, p["w_o_ssd"], p["w_o_swa"], p["w_o_mem"], gates, tmm, MERGE_COLS)
    x1 = _matmul(merged, p["w_out"], F32, tmm, MM_COLS, epi=_epi_residual, tile_extras=(x,), name="proj_out")

    return x1, conv8, h_t, kv


def _state_to_heads(h_t, B):
    return jnp.transpose(h_t.reshape(B, SSD_D_STATE, SSD_N_HEADS, SSD_HEAD_DIM), (0, 2, 3, 1))


def kernel(x_prompt, x_sample, cache_conv, state_ssd, cache_swa_k, cache_swa_v, cache_mem_k, cache_mem_v, mem_prompt, rel_bias_table, g_mix, w_in, conv_w, conv_b, dt_bias, a_log, d_skip, g_ssd, w_o_ssd, g_q_swa, g_k_swa, sinks, w_o_swa, g_mem, w_mem_k, w_mem_v, g_q_mem, g_k_mem, w_o_mem, w_out, g_ffn, w_router_grp, w_router_exp, w_exp_gate, w_exp_up, w_exp_down):
    B, S, D = x_prompt.shape
    Bd, Sd, _ = x_sample.shape
    depth = w_in.shape[0]
    assert depth == 1
    l = 0
    kvw = SWA_N_KV * SWA_HEAD_DIM
    qw = SWA_N_HEADS * SWA_HEAD_DIM
    mw = MEM_N_HEADS * MEM_HEAD_DIM

    sizes = (SSD_D_INNER, SSD_CONV_DIM, SSD_N_HEADS, qw, kvw, kvw, mw, 3 * D)
    offs = np.concatenate([[0], np.cumsum(sizes)])
    w = w_in[l]
    cols = [w[:, int(offs[k]):int(offs[k + 1])] for k in range(len(sizes))]
    pad_h = PAD_HEADS - SSD_N_HEADS
    w_r = jnp.pad(jnp.concatenate([w_router_exp[l], w_router_grp[l]], axis=1),
                  ((0, 0), (0, LANES - N_EXPERTS - N_EXPERT_GROUPS)))
    w_r_hi = w_r.astype(BF16)
    p = {
        "g_mix": g_mix[l],
        "wz": cols[0].astype(BF16),
        "wxbc": cols[1].astype(BF16),
        "wkvdt": jnp.concatenate([cols[4], cols[5], jnp.pad(cols[2], ((0, 0), (0, pad_h)))], axis=1).astype(BF16),
        "wqs": cols[3].astype(BF16),
        "wqm": cols[6].astype(BF16),
        "wg": cols[7].astype(BF16),
        "conv_w": conv_w[l].astype(F32),
        "conv_b": conv_b[l].reshape(1, SSD_CONV_DIM).astype(F32),
        "dt_bias": jnp.pad(dt_bias[l], (0, pad_h)).reshape(1, PAD_HEADS).astype(F32),
        "a_log": jnp.pad(a_log[l], (0, pad_h)).reshape(1, PAD_HEADS).astype(F32),
        "d_skip": jnp.repeat(d_skip[l], SSD_HEAD_DIM).reshape(1, SSD_D_INNER).astype(F32),
        "g_ssd": g_ssd[l].reshape(1, SSD_D_INNER).astype(F32),
        "g_q_swa": (jnp.tile(g_q_swa[l], SWA_N_HEADS) * SWA_HEAD_DIM ** -0.5).reshape(1, qw).astype(F32),
        "g_k": jnp.tile(g_k_swa[l], SWA_N_KV).reshape(1, kvw).astype(F32),
        "g_q_mem": jnp.tile(g_q_mem[l], MEM_N_HEADS).reshape(1, mw).astype(F32),
        "mavg": _group_mean_matrix(256, SWA_HEAD_DIM),
        "sinks": sinks[l],
        "w_o_ssd": w_o_ssd[l].astype(BF16),
        "w_o_swa": w_o_swa[l].astype(BF16),
        "w_o_mem": w_o_mem[l].astype(BF16),
        "w_out": w_out[l].astype(BF16),
        "g_ffn": g_ffn[l],
        "w_r_hi": w_r_hi,
        "w_r_lo": (w_r - w_r_hi.astype(F32)).astype(BF16),
        "w_exp_gate": w_exp_gate[l],
        "w_exp_up": w_exp_up[l],
        "w_exp_down": w_exp_down[l],
    }

    M = mem_prompt.shape[1]
    mn = _rmsnorm(mem_prompt.reshape(B * M, D), g_mem[l], min(256, B * M))
    mk_p = _matmul(mn, w_mem_k[l].astype(BF16), F32, min(256, B * M), MEM_HEAD_DIM, epi=_epi_row_norm,
                   col_extras=(jnp.tile(g_k_mem[l], MEM_N_HEADS).reshape(1, mw).astype(F32),), name="mem_k")
    mv_p = _matmul(mn, w_mem_v[l].astype(BF16), F32, min(256, B * M), MEM_HEAD_DIM, name="mem_v")

    TQ = 2 * CHUNK
    qpos = np.arange(TQ)
    prev_p = _bias_from_table(_bucket_map(qpos, np.arange(TQ) - TQ), rel_bias_table)
    cur_p = _bias_from_table(_bucket_map(qpos, np.arange(TQ)), rel_bias_table)
    bias_p = jnp.stack([_pair_bias(prev_p, cur_p), _pair_bias(jnp.full_like(prev_p, NEG_INF), cur_p)])
    C = cache_swa_k.shape[2]
    qpos_s = PAST_LEN + np.arange(Sd)
    cur_map_s = _bucket_map(qpos_s, PAST_LEN + np.arange(C))
    cur_map_s[:, Sd:] = -1
    bias_s = _pair_bias(_bias_from_table(_bucket_map(qpos_s, PAST_LEN - C + np.arange(C)), rel_bias_table),
                        _bias_from_table(cur_map_s, rel_bias_table))[None]

    conv0 = jnp.zeros((B, SUBLANES, SSD_CONV_DIM), F32)
    h0 = jnp.zeros((B, SSD_D_STATE, SSD_D_INNER), F32)
    x1_p, conv8_p, ht_p, kv_p = _layer(x_prompt.reshape(B * S, D), B, S, p, conv0, h0,
                                     mk_p.astype(BF16), mv_p.astype(BF16), None,
                                     bias_p, CHUNK, TQ)
    conv_prev = jnp.pad(cache_conv[l], ((0, 0), (SUBLANES - (SSD_CONV - 1), 0), (0, 0)))
    h0_s = jnp.transpose(state_ssd[l], (0, 3, 1, 2)).reshape(Bd, SSD_D_STATE, SSD_D_INNER)
    x1_s, conv8_s, ht_s, kv_s = _layer(x_sample.reshape(Bd * Sd, D), Bd, Sd, p, conv_prev, h0_s,
                                     cache_mem_k[l].reshape(Bd * M, mw).astype(BF16),
                                     cache_mem_v[l].reshape(Bd * M, mw).astype(BF16),
                                     (cache_swa_k[l].reshape(Bd * C, kvw), cache_swa_v[l].reshape(Bd * C, kvw)),
                                     bias_s, Sd, Sd)

    tr = math.gcd(B * S, Bd * Sd, MOE_ROWS)
    h2, route = _norm_route(x1_p, x1_s, p["g_ffn"], p["w_r_hi"], p["w_r_lo"], tr)
    yu = _moe(h2, route, p["w_exp_gate"], p["w_exp_up"], p["w_exp_down"])
    yp = _combine(x1_p, route, yu, 0, tr)
    ys = _combine(x1_s, route, yu, (B * S) // tr, tr)

    keep = min(WINDOW, S)
    kv_p = kv_p.reshape(B, S, -1)[:, S - keep:, :2 * kvw].reshape(B, keep, 2, SWA_N_KV, SWA_HEAD_DIM)
    kv_s = kv_s[:, :2 * kvw].reshape(Bd, Sd, 2, SWA_N_KV, SWA_HEAD_DIM)
    tail = SUBLANES - (SSD_CONV - 1)
    return (yp.reshape(B, S, D), ys.reshape(Bd, Sd, D),
            conv8_p[None, :, tail:], _state_to_heads(ht_p, B)[None],
            kv_p[None, :, :, 0], kv_p[None, :, :, 1],
            mk_p.reshape(1, B, M, MEM_N_HEADS, MEM_HEAD_DIM), mv_p.reshape(1, B, M, MEM_N_HEADS, MEM_HEAD_DIM),
            conv8_s[None, :, tail:], _state_to_heads(ht_s, Bd)[None],
            kv_s[None, :, :, 0], kv_s[None, :, :, 1])
```

```python
import functools
import math

import numpy as np
import jax
import jax.numpy as jnp
from jax import lax
from jax.experimental import pallas as pl
from jax.experimental.pallas import tpu as pltpu

F32 = jnp.float32
BF16 = jnp.bfloat16
EPS = 1e-6
NEG_INF = float("-inf")

CHUNK = 64
SSD_HEAD_DIM = 64
SSD_N_HEADS = 64
SSD_N_GROUPS = 8
SSD_D_STATE = 128
SSD_D_INNER = SSD_N_HEADS * SSD_HEAD_DIM
SSD_GN = SSD_N_GROUPS * SSD_D_STATE
SSD_CONV_DIM = SSD_D_INNER + 2 * SSD_GN
SSD_CONV = 4
SWA_N_HEADS = 32
SWA_N_KV = 4
SWA_HEAD_DIM = 64
SWA_REP = SWA_N_HEADS // SWA_N_KV
WINDOW = 128
WINDOW_CHUNKS = WINDOW // CHUNK
MEM_N_HEADS = 4
MEM_HEAD_DIM = 512
N_BUCKETS = 32
MAX_DISTANCE = 128
N_EXPERT_GROUPS = 4
EXPERTS_PER_GROUP = 16
N_EXPERTS = N_EXPERT_GROUPS * EXPERTS_PER_GROUP
TOP_K = 2
PAST_LEN = 4096

LANES = 128
SUBLANES = 8
VMEM_LIMIT = 56 * 1024 * 1024
MOE_ROWS = 256
MM_ROWS = 1024
MM_COLS = 1024
MM_COLS_WIDE = 2048
MERGE_COLS = 256
PAD_HEADS = LANES


def _params(sem, vmem=VMEM_LIMIT):
    return pltpu.CompilerParams(dimension_semantics=sem, vmem_limit_bytes=vmem)


def _sigmoid(x):
    return 0.5 * (jnp.tanh(0.5 * x) + 1.0)


def _silu(x):
    u = 0.5 * x
    return u * (jnp.tanh(u) + 1.0)


def _split3(x):
    x1 = x.astype(BF16)
    r1 = x - x1.astype(F32)
    x2 = r1.astype(BF16)
    x3 = (r1 - x2.astype(F32)).astype(BF16)
    return x1, x2, x3


def _rmsnorm_kernel(x_ref, g_ref, o_ref):
    x = x_ref[...]
    ms = jnp.mean(x * x, axis=-1, keepdims=True)
    o_ref[...] = (x * lax.rsqrt(ms + EPS) * g_ref[...]).astype(o_ref.dtype)


def _rmsnorm(x, g, tm):
    T, D = x.shape
    return pl.pallas_call(
        _rmsnorm_kernel,
        grid=(T // tm,),
        in_specs=[pl.BlockSpec((tm, D), lambda i: (i, 0)),
                  pl.BlockSpec((1, D), lambda i: (0, 0))],
        out_specs=pl.BlockSpec((tm, D), lambda i: (i, 0)),
        out_shape=jax.ShapeDtypeStruct((T, D), BF16),
        compiler_params=_params(("parallel",)),
        name="rmsnorm",
    )(x, g.reshape(1, D).astype(F32))


def _mm_kernel(a_ref, b_ref, *refs, epi):
    o_ref = refs[-1]
    acc = jnp.dot(a_ref[...], b_ref[...], preferred_element_type=F32)
    if epi is not None:
        acc = epi(acc, *[r[...] for r in refs[:-1]])
    o_ref[...] = acc.astype(o_ref.dtype)


def _matmul(a, b, out_dtype, tm, tn, epi=None, col_extras=(), tile_extras=(), const_extras=(), name="matmul"):
    M, K = a.shape
    N = b.shape[1]
    assert M % tm == 0 and N % tn == 0, (M, N, tm, tn)
    in_specs = [pl.BlockSpec((tm, K), lambda i, j: (i, 0)),
                pl.BlockSpec((K, tn), lambda i, j: (0, j))]
    for _ in col_extras:
        in_specs.append(pl.BlockSpec((1, tn), lambda i, j: (0, j)))
    for _ in tile_extras:
        in_specs.append(pl.BlockSpec((tm, tn), lambda i, j: (i, j)))
    for c in const_extras:
        in_specs.append(pl.BlockSpec(c.shape, lambda i, j: (0, 0)))
    return pl.pallas_call(
        functools.partial(_mm_kernel, epi=epi),
        grid=(M // tm, N // tn),
        in_specs=in_specs,
        out_specs=pl.BlockSpec((tm, tn), lambda i, j: (i, j)),
        out_shape=jax.ShapeDtypeStruct((M, N), out_dtype),
        compiler_params=_params(("parallel", "arbitrary")),
        name=name,
    )(a, b, *col_extras, *tile_extras, *const_extras)


def _group_mean_matrix(width, group):
    idx = np.arange(width) // group
    return jnp.asarray((idx[:, None] == idx[None, :]).astype(np.float32) / group, dtype=BF16)


def _epi_group_norm(acc, gain, mavg):
    w = mavg.shape[0]
    outs = []
    for c in range(acc.shape[1] // w):
        a = acc[:, c * w:(c + 1) * w]
        s = a * a
        hi = s.astype(BF16)
        lo = (s - hi.astype(F32)).astype(BF16)
        ms = (jnp.dot(hi, mavg, preferred_element_type=F32)
              + jnp.dot(lo, mavg, preferred_element_type=F32))
        outs.append(a * lax.rsqrt(ms + EPS))
    normed = outs[0] if len(outs) == 1 else jnp.concatenate(outs, axis=1)
    return normed * gain


def _epi_kv_dt(acc, gain, mavg):
    kw = mavg.shape[0]
    return jnp.concatenate([_epi_group_norm(acc[:, :kw], gain, mavg), acc[:, kw:]], axis=1)


def _epi_row_norm(acc, gain):
    outs = []
    for c in range(acc.shape[1] // MEM_HEAD_DIM):
        a = acc[:, c * MEM_HEAD_DIM:(c + 1) * MEM_HEAD_DIM]
        outs.append(a * lax.rsqrt(jnp.mean(a * a, axis=-1, keepdims=True) + EPS))
    normed = outs[0] if len(outs) == 1 else jnp.concatenate(outs, axis=1)
    return normed * gain


def _epi_sigmoid(acc):
    return _sigmoid(acc)


def _epi_residual(acc, res):
    return acc + res


def _ssd_kernel(z_ref, xbc_ref, dt_ref, cprev_ref, h0_ref, cw_ref, cb_ref, dtb_ref, alog_ref,
                dskip_ref, gn_ref, y_ref, hout_ref, cout_ref, xp_s, h_s, conv_s, y_s, *, Lc):
    c = pl.program_id(1)
    n_chunks = pl.num_programs(1)
    P2 = 2 * SSD_HEAD_DIM
    L2 = 2 * Lc

    @pl.when(c == 0)
    def _init():
        xp_s[0:SUBLANES, :] = cprev_ref[0]
        h_s[...] = h0_ref[0]

    xp_s[SUBLANES:SUBLANES + Lc, :] = xbc_ref[...].astype(F32)
    cblk = 512
    row8 = lax.broadcasted_iota(jnp.int32, (SUBLANES, cblk), 0)
    for j in range(SSD_CONV_DIM // cblk):
        sl = slice(j * cblk, (j + 1) * cblk)
        cur = xp_s[SUBLANES:SUBLANES + Lc, sl]
        tail = xp_s[0:SUBLANES, sl]
        acc = cb_ref[:, sl] + cw_ref[SSD_CONV - 1:SSD_CONV, sl] * cur
        for k in range(SSD_CONV - 1):
            sh = SSD_CONV - 1 - k
            down = pltpu.roll(cur, sh, axis=0)
            top = jnp.where(row8 >= sh, down[0:SUBLANES], pltpu.roll(tail, sh, axis=0))
            acc = acc + cw_ref[k:k + 1, sl] * jnp.concatenate([top, down[SUBLANES:]], axis=0)
        conv_s[:, sl] = _silu(acc)
    xp_s[0:SUBLANES, :] = xp_s[Lc:Lc + SUBLANES, :]

    dtv = dt_ref[...] + dtb_ref[...]
    dt = jnp.maximum(dtv, 0.0) + jnp.log1p(jnp.exp(-jnp.abs(dtv)))
    adt = dt * (-jnp.exp(alog_ref[...]))
    row = lax.broadcasted_iota(jnp.int32, (Lc, Lc), 0)
    col = lax.broadcasted_iota(jnp.int32, (Lc, Lc), 1)
    tri = (col <= row).astype(BF16)
    row2 = lax.broadcasted_iota(jnp.int32, (Lc, L2), 0)
    col2 = lax.broadcasted_iota(jnp.int32, (Lc, L2), 1)
    col2m = jnp.where(col2 >= Lc, col2 - Lc, col2)
    tri_t2 = (row2 <= col2m).astype(BF16)
    causal2 = col2m <= row2
    a1, a2, a3 = _split3(adt)
    acs = (jnp.dot(tri, a1, preferred_element_type=F32)
           + jnp.dot(tri, a2, preferred_element_type=F32)
           + jnp.dot(tri, a3, preferred_element_type=F32))
    tdims = (((0,), (0,)), ((), ()))
    acs_t2 = (lax.dot_general(a1, tri_t2, tdims, preferred_element_type=F32)
              + lax.dot_general(a2, tri_t2, tdims, preferred_element_type=F32)
              + lax.dot_general(a3, tri_t2, tdims, preferred_element_type=F32))

    lane_p = lax.broadcasted_iota(jnp.int32, (Lc, P2), 1)
    first_p = lane_p < SSD_HEAD_DIM
    first_l = col2 < Lc
    first_l1 = first_l[0:1, :]
    ndims = (((1,), (1,)), ((), ()))

    for g in range(SSD_N_GROUPS):
        b_g = conv_s[:, SSD_D_INNER + g * SSD_D_STATE:SSD_D_INNER + (g + 1) * SSD_D_STATE].astype(BF16)
        c_g = conv_s[:, SSD_D_INNER + SSD_GN + g * SSD_D_STATE:
                     SSD_D_INNER + SSD_GN + (g + 1) * SSD_D_STATE].astype(BF16)
        b2 = jnp.concatenate([b_g, b_g], axis=0)
        cb2 = lax.dot_general(c_g, b2, ndims, preferred_element_type=F32)
        gw = SSD_HEAD_DIM * (SSD_N_HEADS // SSD_N_GROUPS)
        inter = jnp.dot(c_g, h_s[:, g * gw:(g + 1) * gw].astype(BF16), preferred_element_type=F32)
        for jj in range(gw // P2):
            j = g * (gw // P2) + jj
            sl = slice(j * P2, (j + 1) * P2)
            acs_a = acs[:, 2 * j:2 * j + 1]
            acs_b = acs[:, 2 * j + 1:2 * j + 2]
            col_l = jnp.where(first_l, acs_a, acs_b)
            row_l = jnp.where(first_l1, acs_t2[2 * j:2 * j + 1, :], acs_t2[2 * j + 1:2 * j + 2, :])
            dec = jnp.exp(jnp.where(causal2, col_l - row_l, NEG_INF))
            m_pair = (cb2 * dec).astype(BF16)
            col_p = col_l if L2 == P2 else jnp.where(first_p, acs_a, acs_b)
            dt_p = jnp.where(first_p, dt[:, 2 * j:2 * j + 1], dt[:, 2 * j + 1:2 * j + 2])
            xs_p = conv_s[:, sl]
            xdt = xs_p * dt_p
            rhs = jnp.concatenate([jnp.where(first_p, xdt, 0.0), jnp.where(first_p, 0.0, xdt)],
                                  axis=0).astype(BF16)
            y = jnp.dot(m_pair, rhs, preferred_element_type=F32)
            y = y + inter[:, jj * P2:(jj + 1) * P2] * jnp.exp(col_p) + dskip_ref[:, sl] * xs_p
            y_s[:, sl] = y
            a_end = col_p[Lc - 1:Lc, :]
            xw = (xdt * jnp.exp(a_end - col_p)).astype(BF16)
            h_s[:, sl] = (h_s[:, sl] * jnp.exp(a_end)
                          + lax.dot_general(b_g, xw, tdims, preferred_element_type=F32))

    gdim = SSD_D_INNER // SSD_N_GROUPS
    for g in range(SSD_N_GROUPS):
        sl = slice(g * gdim, (g + 1) * gdim)
        zz = z_ref[:, sl].astype(F32)
        yy = y_s[:, sl] * _silu(zz)
        ms = jnp.mean(yy * yy, axis=-1, keepdims=True)
        y_ref[:, sl] = (yy * lax.rsqrt(ms + EPS) * gn_ref[:, sl]).astype(y_ref.dtype)

    @pl.when(c == n_chunks - 1)
    def _fin():
        hout_ref[0] = h_s[...]
        cout_ref[0] = xp_s[0:SUBLANES, :]


def _ssd(z, xbc, dt, dt_col, conv_prev8, h0_t, p, B, L, Lc):
    T = B * L
    nc = L // Lc
    tok = lambda b, c: (b * nc + c, 0)
    tok_dt = lambda b, c: (b * nc + c, dt_col)
    per_b = lambda b, c: (b, 0, 0)
    whole = lambda b, c: (0, 0)
    y, h_t, conv8 = pl.pallas_call(
        functools.partial(_ssd_kernel, Lc=Lc),
        grid=(B, nc),
        in_specs=[pl.BlockSpec((Lc, SSD_D_INNER), tok),
                  pl.BlockSpec((Lc, SSD_CONV_DIM), tok),
                  pl.BlockSpec((Lc, PAD_HEADS), tok_dt),
                  pl.BlockSpec((1, SUBLANES, SSD_CONV_DIM), per_b),
                  pl.BlockSpec((1, SSD_D_STATE, SSD_D_INNER), per_b),
                  pl.BlockSpec((SSD_CONV, SSD_CONV_DIM), whole),
                  pl.BlockSpec((1, SSD_CONV_DIM), whole),
                  pl.BlockSpec((1, PAD_HEADS), whole),
                  pl.BlockSpec((1, PAD_HEADS), whole),
                  pl.BlockSpec((1, SSD_D_INNER), whole),
                  pl.BlockSpec((1, SSD_D_INNER), whole)],
        out_specs=[pl.BlockSpec((Lc, SSD_D_INNER), tok),
                   pl.BlockSpec((1, SSD_D_STATE, SSD_D_INNER), per_b),
                   pl.BlockSpec((1, SUBLANES, SSD_CONV_DIM), per_b)],
        out_shape=[jax.ShapeDtypeStruct((T, SSD_D_INNER), BF16),
                   jax.ShapeDtypeStruct((B, SSD_D_STATE, SSD_D_INNER), F32),
                   jax.ShapeDtypeStruct((B, SUBLANES, SSD_CONV_DIM), F32)],
        scratch_shapes=[pltpu.VMEM((SUBLANES + Lc, SSD_CONV_DIM), F32),
                        pltpu.VMEM((SSD_D_STATE, SSD_D_INNER), F32),
                        pltpu.VMEM((Lc, SSD_CONV_DIM), F32),
                        pltpu.VMEM((Lc, SSD_D_INNER), F32)],
        compiler_params=_params(("arbitrary", "arbitrary")),
        name="ssd_scan",
    )(z, xbc, dt, conv_prev8, h0_t, p["conv_w"], p["conv_b"], p["dt_bias"], p["a_log"],
      p["d_skip"], p["g_ssd"])
    return y, h_t, conv8


def _bias_kernel(idx_ref, tab_ref, o_ref):
    h = pl.program_id(0)
    idx = idx_ref[...]
    acc = jnp.full(idx.shape, NEG_INF, F32)
    for b in range(N_BUCKETS):
        acc = jnp.where(idx == b, tab_ref[b, h], acc)
    o_ref[0] = acc


def _bias_from_table(idx, table):
    Q, K = idx.shape
    return pl.pallas_call(
        _bias_kernel,
        grid=(SWA_N_HEADS,),
        in_specs=[pl.BlockSpec((Q, K), lambda h: (0, 0)),
                  pl.BlockSpec(memory_space=pltpu.SMEM)],
        out_specs=pl.BlockSpec((1, Q, K), lambda h: (h, 0, 0)),
        out_shape=jax.ShapeDtypeStruct((SWA_N_HEADS, Q, K), F32),
        compiler_params=_params(("arbitrary",)),
        name="rel_bias",
    )(jnp.asarray(idx, jnp.int32), table.astype(F32))


def _t5_bucket_np(rel):
    nb = N_BUCKETS // 2
    max_exact = nb // 2
    ret = np.where(rel > 0, nb, 0)
    n = np.abs(rel)
    nf = np.maximum(n, 1).astype(np.float32)
    large = max_exact + (np.log(nf / np.float32(max_exact)) / np.float32(math.log(MAX_DISTANCE / max_exact))
                         * np.float32(nb - max_exact)).astype(np.int32)
    large = np.minimum(large, nb - 1)
    return (ret + np.where(n < max_exact, n, large)).astype(np.int32)


def _bucket_map(q_pos, k_pos):
    qc, kc = q_pos // CHUNK, k_pos // CHUNK
    valid = (kc[None, :] >= qc[:, None] - WINDOW_CHUNKS) & (kc[None, :] <= qc[:, None])
    return np.where(valid, _t5_bucket_np(k_pos[None, :] - q_pos[:, None]), -1).astype(np.int32)


def _swa_kernel(q_ref, kp_ref, vp_ref, kc_ref, vc_ref, bias_ref, sink_ref, o_ref, s_scr, p_scr, t_scr):
    TQ = q_ref.shape[0]
    NP = kp_ref.shape[0]
    NK = NP + kc_ref.shape[0]
    HD = SWA_HEAD_DIM
    PW = 2 * HD
    pairs = SWA_REP // 2
    ndims = (((1,), (1,)), ((), ()))
    first_o = lax.broadcasted_iota(jnp.int32, (TQ, PW), 1) < HD
    first_s = lax.broadcasted_iota(jnp.int32, (TQ, 2 * NK), 1) < NK
    zero = jnp.zeros((NK, HD), BF16)
    one = jnp.ones((NK, HD), BF16)
    for g in range(SWA_N_KV):
        ks = slice(g * HD, (g + 1) * HD)
        k_g = jnp.concatenate([kp_ref[:, ks], kc_ref[:, ks]], axis=0).astype(BF16)
        v_g = jnp.concatenate([vp_ref[:, ks], vc_ref[:, ks]], axis=0).astype(BF16)
        kk = jnp.concatenate([jnp.concatenate([k_g, zero], axis=1),
                              jnp.concatenate([zero, k_g], axis=1)], axis=0)
        vv = jnp.concatenate([jnp.concatenate([v_g, zero, one, zero], axis=1),
                              jnp.concatenate([zero, v_g, zero, one], axis=1)], axis=0)
        for pr in range(pairs):
            pidx = g * pairs + pr
            s_scr[pr * TQ:(pr + 1) * TQ, :] = (
                lax.dot_general(q_ref[:, pidx * PW:(pidx + 1) * PW], kk, ndims, preferred_element_type=F32)
                + bias_ref[0, pidx])
        for pr in range(pairs):
            pidx = g * pairs + pr
            rows = slice(pr * TQ, (pr + 1) * TQ)
            s = s_scr[rows, :]
            sink_a = sink_ref[2 * pidx]
            sink_b = sink_ref[2 * pidx + 1]
            ma = jnp.maximum(jnp.max(s[:, :NK], axis=-1, keepdims=True), sink_a)
            mb = jnp.maximum(jnp.max(s[:, NK:], axis=-1, keepdims=True), sink_b)
            p_scr[rows, :] = jnp.exp(s - jnp.where(first_s, ma, mb)).astype(BF16)
            t_scr[rows, :] = jnp.where(first_o, jnp.exp(sink_a - ma), jnp.exp(sink_b - mb))
        for pr in range(pairs):
            pidx = g * pairs + pr
            rows = slice(pr * TQ, (pr + 1) * TQ)
            ov = jnp.dot(p_scr[rows, :], vv, preferred_element_type=F32)
            o = ov[:, :PW] / (ov[:, PW:] + t_scr[rows, :])
            o_ref[:, pidx * PW:(pidx + 1) * PW] = o.astype(o_ref.dtype)


def _pair_bias(bias_prev, bias_cur):
    full = jnp.concatenate([bias_prev, bias_cur], axis=-1)
    H, Q, NK = full.shape
    return jnp.transpose(full.reshape(H // 2, 2, Q, NK), (0, 2, 1, 3)).reshape(H // 2, Q, 2 * NK)


def _swa(q, k_prev_arr, v_prev_arr, prev_map, k_cur_arr, v_cur_arr, cur_map, n_cur, bias, sinks,
         B, L, TQ):
    T = B * L
    nblk = L // TQ
    kvw = SWA_N_KV * SWA_HEAD_DIM
    n_var = bias.shape[0]
    n_keys = bias.shape[3] // 2
    n_prev = n_keys - n_cur
    pairs = SWA_REP // 2
    return pl.pallas_call(
        _swa_kernel,
        grid=(B, nblk),
        in_specs=[pl.BlockSpec((TQ, SWA_N_HEADS * SWA_HEAD_DIM), lambda b, i: (b * nblk + i, 0)),
                  pl.BlockSpec((n_prev, kvw), prev_map[0]),
                  pl.BlockSpec((n_prev, kvw), prev_map[1]),
                  pl.BlockSpec((n_cur, kvw), cur_map[0]),
                  pl.BlockSpec((n_cur, kvw), cur_map[1]),
                  pl.BlockSpec((1,) + bias.shape[1:], lambda b, i: (jnp.where(i == 0, n_var - 1, 0), 0, 0, 0)),
                  pl.BlockSpec(memory_space=pltpu.SMEM)],
        out_specs=pl.BlockSpec((TQ, SWA_N_HEADS * SWA_HEAD_DIM), lambda b, i: (b * nblk + i, 0)),
        out_shape=jax.ShapeDtypeStruct((T, SWA_N_HEADS * SWA_HEAD_DIM), BF16),
        scratch_shapes=[pltpu.VMEM((pairs * TQ, 2 * n_keys), F32),
                        pltpu.VMEM((pairs * TQ, 2 * n_keys), BF16),
                        pltpu.VMEM((pairs * TQ, 2 * SWA_HEAD_DIM), F32)],
        compiler_params=_params(("parallel", "arbitrary")),
        name="swa_attention",
    )(q, k_prev_arr, v_prev_arr, k_cur_arr, v_cur_arr, bias, sinks.astype(F32))


def _mem_kernel(q_ref, mk_ref, mv_ref, o_ref):
    scale = MEM_HEAD_DIM ** -0.5
    ndims = (((1,), (1,)), ((), ()))
    for h in range(MEM_N_HEADS):
        hs = slice(h * MEM_HEAD_DIM, (h + 1) * MEM_HEAD_DIM)
        s = lax.dot_general(q_ref[:, hs], mk_ref[:, hs], ndims, preferred_element_type=F32) * scale
        m = jnp.max(s, axis=-1, keepdims=True)
        p = jnp.exp(s - m)
        den = jnp.sum(p, axis=-1, keepdims=True)
        o = jnp.dot(p.astype(BF16), mv_ref[:, hs], preferred_element_type=F32)
        o_ref[:, hs] = (o / den).astype(o_ref.dtype)


def _mem_attend(q, mk, mv, L, tm):
    T, W = q.shape
    M = mk.shape[0] // (T // L)
    return pl.pallas_call(
        _mem_kernel,
        grid=(T // tm,),
        in_specs=[pl.BlockSpec((tm, W), lambda i: (i, 0)),
                  pl.BlockSpec((M, W), lambda i: ((i * tm) // L, 0)),
                  pl.BlockSpec((M, W), lambda i: ((i * tm) // L, 0))],
        out_specs=pl.BlockSpec((tm, W), lambda i: (i, 0)),
        out_shape=jax.ShapeDtypeStruct((T, W), BF16),
        compiler_params=_params(("parallel",)),
        name="mem_attention",
    )(q, mk, mv)


def _merge_kernel(ys_ref, os_ref, om_ref, w1_ref, w2_ref, w3_ref, g0_ref, g1_ref, g2_ref, o_ref):
    a = jnp.dot(ys_ref[...], w1_ref[...], preferred_element_type=F32)
    b = jnp.dot(os_ref[...], w2_ref[...], preferred_element_type=F32)
    c = jnp.dot(om_ref[...], w3_ref[...], preferred_element_type=F32)
    o = (g0_ref[...].astype(F32) * a + g1_ref[...].astype(F32) * b + g2_ref[...].astype(F32) * c)
    o_ref[...] = o.astype(o_ref.dtype)


def _merge(y_ssd, o_s, o_m, w1, w2, w3, gates, tm, tn):
    T = y_ssd.shape[0]
    D = w1.shape[1]
    nj = D // tn
    row = lambda i, j: (i, 0)
    colw = lambda i, j: (0, j)
    return pl.pallas_call(
        _merge_kernel,
        grid=(T // tm, nj),
        in_specs=[pl.BlockSpec((tm, y_ssd.shape[1]), row),
                  pl.BlockSpec((tm, o_s.shape[1]), row),
                  pl.BlockSpec((tm, o_m.shape[1]), row),
                  pl.BlockSpec((w1.shape[0], tn), colw),
                  pl.BlockSpec((w2.shape[0], tn), colw),
                  pl.BlockSpec((w3.shape[0], tn), colw),
                  pl.BlockSpec((tm, tn), lambda i, j: (i, j)),
                  pl.BlockSpec((tm, tn), lambda i, j: (i, j + nj)),
                  pl.BlockSpec((tm, tn), lambda i, j: (i, j + 2 * nj))],
        out_specs=pl.BlockSpec((tm, tn), lambda i, j: (i, j)),
        out_shape=jax.ShapeDtypeStruct((T, D), BF16),
        compiler_params=_params(("parallel", "arbitrary")),
        name="gated_merge",
    )(y_ssd, o_s, o_m, w1, w2, w3, gates, gates, gates)


def _norm_route_kernel(xa_ref, xb_ref, g_ref, whi_ref, wlo_ref, h_ref, r_ref, *, n_a):
    x = jnp.where(pl.program_id(0) < n_a, xa_ref[...], xb_ref[...])
    ms = jnp.mean(x * x, axis=-1, keepdims=True)
    h = x * lax.rsqrt(ms + EPS) * g_ref[...]
    h_ref[:, 0, :] = h
    hb = h.astype(BF16)
    lo = (h - hb.astype(F32)).astype(BF16)
    lg = (jnp.dot(hb, whi_ref[...], preferred_element_type=F32)
          + jnp.dot(lo, whi_ref[...], preferred_element_type=F32)
          + jnp.dot(hb, wlo_ref[...], preferred_element_type=F32))
    lane = lax.broadcasted_iota(jnp.int32, lg.shape, 1)
    lane_f = lane.astype(F32)
    far = float(LANES)
    gl = jnp.where((lane >= N_EXPERTS) & (lane < N_EXPERTS + N_EXPERT_GROUPS), lg, NEG_INF)
    gmax = jnp.max(gl, axis=-1, keepdims=True)
    gidx = jnp.min(jnp.where(gl == gmax, lane_f - N_EXPERTS, far), axis=-1, keepdims=True)
    gw = 1.0 / jnp.sum(jnp.exp(gl - gmax), axis=-1, keepdims=True)
    lo_e = gidx * EXPERTS_PER_GROUP
    el = jnp.where((lane_f >= lo_e) & (lane_f < lo_e + EXPERTS_PER_GROUP), lg, NEG_INF)
    v1 = jnp.max(el, axis=-1, keepdims=True)
    i1 = jnp.min(jnp.where(el == v1, lane_f, far), axis=-1, keepdims=True)
    el2 = jnp.where(lane_f == i1, NEG_INF, el)
    v2 = jnp.max(el2, axis=-1, keepdims=True)
    i2 = jnp.min(jnp.where(el2 == v2, lane_f, far), axis=-1, keepdims=True)
    e = jnp.exp(v2 - v1)
    w1 = gw / (1.0 + e)
    w2 = gw * e / (1.0 + e)
    r_ref[...] = jnp.where(lane == 0, w1, jnp.where(lane == 1, w2,
                           jnp.where(lane == 2, i1, jnp.where(lane == 3, i2, 0.0))))


def _norm_route(xa, xb, g, w_hi, w_lo, tm):
    Ta, D = xa.shape
    Tb = xb.shape[0]
    T = Ta + Tb
    n_a, n_b = Ta // tm, Tb // tm
    assert n_a * tm == Ta and n_b * tm == Tb
    return pl.pallas_call(
        functools.partial(_norm_route_kernel, n_a=n_a),
        grid=(n_a + n_b,),
        in_specs=[pl.BlockSpec((tm, D), lambda i: (jnp.minimum(i, n_a - 1), 0)),
                  pl.BlockSpec((tm, D), lambda i: (jnp.maximum(i - n_a, 0), 0)),
                  pl.BlockSpec((1, D), lambda i: (0, 0)),
                  pl.BlockSpec((D, LANES), lambda i: (0, 0)),
                  pl.BlockSpec((D, LANES), lambda i: (0, 0))],
        out_specs=[pl.BlockSpec((tm, 1, D), lambda i: (i, 0, 0)),
                   pl.BlockSpec((tm, LANES), lambda i: (i, 0))],
        out_shape=[jax.ShapeDtypeStruct((T, 1, D), F32),
                   jax.ShapeDtypeStruct((T, LANES), F32)],
        compiler_params=_params(("parallel",)),
        name="ffn_norm_router",
    )(xa, xb, g.reshape(1, D).astype(F32), w_hi, w_lo)


def _moe_kernel(be_ref, nv_ref, s0_ref, ord_ref, h_hbm, w1_ref, w3_ref, w2_ref, yu_hbm,
                xbuf, ybuf, w1b, w3b, w2b, sem_in, sem_out, *, n_tok):
    i = pl.program_id(0)
    nb = pl.num_programs(0)
    rows = xbuf.shape[1]
    n_assign = TOP_K * n_tok
    nv = nv_ref[i]
    slot = i % 2
    nxt = jnp.minimum(i + 1, nb - 1)
    next_valid = (i + 1 < nb) & (nv_ref[nxt] > 0)

    def assignment(blk, r):
        return ord_ref[s0_ref[blk] + r]

    def row_in(s, r, tok):
        return pltpu.make_async_copy(h_hbm.at[tok], xbuf.at[s, pl.ds(r, 1), :], sem_in.at[s])

    def row_out(s, r, dst):
        return pltpu.make_async_copy(ybuf.at[s, pl.ds(r, 1), :], yu_hbm.at[dst], sem_out.at[s])

    def gather_start(blk, s):
        for r in range(rows):
            row_in(s, r, lax.shift_right_logical(assignment(blk, r), 1)).start()

    def gather_wait(s):
        for r in range(rows):
            row_in(s, r, 0).wait()

    def scatter_start(blk, s):
        n_valid = nv_ref[blk]
        for r in range(rows):
            m = assignment(blk, r)
            dst = jnp.where(r < n_valid, (m & 1) * n_tok + lax.shift_right_logical(m, 1),
                            n_assign + s * rows + r)
            row_out(s, r, dst).start()

    def scatter_wait(s):
        for r in range(rows):
            row_out(s, r, 0).wait()

    @pl.when(i == 0)
    def _clear_spare_rows():
        ybuf[...] = jnp.zeros(ybuf.shape, ybuf.dtype)
        for s in range(2):
            for r in range(rows):
                row_out(s, r, n_assign + s * rows + r).start()
            scatter_wait(s)

    @pl.when((i == 0) & (nv > 0))
    def _prologue():
        gather_start(i, 0)

    @pl.when(next_valid)
    def _prefetch():
        gather_start(nxt, 1 - slot)

    prev = be_ref[jnp.maximum(i - 1, 0)]

    @pl.when((i == 0) | (be_ref[i] != prev))
    def _load_expert():
        w1b[...] = w1_ref[0].astype(BF16)
        w3b[...] = w3_ref[0].astype(BF16)
        w2b[...] = w2_ref[0].astype(BF16)

    @pl.when(nv > 0)
    def _compute():
        gather_wait(slot)

        @pl.when(i >= 2)
        def _free_ybuf():
            scatter_wait(slot)

        x = xbuf[slot].astype(BF16)
        a = jnp.dot(x, w1b[...], preferred_element_type=F32)
        b = jnp.dot(x, w3b[...], preferred_element_type=F32)
        mid = (_silu(a) * b).astype(BF16)
        y = jnp.dot(mid, w2b[...], preferred_element_type=F32)
        ybuf[slot] = y
        scatter_start(i, slot)

        @pl.when(jnp.logical_not(next_valid))
        def _drain():
            scatter_wait(slot)

            @pl.when(i >= 1)
            def _drain_prev():
                scatter_wait(1 - slot)


def _moe(h2, route, w_gate, w_up, w_down):
    T, _, D = h2.shape
    F = w_gate.shape[2]
    assert TOP_K == 2
    M = T * TOP_K
    BM = MOE_ROWS
    nb = (M + N_EXPERTS * (BM - 1) + BM - 1) // BM
    e_flat = route[:, TOP_K:2 * TOP_K].astype(jnp.int32).reshape(M)
    idx_bits = (M - 1).bit_length()
    assert (N_EXPERTS << idx_bits) < 2 ** 31
    packed = lax.sort(e_flat * (1 << idx_bits) + jnp.arange(M, dtype=jnp.int32), is_stable=False)
    order = jnp.pad(packed & ((1 << idx_bits) - 1), (0, BM))
    experts = jnp.arange(N_EXPERTS, dtype=jnp.int32)
    counts = jnp.sum((experts[:, None] == e_flat[None, :]).astype(jnp.int32), axis=1)
    padded = (counts + BM - 1) // BM * BM
    pad_end = jnp.cumsum(padded)
    pad_start = pad_end - padded
    start = jnp.cumsum(counts) - counts
    blk0 = jnp.arange(nb, dtype=jnp.int32) * BM
    blk_exp = jnp.minimum(jnp.sum((pad_end[None, :] <= blk0[:, None]).astype(jnp.int32), axis=1), N_EXPERTS - 1)
    pick = (blk_exp[:, None] == experts[None, :]).astype(jnp.int32)
    off0 = blk0 - jnp.sum(pick * pad_start[None, :], axis=1)
    blk_nv = jnp.clip(jnp.sum(pick * counts[None, :], axis=1) - off0, 0, BM).astype(jnp.int32)
    blk_s0 = jnp.clip(jnp.sum(pick * start[None, :], axis=1) + off0, 0, M - 1).astype(jnp.int32)

    grid_spec = pltpu.PrefetchScalarGridSpec(
        num_scalar_prefetch=4,
        grid=(nb,),
        in_specs=[pl.BlockSpec(memory_space=pl.ANY),
                  pl.BlockSpec((1, D, F), lambda i, be, nv, s0, od: (be[i], 0, 0)),
                  pl.BlockSpec((1, D, F), lambda i, be, nv, s0, od: (be[i], 0, 0)),
                  pl.BlockSpec((1, F, D), lambda i, be, nv, s0, od: (be[i], 0, 0))],
        out_specs=pl.BlockSpec(memory_space=pl.ANY),
        scratch_shapes=[pltpu.VMEM((2, BM, D), F32),
                        pltpu.VMEM((2, BM, D), F32),
                        pltpu.VMEM((D, F), BF16),
                        pltpu.VMEM((D, F), BF16),
                        pltpu.VMEM((F, D), BF16),
                        pltpu.SemaphoreType.DMA((2,)),
                        pltpu.SemaphoreType.DMA((2,))],
    )
    return pl.pallas_call(
        functools.partial(_moe_kernel, n_tok=T),
        grid_spec=grid_spec,
        out_shape=jax.ShapeDtypeStruct((M + 2 * BM, 1, D), F32),
        compiler_params=_params(("arbitrary",)),
        name="moe_experts",
    )(blk_exp.astype(jnp.int32), blk_nv, blk_s0, order, h2, w_gate, w_up, w_down)


def _combine_kernel(x_ref, r_ref, y0_ref, y1_ref, o_ref):
    w0 = r_ref[:, 0:1]
    w1 = r_ref[:, 1:2]
    o_ref[...] = x_ref[...] + (w0 * y0_ref[:, 0, :] + w1 * y1_ref[:, 0, :])


def _combine(x, route, yu, blk0, tm):
    T, D = x.shape
    n_all = route.shape[0] // tm
    return pl.pallas_call(
        _combine_kernel,
        grid=(T // tm,),
        in_specs=[pl.BlockSpec((tm, D), lambda i: (i, 0)),
                  pl.BlockSpec((tm, LANES), lambda i: (blk0 + i, 0)),
                  pl.BlockSpec((tm, 1, D), lambda i: (blk0 + i, 0, 0)),
                  pl.BlockSpec((tm, 1, D), lambda i: (n_all + blk0 + i, 0, 0))],
        out_specs=pl.BlockSpec((tm, D), lambda i: (i, 0)),
        out_shape=jax.ShapeDtypeStruct((T, D), F32),
        compiler_params=_params(("parallel",)),
        name="moe_combine",
    )(x, route, yu, yu)


def _layer(x, B, L, p, conv_prev8, h0_t, mk, mv, swa_prev, swa_bias, Lc, TQ):
    T, D = x.shape
    tmm = min(MM_ROWS, T)
    h = _rmsnorm(x, p["g_mix"], min(512, T))
    z = _matmul(h, p["wz"], BF16, tmm, MM_COLS_WIDE, name="proj_z")
    xbc = _matmul(h, p["wxbc"], BF16, tmm, MM_COLS_WIDE, name="proj_xbc")
    q_s = _matmul(h, p["wqs"], BF16, tmm, MM_COLS_WIDE, epi=_epi_group_norm, col_extras=(p["g_q_swa"],),
                  const_extras=(p["mavg"],), name="proj_q_swa")
    kvw = SWA_N_KV * SWA_HEAD_DIM
    kvd = _matmul(h, p["wkvdt"], F32, tmm, 2 * kvw + PAD_HEADS, epi=_epi_kv_dt, const_extras=(p["g_k"], p["mavg"]),
                  name="proj_kv_dt")
    kv = kvd
    q_m = _matmul(h, p["wqm"], BF16, tmm, MM_COLS_WIDE, epi=_epi_row_norm, col_extras=(p["g_q_mem"],),
                  name="proj_q_mem")
    gates = _matmul(h, p["wg"], BF16, tmm, MM_COLS_WIDE, epi=_epi_sigmoid, name="proj_gates")

    y_ssd, h_t, conv8 = _ssd(z, xbc, kvd, (2 * kvw) // PAD_HEADS, conv_prev8, h0_t, p, B, L, Lc)

    nblk = L // TQ
    if swa_prev is None:
        prev_map = (lambda b, i: (b * nblk + jnp.maximum(i - 1, 0), 0),
                    lambda b, i: (b * nblk + jnp.maximum(i - 1, 0), 1))
        cur_map = (lambda b, i: (b * nblk + i, 0), lambda b, i: (b * nblk + i, 1))
        o_s = _swa(q_s, kv, kv, prev_map, kv, kv, cur_map, TQ, swa_bias, p["sinks"], B, L, TQ)
    else:
        n_cur = swa_bias.shape[3] // 2 - swa_prev[0].shape[0] // B
        kv3 = jnp.pad(kv[:, :2 * kvw].reshape(B, L, 2 * kvw), ((0, 0), (0, n_cur - L), (0, 0)))
        k_new = kv3[:, :, :kvw].reshape(B * n_cur, kvw)
        v_new = kv3[:, :, kvw:].reshape(B * n_cur, kvw)
        per_b = (lambda b, i: (b, 0), lambda b, i: (b, 0))
        o_s = _swa(q_s, swa_prev[0], swa_prev[1], per_b, k_new, v_new, per_b, n_cur, swa_bias, p["sinks"],
                   B, L, TQ)

    o_m = _mem_attend(q_m, mk, mv, L, min(256, L))

    merged = _merge(y_ssd, o_s, o_m, p["w_o_ssd"], p["w_o_swa"], p["w_o_mem"], gates, tmm, MERGE_COLS)
    x1 = _matmul(merged, p["w_out"], F32, tmm, MM_COLS, epi=_epi_residual, tile_extras=(x,), name="proj_out")

    return x1, conv8, h_t, kv


def _state_to_heads(h_t, B):
    return jnp.transpose(h_t.reshape(B, SSD_D_STATE, SSD_N_HEADS, SSD_HEAD_DIM), (0, 2, 3, 1))


def kernel(x_prompt, x_sample, cache_conv, state_ssd, cache_swa_k, cache_swa_v, cache_mem_k, cache_mem_v, mem_prompt, rel_bias_table, g_mix, w_in, conv_w, conv_b, dt_bias, a_log, d_skip, g_ssd, w_o_ssd, g_q_swa, g_k_swa, sinks, w_o_swa, g_mem, w_mem_k, w_mem_v, g_q_mem, g_k_mem, w_o_mem, w_out, g_ffn, w_router_grp, w_router_exp, w_exp_gate, w_exp_up, w_exp_down):
    B, S, D = x_prompt.shape
    Bd, Sd, _ = x_sample.shape
    depth = w_in.shape[0]
    assert depth == 1
    l = 0
    kvw = SWA_N_KV * SWA_HEAD_DIM
    qw = SWA_N_HEADS * SWA_HEAD_DIM
    mw = MEM_N_HEADS * MEM_HEAD_DIM

    sizes = (SSD_D_INNER, SSD_CONV_DIM, SSD_N_HEADS, qw, kvw, kvw, mw, 3 * D)
    offs = np.concatenate([[0], np.cumsum(sizes)])
    w = w_in[l]
    cols = [w[:, int(offs[k]):int(offs[k + 1])] for k in range(len(sizes))]
    pad_h = PAD_HEADS - SSD_N_HEADS
    w_r = jnp.pad(jnp.concatenate([w_router_exp[l], w_router_grp[l]], axis=1),
                  ((0, 0), (0, LANES - N_EXPERTS - N_EXPERT_GROUPS)))
    w_r_hi = w_r.astype(BF16)
    p = {
        "g_mix": g_mix[l],
        "wz": cols[0].astype(BF16),
        "wxbc": cols[1].astype(BF16),
        "wkvdt": jnp.concatenate([cols[4], cols[5], jnp.pad(cols[2], ((0, 0), (0, pad_h)))], axis=1).astype(BF16),
        "wqs": cols[3].astype(BF16),
        "wqm": cols[6].astype(BF16),
        "wg": cols[7].astype(BF16),
        "conv_w": conv_w[l].astype(F32),
        "conv_b": conv_b[l].reshape(1, SSD_CONV_DIM).astype(F32),
        "dt_bias": jnp.pad(dt_bias[l], (0, pad_h)).reshape(1, PAD_HEADS).astype(F32),
        "a_log": jnp.pad(a_log[l], (0, pad_h)).reshape(1, PAD_HEADS).astype(F32),
        "d_skip": jnp.repeat(d_skip[l], SSD_HEAD_DIM).reshape(1, SSD_D_INNER).astype(F32),
        "g_ssd": g_ssd[l].reshape(1, SSD_D_INNER).astype(F32),
        "g_q_swa": (jnp.tile(g_q_swa[l], SWA_N_HEADS) * SWA_HEAD_DIM ** -0.5).reshape(1, qw).astype(F32),
        "g_k": jnp.tile(g_k_swa[l], SWA_N_KV).reshape(1, kvw).astype(F32),
        "g_q_mem": jnp.tile(g_q_mem[l], MEM_N_HEADS).reshape(1, mw).astype(F32),
        "mavg": _group_mean_matrix(256, SWA_HEAD_DIM),
        "sinks": sinks[l],
        "w_o_ssd": w_o_ssd[l].astype(BF16),
        "w_o_swa": w_o_swa[l].astype(BF16),
        "w_o_mem": w_o_mem[l].astype(BF16),
        "w_out": w_out[l].astype(BF16),
        "g_ffn": g_ffn[l],
        "w_r_hi": w_r_hi,
        "w_r_lo": (w_r - w_r_hi.astype(F32)).astype(BF16),
        "w_exp_gate": w_exp_gate[l],
        "w_exp_up": w_exp_up[l],
        "w_exp_down": w_exp_down[l],
    }

    M = mem_prompt.shape[1]
    mn = _rmsnorm(mem_prompt.reshape(B * M, D), g_mem[l], min(256, B * M))
    mk_p = _matmul(mn, w_mem_k[l].astype(BF16), F32, min(256, B * M), MEM_HEAD_DIM, epi=_epi_row_norm,
                   col_extras=(jnp.tile(g_k_mem[l], MEM_N_HEADS).reshape(1, mw).astype(F32),), name="mem_k")
    mv_p = _matmul(mn, w_mem_v[l].astype(BF16), F32, min(256, B * M), MEM_HEAD_DIM, name="mem_v")

    TQ = 2 * CHUNK
    qpos = np.arange(TQ)
    prev_p = _bias_from_table(_bucket_map(qpos, np.arange(TQ) - TQ), rel_bias_table)
    cur_p = _bias_from_table(_bucket_map(qpos, np.arange(TQ)), rel_bias_table)
    bias_p = jnp.stack([_pair_bias(prev_p, cur_p), _pair_bias(jnp.full_like(prev_p, NEG_INF), cur_p)])
    C = cache_swa_k.shape[2]
    qpos_s = PAST_LEN + np.arange(Sd)
    cur_map_s = _bucket_map(qpos_s, PAST_LEN + np.arange(C))
    cur_map_s[:, Sd:] = -1
    bias_s = _pair_bias(_bias_from_table(_bucket_map(qpos_s, PAST_LEN - C + np.arange(C)), rel_bias_table),
                        _bias_from_table(cur_map_s, rel_bias_table))[None]

    conv0 = jnp.zeros((B, SUBLANES, SSD_CONV_DIM), F32)
    h0 = jnp.zeros((B, SSD_D_STATE, SSD_D_INNER), F32)
    x1_p, conv8_p, ht_p, kv_p = _layer(x_prompt.reshape(B * S, D), B, S, p, conv0, h0,
                                     mk_p.astype(BF16), mv_p.astype(BF16), None,
                                     bias_p, CHUNK, TQ)
    conv_prev = jnp.pad(cache_conv[l], ((0, 0), (SUBLANES - (SSD_CONV - 1), 0), (0, 0)))
    h0_s = jnp.transpose(state_ssd[l], (0, 3, 1, 2)).reshape(Bd, SSD_D_STATE, SSD_D_INNER)
    x1_s, conv8_s, ht_s, kv_s = _layer(x_sample.reshape(Bd * Sd, D), Bd, Sd, p, conv_prev, h0_s,
                                     cache_mem_k[l].reshape(Bd * M, mw).astype(BF16),
                                     cache_mem_v[l].reshape(Bd * M, mw).astype(BF16),
                                     (cache_swa_k[l].reshape(Bd * C, kvw), cache_swa_v[l].reshape(Bd * C, kvw)),
                                     bias_s, Sd, Sd)

    tr = math.gcd(B * S, Bd * Sd, MOE_ROWS)
    h2, route = _norm_route(x1_p, x1_s, p["g_ffn"], p["w_r_hi"], p["w_r_lo"], tr)
    yu = _moe(h2, route, p["w_exp_gate"], p["w_exp_up"], p["w_exp_down"])
    yp = _combine(x1_p, route, yu, 0, tr)
    ys = _combine(x1_s, route, yu, (B * S) // tr, tr)

    keep = min(WINDOW, S)
    kv_p = kv_p.reshape(B, S, -1)[:, S - keep:, :2 * kvw].reshape(B, keep, 2, SWA_N_KV, SWA_HEAD_DIM)
    kv_s = kv_s[:, :2 * kvw].reshape(Bd, Sd, 2, SWA_N_KV, SWA_HEAD_DIM)
    tail = SUBLANES - (SSD_CONV - 1)
    return (yp.reshape(B, S, D), ys.reshape(Bd, Sd, D),
            conv8_p[None, :, tail:], _state_to_heads(ht_p, B)[None],
            kv_p[None, :, :, 0], kv_p[None, :, :, 1],
            mk_p.reshape(1, B, M, MEM_N_HEADS, MEM_HEAD_DIM), mv_p.reshape(1, B, M, MEM_N_HEADS, MEM_HEAD_DIM),
            conv8_s[None, :, tail:], _state_to_heads(ht_s, Bd)[None],
            kv_s[None, :, :, 0], kv_s[None, :, :, 1])
```

```python
import functools
import math

import numpy as np
import jax
import jax.numpy as jnp
from jax import lax
from jax.experimental import pallas as pl
from jax.experimental.pallas import tpu as pltpu

F32 = jnp.float32
BF16 = jnp.bfloat16
EPS = 1e-6
NEG_INF = float("-inf")

CHUNK = 64
SSD_HEAD_DIM = 64
SSD_N_HEADS = 64
SSD_N_GROUPS = 8
SSD_D_STATE = 128
SSD_D_INNER = SSD_N_HEADS * SSD_HEAD_DIM
SSD_GN = SSD_N_GROUPS * SSD_D_STATE
SSD_CONV_DIM = SSD_D_INNER + 2 * SSD_GN
SSD_CONV = 4
SWA_N_HEADS = 32
SWA_N_KV = 4
SWA_HEAD_DIM = 64
SWA_REP = SWA_N_HEADS // SWA_N_KV
WINDOW = 128
WINDOW_CHUNKS = WINDOW // CHUNK
MEM_N_HEADS = 4
MEM_HEAD_DIM = 512
N_BUCKETS = 32
MAX_DISTANCE = 128
N_EXPERT_GROUPS = 4
EXPERTS_PER_GROUP = 16
N_EXPERTS = N_EXPERT_GROUPS * EXPERTS_PER_GROUP
TOP_K = 2
PAST_LEN = 4096

LANES = 128
SUBLANES = 8
VMEM_LIMIT = 56 * 1024 * 1024
MOE_ROWS = 256
MM_ROWS = 1024
MM_COLS = 1024
MM_COLS_WIDE = 2048
MERGE_COLS = 256
PAD_HEADS = LANES


def _params(sem, vmem=VMEM_LIMIT):
    return pltpu.CompilerParams(dimension_semantics=sem, vmem_limit_bytes=vmem)


def _sigmoid(x):
    return 0.5 * (jnp.tanh(0.5 * x) + 1.0)


def _silu(x):
    u = 0.5 * x
    return u * (jnp.tanh(u) + 1.0)


def _split3(x):
    x1 = x.astype(BF16)
    r1 = x - x1.astype(F32)
    x2 = r1.astype(BF16)
    x3 = (r1 - x2.astype(F32)).astype(BF16)
    return x1, x2, x3


def _rmsnorm_kernel(x_ref, g_ref, o_ref):
    x = x_ref[...]
    ms = jnp.mean(x * x, axis=-1, keepdims=True)
    o_ref[...] = (x * lax.rsqrt(ms + EPS) * g_ref[...]).astype(o_ref.dtype)


def _rmsnorm(x, g, tm):
    T, D = x.shape
    return pl.pallas_call(
        _rmsnorm_kernel,
        grid=(T // tm,),
        in_specs=[pl.BlockSpec((tm, D), lambda i: (i, 0)),
                  pl.BlockSpec((1, D), lambda i: (0, 0))],
        out_specs=pl.BlockSpec((tm, D), lambda i: (i, 0)),
        out_shape=jax.ShapeDtypeStruct((T, D), BF16),
        compiler_params=_params(("parallel",)),
        name="rmsnorm",
    )(x, g.reshape(1, D).astype(F32))


def _mm_kernel(a_ref, b_ref, *refs, epi):
    o_ref = refs[-1]
    acc = jnp.dot(a_ref[...], b_ref[...], preferred_element_type=F32)
    if epi is not None:
        acc = epi(acc, *[r[...] for r in refs[:-1]])
    o_ref[...] = acc.astype(o_ref.dtype)


def _matmul(a, b, out_dtype, tm, tn, epi=None, col_extras=(), tile_extras=(), const_extras=(), name="matmul"):
    M, K = a.shape
    N = b.shape[1]
    assert M % tm == 0 and N % tn == 0, (M, N, tm, tn)
    in_specs = [pl.BlockSpec((tm, K), lambda i, j: (i, 0)),
                pl.BlockSpec((K, tn), lambda i, j: (0, j))]
    for _ in col_extras:
        in_specs.append(pl.BlockSpec((1, tn), lambda i, j: (0, j)))
    for _ in tile_extras:
        in_specs.append(pl.BlockSpec((tm, tn), lambda i, j: (i, j)))
    for c in const_extras:
        in_specs.append(pl.BlockSpec(c.shape, lambda i, j: (0, 0)))
    return pl.pallas_call(
        functools.partial(_mm_kernel, epi=epi),
        grid=(M // tm, N // tn),
        in_specs=in_specs,
        out_specs=pl.BlockSpec((tm, tn), lambda i, j: (i, j)),
        out_shape=jax.ShapeDtypeStruct((M, N), out_dtype),
        compiler_params=_params(("parallel", "arbitrary")),
        name=name,
    )(a, b, *col_extras, *tile_extras, *const_extras)


def _group_mean_matrix(width, group):
    idx = np.arange(width) // group
    return jnp.asarray((idx[:, None] == idx[None, :]).astype(np.float32) / group, dtype=BF16)


def _epi_group_norm(acc, gain, mavg):
    w = mavg.shape[0]
    outs = []
    for c in range(acc.shape[1] // w):
        a = acc[:, c * w:(c + 1) * w]
        s = a * a
        hi = s.astype(BF16)
        lo = (s - hi.astype(F32)).astype(BF16)
        ms = (jnp.dot(hi, mavg, preferred_element_type=F32)
              + jnp.dot(lo, mavg, preferred_element_type=F32))
        outs.append(a * lax.rsqrt(ms + EPS))
    normed = outs[0] if len(outs) == 1 else jnp.concatenate(outs, axis=1)
    return normed * gain


def _epi_kv_dt(acc, gain, mavg):
    kw = mavg.shape[0]
    return jnp.concatenate([_epi_group_norm(acc[:, :kw], gain, mavg), acc[:, kw:]], axis=1)


def _epi_row_norm(acc, gain):
    outs = []
    for c in range(acc.shape[1] // MEM_HEAD_DIM):
        a = acc[:, c * MEM_HEAD_DIM:(c + 1) * MEM_HEAD_DIM]
        outs.append(a * lax.rsqrt(jnp.mean(a * a, axis=-1, keepdims=True) + EPS))
    normed = outs[0] if len(outs) == 1 else jnp.concatenate(outs, axis=1)
    return normed * gain


def _epi_sigmoid(acc):
    return _sigmoid(acc)


def _epi_residual(acc, res):
    return acc + res


def _ssd_kernel(z_ref, xbc_ref, dt_ref, cprev_ref, h0_ref, cw_ref, cb_ref, dtb_ref, alog_ref,
                dskip_ref, gn_ref, y_ref, hout_ref, cout_ref, xp_s, h_s, conv_s, y_s, *, Lc):
    c = pl.program_id(1)
    n_chunks = pl.num_programs(1)
    P2 = 2 * SSD_HEAD_DIM
    L2 = 2 * Lc

    @pl.when(c == 0)
    def _init():
        xp_s[0:SUBLANES, :] = cprev_ref[0]
        h_s[...] = h0_ref[0]

    xp_s[SUBLANES:SUBLANES + Lc, :] = xbc_ref[...].astype(F32)
    cblk = 512
    row8 = lax.broadcasted_iota(jnp.int32, (SUBLANES, cblk), 0)
    for j in range(SSD_CONV_DIM // cblk):
        sl = slice(j * cblk, (j + 1) * cblk)
        cur = xp_s[SUBLANES:SUBLANES + Lc, sl]
        tail = xp_s[0:SUBLANES, sl]
        acc = cb_ref[:, sl] + cw_ref[SSD_CONV - 1:SSD_CONV, sl] * cur
        for k in range(SSD_CONV - 1):
            sh = SSD_CONV - 1 - k
            down = pltpu.roll(cur, sh, axis=0)
            top = jnp.where(row8 >= sh, down[0:SUBLANES], pltpu.roll(tail, sh, axis=0))
            acc = acc + cw_ref[k:k + 1, sl] * jnp.concatenate([top, down[SUBLANES:]], axis=0)
        conv_s[:, sl] = _silu(acc)
    xp_s[0:SUBLANES, :] = xp_s[Lc:Lc + SUBLANES, :]

    dtv = dt_ref[...] + dtb_ref[...]
    dt = jnp.maximum(dtv, 0.0) + jnp.log1p(jnp.exp(-jnp.abs(dtv)))
    adt = dt * (-jnp.exp(alog_ref[...]))
    row = lax.broadcasted_iota(jnp.int32, (Lc, Lc), 0)
    col = lax.broadcasted_iota(jnp.int32, (Lc, Lc), 1)
    tri = (col <= row).astype(BF16)
    row2 = lax.broadcasted_iota(jnp.int32, (Lc, L2), 0)
    col2 = lax.broadcasted_iota(jnp.int32, (Lc, L2), 1)
    col2m = jnp.where(col2 >= Lc, col2 - Lc, col2)
    tri_t2 = (row2 <= col2m).astype(BF16)
    causal2 = col2m <= row2
    a1, a2, a3 = _split3(adt)
    acs = (jnp.dot(tri, a1, preferred_element_type=F32)
           + jnp.dot(tri, a2, preferred_element_type=F32)
           + jnp.dot(tri, a3, preferred_element_type=F32))
    tdims = (((0,), (0,)), ((), ()))
    acs_t2 = (lax.dot_general(a1, tri_t2, tdims, preferred_element_type=F32)
              + lax.dot_general(a2, tri_t2, tdims, preferred_element_type=F32)
              + lax.dot_general(a3, tri_t2, tdims, preferred_element_type=F32))

    lane_p = lax.broadcasted_iota(jnp.int32, (Lc, P2), 1)
    first_p = lane_p < SSD_HEAD_DIM
    first_l = col2 < Lc
    first_l1 = first_l[0:1, :]
    ndims = (((1,), (1,)), ((), ()))

    for g in range(SSD_N_GROUPS):
        b_g = conv_s[:, SSD_D_INNER + g * SSD_D_STATE:SSD_D_INNER + (g + 1) * SSD_D_STATE].astype(BF16)
        c_g = conv_s[:, SSD_D_INNER + SSD_GN + g * SSD_D_STATE:
                     SSD_D_INNER + SSD_GN + (g + 1) * SSD_D_STATE].astype(BF16)
        b2 = jnp.concatenate([b_g, b_g], axis=0)
        cb2 = lax.dot_general(c_g, b2, ndims, preferred_element_type=F32)
        gw = SSD_HEAD_DIM * (SSD_N_HEADS // SSD_N_GROUPS)
        inter = jnp.dot(c_g, h_s[:, g * gw:(g + 1) * gw].astype(BF16), preferred_element_type=F32)
        for jj in range(gw // P2):
            j = g * (gw // P2) + jj
            sl = slice(j * P2, (j + 1) * P2)
            acs_a = acs[:, 2 * j:2 * j + 1]
            acs_b = acs[:, 2 * j + 1:2 * j + 2]
            col_l = jnp.where(first_l, acs_a, acs_b)
            row_l = jnp.where(first_l1, acs_t2[2 * j:2 * j + 1, :], acs_t2[2 * j + 1:2 * j + 2, :])
            dec = jnp.exp(jnp.where(causal2, col_l - row_l, NEG_INF))
            m_pair = (cb2 * dec).astype(BF16)
            col_p = col_l if L2 == P2 else jnp.where(first_p, acs_a, acs_b)
            dt_p = jnp.where(first_p, dt[:, 2 * j:2 * j + 1], dt[:, 2 * j + 1:2 * j + 2])
            xs_p = conv_s[:, sl]
            xdt = xs_p * dt_p
            rhs = jnp.concatenate([jnp.where(first_p, xdt, 0.0), jnp.where(first_p, 0.0, xdt)],
                                  axis=0).astype(BF16)
            y = jnp.dot(m_pair, rhs, preferred_element_type=F32)
            y = y + inter[:, jj * P2:(jj + 1) * P2] * jnp.exp(col_p) + dskip_ref[:, sl] * xs_p
            y_s[:, sl] = y
            a_end = col_p[Lc - 1:Lc, :]
            xw = (xdt * jnp.exp(a_end - col_p)).astype(BF16)
            h_s[:, sl] = (h_s[:, sl] * jnp.exp(a_end)
                          + lax.dot_general(b_g, xw, tdims, preferred_element_type=F32))

    gdim = SSD_D_INNER // SSD_N_GROUPS
    for g in range(SSD_N_GROUPS):
        sl = slice(g * gdim, (g + 1) * gdim)
        zz = z_ref[:, sl].astype(F32)
        yy = y_s[:, sl] * _silu(zz)
        ms = jnp.mean(yy * yy, axis=-1, keepdims=True)
        y_ref[:, sl] = (yy * lax.rsqrt(ms + EPS) * gn_ref[:, sl]).astype(y_ref.dtype)

    @pl.when(c == n_chunks - 1)
    def _fin():
        hout_ref[0] = h_s[...]
        cout_ref[0] = xp_s[0:SUBLANES, :]


def _ssd(z, xbc, dt, dt_col, conv_prev8, h0_t, p, B, L, Lc):
    T = B * L
    nc = L // Lc
    tok = lambda b, c: (b * nc + c, 0)
    tok_dt = lambda b, c: (b * nc + c, dt_col)
    per_b = lambda b, c: (b, 0, 0)
    whole = lambda b, c: (0, 0)
    y, h_t, conv8 = pl.pallas_call(
        functools.partial(_ssd_kernel, Lc=Lc),
        grid=(B, nc),
        in_specs=[pl.BlockSpec((Lc, SSD_D_INNER), tok),
                  pl.BlockSpec((Lc, SSD_CONV_DIM), tok),
                  pl.BlockSpec((Lc, PAD_HEADS), tok_dt),
                  pl.BlockSpec((1, SUBLANES, SSD_CONV_DIM), per_b),
                  pl.BlockSpec((1, SSD_D_STATE, SSD_D_INNER), per_b),
                  pl.BlockSpec((SSD_CONV, SSD_CONV_DIM), whole),
                  pl.BlockSpec((1, SSD_CONV_DIM), whole),
                  pl.BlockSpec((1, PAD_HEADS), whole),
                  pl.BlockSpec((1, PAD_HEADS), whole),
                  pl.BlockSpec((1, SSD_D_INNER), whole),
                  pl.BlockSpec((1, SSD_D_INNER), whole)],
        out_specs=[pl.BlockSpec((Lc, SSD_D_INNER), tok),
                   pl.BlockSpec((1, SSD_D_STATE, SSD_D_INNER), per_b),
                   pl.BlockSpec((1, SUBLANES, SSD_CONV_DIM), per_b)],
        out_shape=[jax.ShapeDtypeStruct((T, SSD_D_INNER), BF16),
                   jax.ShapeDtypeStruct((B, SSD_D_STATE, SSD_D_INNER), F32),
                   jax.ShapeDtypeStruct((B, SUBLANES, SSD_CONV_DIM), F32)],
        scratch_shapes=[pltpu.VMEM((SUBLANES + Lc, SSD_CONV_DIM), F32),
                        pltpu.VMEM((SSD_D_STATE, SSD_D_INNER), F32),
                        pltpu.VMEM((Lc, SSD_CONV_DIM), F32),
                        pltpu.VMEM((Lc, SSD_D_INNER), F32)],
        compiler_params=_params(("arbitrary", "arbitrary")),
        name="ssd_scan",
    )(z, xbc, dt, conv_prev8, h0_t, p["conv_w"], p["conv_b"], p["dt_bias"], p["a_log"],
      p["d_skip"], p["g_ssd"])
    return y, h_t, conv8


def _bias_kernel(idx_ref, tab_ref, o_ref):
    h = pl.program_id(0)
    idx = idx_ref[...]
    acc = jnp.full(idx.shape, NEG_INF, F32)
    for b in range(N_BUCKETS):
        acc = jnp.where(idx == b, tab_ref[b, h], acc)
    o_ref[0] = acc


def _bias_from_table(idx, table):
    Q, K = idx.shape
    return pl.pallas_call(
        _bias_kernel,
        grid=(SWA_N_HEADS,),
        in_specs=[pl.BlockSpec((Q, K), lambda h: (0, 0)),
                  pl.BlockSpec(memory_space=pltpu.SMEM)],
        out_specs=pl.BlockSpec((1, Q, K), lambda h: (h, 0, 0)),
        out_shape=jax.ShapeDtypeStruct((SWA_N_HEADS, Q, K), F32),
        compiler_params=_params(("arbitrary",)),
        name="rel_bias",
    )(jnp.asarray(idx, jnp.int32), table.astype(F32))


def _t5_bucket_np(rel):
    nb = N_BUCKETS // 2
    max_exact = nb // 2
    ret = np.where(rel > 0, nb, 0)
    n = np.abs(rel)
    nf = np.maximum(n, 1).astype(np.float32)
    large = max_exact + (np.log(nf / np.float32(max_exact)) / np.float32(math.log(MAX_DISTANCE / max_exact))
                         * np.float32(nb - max_exact)).astype(np.int32)
    large = np.minimum(large, nb - 1)
    return (ret + np.where(n < max_exact, n, large)).astype(np.int32)


def _bucket_map(q_pos, k_pos):
    qc, kc = q_pos // CHUNK, k_pos // CHUNK
    valid = (kc[None, :] >= qc[:, None] - WINDOW_CHUNKS) & (kc[None, :] <= qc[:, None])
    return np.where(valid, _t5_bucket_np(k_pos[None, :] - q_pos[:, None]), -1).astype(np.int32)


def _swa_kernel(q_ref, kp_ref, vp_ref, kc_ref, vc_ref, bias_ref, sink_ref, o_ref, s_scr, p_scr, t_scr):
    TQ = q_ref.shape[0]
    NP = kp_ref.shape[0]
    NK = NP + kc_ref.shape[0]
    HD = SWA_HEAD_DIM
    PW = 2 * HD
    pairs = SWA_REP // 2
    ndims = (((1,), (1,)), ((), ()))
    first_o = lax.broadcasted_iota(jnp.int32, (TQ, PW), 1) < HD
    first_s = lax.broadcasted_iota(jnp.int32, (TQ, 2 * NK), 1) < NK
    zero = jnp.zeros((NK, HD), BF16)
    one = jnp.ones((NK, HD), BF16)
    for g in range(SWA_N_KV):
        ks = slice(g * HD, (g + 1) * HD)
        k_g = jnp.concatenate([kp_ref[:, ks], kc_ref[:, ks]], axis=0).astype(BF16)
        v_g = jnp.concatenate([vp_ref[:, ks], vc_ref[:, ks]], axis=0).astype(BF16)
        kk = jnp.concatenate([jnp.concatenate([k_g, zero], axis=1),
                              jnp.concatenate([zero, k_g], axis=1)], axis=0)
        vv = jnp.concatenate([jnp.concatenate([v_g, zero, one, zero], axis=1),
                              jnp.concatenate([zero, v_g, zero, one], axis=1)], axis=0)
        for pr in range(pairs):
            pidx = g * pairs + pr
            s_scr[pr * TQ:(pr + 1) * TQ, :] = (
                lax.dot_general(q_ref[:, pidx * PW:(pidx + 1) * PW], kk, ndims, preferred_element_type=F32)
                + bias_ref[0, pidx])
        for pr in range(pairs):
            pidx = g * pairs + pr
            rows = slice(pr * TQ, (pr + 1) * TQ)
            s = s_scr[rows, :]
            sink_a = sink_ref[2 * pidx]
            sink_b = sink_ref[2 * pidx + 1]
            ma = jnp.maximum(jnp.max(s[:, :NK], axis=-1, keepdims=True), sink_a)
            mb = jnp.maximum(jnp.max(s[:, NK:], axis=-1, keepdims=True), sink_b)
            p_scr[rows, :] = jnp.exp(s - jnp.where(first_s, ma, mb)).astype(BF16)
            t_scr[rows, :] = jnp.where(first_o, jnp.exp(sink_a - ma), jnp.exp(sink_b - mb))
        for pr in range(pairs):
            pidx = g * pairs + pr
            rows = slice(pr * TQ, (pr + 1) * TQ)
            ov = jnp.dot(p_scr[rows, :], vv, preferred_element_type=F32)
            o = ov[:, :PW] / (ov[:, PW:] + t_scr[rows, :])
            o_ref[:, pidx * PW:(pidx + 1) * PW] = o.astype(o_ref.dtype)


def _pair_bias(bias_prev, bias_cur):
    full = jnp.concatenate([bias_prev, bias_cur], axis=-1)
    H, Q, NK = full.shape
    return jnp.transpose(full.reshape(H // 2, 2, Q, NK), (0, 2, 1, 3)).reshape(H // 2, Q, 2 * NK)


def _swa(q, k_prev_arr, v_prev_arr, prev_map, k_cur_arr, v_cur_arr, cur_map, n_cur, bias, sinks,
         B, L, TQ):
    T = B * L
    nblk = L // TQ
    kvw = SWA_N_KV * SWA_HEAD_DIM
    n_var = bias.shape[0]
    n_keys = bias.shape[3] // 2
    n_prev = n_keys - n_cur
    pairs = SWA_REP // 2
    return pl.pallas_call(
        _swa_kernel,
        grid=(B, nblk),
        in_specs=[pl.BlockSpec((TQ, SWA_N_HEADS * SWA_HEAD_DIM), lambda b, i: (b * nblk + i, 0)),
                  pl.BlockSpec((n_prev, kvw), prev_map[0]),
                  pl.BlockSpec((n_prev, kvw), prev_map[1]),
                  pl.BlockSpec((n_cur, kvw), cur_map[0]),
                  pl.BlockSpec((n_cur, kvw), cur_map[1]),
                  pl.BlockSpec((1,) + bias.shape[1:], lambda b, i: (jnp.where(i == 0, n_var - 1, 0), 0, 0, 0)),
                  pl.BlockSpec(memory_space=pltpu.SMEM)],
        out_specs=pl.BlockSpec((TQ, SWA_N_HEADS * SWA_HEAD_DIM), lambda b, i: (b * nblk + i, 0)),
        out_shape=jax.ShapeDtypeStruct((T, SWA_N_HEADS * SWA_HEAD_DIM), BF16),
        scratch_shapes=[pltpu.VMEM((pairs * TQ, 2 * n_keys), F32),
                        pltpu.VMEM((pairs * TQ, 2 * n_keys), BF16),
                        pltpu.VMEM((pairs * TQ, 2 * SWA_HEAD_DIM), F32)],
        compiler_params=_params(("parallel", "arbitrary")),
        name="swa_attention",
    )(q, k_prev_arr, v_prev_arr, k_cur_arr, v_cur_arr, bias, sinks.astype(F32))


def _mem_kernel(q_ref, mk_ref, mv_ref, o_ref):
    scale = MEM_HEAD_DIM ** -0.5
    ndims = (((1,), (1,)), ((), ()))
    for h in range(MEM_N_HEADS):
        hs = slice(h * MEM_HEAD_DIM, (h + 1) * MEM_HEAD_DIM)
        s = lax.dot_general(q_ref[:, hs], mk_ref[:, hs], ndims, preferred_element_type=F32) * scale
        m = jnp.max(s, axis=-1, keepdims=True)
        p = jnp.exp(s - m)
        den = jnp.sum(p, axis=-1, keepdims=True)
        o = jnp.dot(p.astype(BF16), mv_ref[:, hs], preferred_element_type=F32)
        o_ref[:, hs] = (o / den).astype(o_ref.dtype)


def _mem_attend(q, mk, mv, L, tm):
    T, W = q.shape
    M = mk.shape[0] // (T // L)
    return pl.pallas_call(
        _mem_kernel,
        grid=(T // tm,),
        in_specs=[pl.BlockSpec((tm, W), lambda i: (i, 0)),
                  pl.BlockSpec((M, W), lambda i: ((i * tm) // L, 0)),
                  pl.BlockSpec((M, W), lambda i: ((i * tm) // L, 0))],
        out_specs=pl.BlockSpec((tm, W), lambda i: (i, 0)),
        out_shape=jax.ShapeDtypeStruct((T, W), BF16),
        compiler_params=_params(("parallel",)),
        name="mem_attention",
    )(q, mk, mv)


def _merge_kernel(ys_ref, os_ref, om_ref, w1_ref, w2_ref, w3_ref, g0_ref, g1_ref, g2_ref, o_ref):
    a = jnp.dot(ys_ref[...], w1_ref[...], preferred_element_type=F32)
    b = jnp.dot(os_ref[...], w2_ref[...], preferred_element_type=F32)
    c = jnp.dot(om_ref[...], w3_ref[...], preferred_element_type=F32)
    o = (g0_ref[...].astype(F32) * a + g1_ref[...].astype(F32) * b + g2_ref[...].astype(F32) * c)
    o_ref[...] = o.astype(o_ref.dtype)


def _merge(y_ssd, o_s, o_m, w1, w2, w3, gates, tm, tn):
    T = y_ssd.shape[0]
    D = w1.shape[1]
    nj = D // tn
    row = lambda i, j: (i, 0)
    colw = lambda i, j: (0, j)
    return pl.pallas_call(
        _merge_kernel,
        grid=(T // tm, nj),
        in_specs=[pl.BlockSpec((tm, y_ssd.shape[1]), row),
                  pl.BlockSpec((tm, o_s.shape[1]), row),
                  pl.BlockSpec((tm, o_m.shape[1]), row),
                  pl.BlockSpec((w1.shape[0], tn), colw),
                  pl.BlockSpec((w2.shape[0], tn), colw),
                  pl.BlockSpec((w3.shape[0], tn), colw),
                  pl.BlockSpec((tm, tn), lambda i, j: (i, j)),
                  pl.BlockSpec((tm, tn), lambda i, j: (i, j + nj)),
                  pl.BlockSpec((tm, tn), lambda i, j: (i, j + 2 * nj))],
        out_specs=pl.BlockSpec((tm, tn), lambda i, j: (i, j)),
        out_shape=jax.ShapeDtypeStruct((T, D), BF16),
        compiler_params=_params(("parallel", "arbitrary")),
        name="gated_merge",
    )(y_ssd, o_s, o_m, w1, w2, w3, gates, gates, gates)


def _norm_route_kernel(xa_ref, xb_ref, g_ref, whi_ref, wlo_ref, h_ref, r_ref, *, n_a):
    x = jnp.where(pl.program_id(0) < n_a, xa_ref[...], xb_ref[...])
    ms = jnp.mean(x * x, axis=-1, keepdims=True)
    h = x * lax.rsqrt(ms + EPS) * g_ref[...]
    h_ref[:, 0, :] = h
    hb = h.astype(BF16)
    lo = (h - hb.astype(F32)).astype(BF16)
    lg = (jnp.dot(hb, whi_ref[...], preferred_element_type=F32)
          + jnp.dot(lo, whi_ref[...], preferred_element_type=F32)
          + jnp.dot(hb, wlo_ref[...], preferred_element_type=F32))
    lane = lax.broadcasted_iota(jnp.int32, lg.shape, 1)
    lane_f = lane.astype(F32)
    far = float(LANES)
    gl = jnp.where((lane >= N_EXPERTS) & (lane < N_EXPERTS + N_EXPERT_GROUPS), lg, NEG_INF)
    gmax = jnp.max(gl, axis=-1, keepdims=True)
    gidx = jnp.min(jnp.where(gl == gmax, lane_f - N_EXPERTS, far), axis=-1, keepdims=True)
    gw = 1.0 / jnp.sum(jnp.exp(gl - gmax), axis=-1, keepdims=True)
    lo_e = gidx * EXPERTS_PER_GROUP
    el = jnp.where((lane_f >= lo_e) & (lane_f < lo_e + EXPERTS_PER_GROUP), lg, NEG_INF)
    v1 = jnp.max(el, axis=-1, keepdims=True)
    i1 = jnp.min(jnp.where(el == v1, lane_f, far), axis=-1, keepdims=True)
    el2 = jnp.where(lane_f == i1, NEG_INF, el)
    v2 = jnp.max(el2, axis=-1, keepdims=True)
    i2 = jnp.min(jnp.where(el2 == v2, lane_f, far), axis=-1, keepdims=True)
    e = jnp.exp(v2 - v1)
    w1 = gw / (1.0 + e)
    w2 = gw * e / (1.0 + e)
    r_ref[...] = jnp.where(lane == 0, w1, jnp.where(lane == 1, w2,
                           jnp.where(lane == 2, i1, jnp.where(lane == 3, i2, 0.0))))


def _norm_route(xa, xb, g, w_hi, w_lo, tm):
    Ta, D = xa.shape
    Tb = xb.shape[0]
    T = Ta + Tb
    n_a, n_b = Ta // tm, Tb // tm
    assert n_a * tm == Ta and n_b * tm == Tb
    return pl.pallas_call(
        functools.partial(_norm_route_kernel, n_a=n_a),
        grid=(n_a + n_b,),
        in_specs=[pl.BlockSpec((tm, D), lambda i: (jnp.minimum(i, n_a - 1), 0)),
                  pl.BlockSpec((tm, D), lambda i: (jnp.maximum(i - n_a, 0), 0)),
                  pl.BlockSpec((1, D), lambda i: (0, 0)),
                  pl.BlockSpec((D, LANES), lambda i: (0, 0)),
                  pl.BlockSpec((D, LANES), lambda i: (0, 0))],
        out_specs=[pl.BlockSpec((tm, 1, D), lambda i: (i, 0, 0)),
                   pl.BlockSpec((tm, LANES), lambda i: (i, 0))],
        out_shape=[jax.ShapeDtypeStruct((T, 1, D), F32),
                   jax.ShapeDtypeStruct((T, LANES), F32)],
        compiler_params=_params(("parallel",)),
        name="ffn_norm_router",
    )(xa, xb, g.reshape(1, D).astype(F32), w_hi, w_lo)


def _moe_kernel(be_ref, nv_ref, s0_ref, ord_ref, h_hbm, w1_ref, w3_ref, w2_ref, yu_hbm,
                xbuf, ybuf, w1b, w3b, w2b, sem_in, sem_out, *, n_tok):
    i = pl.program_id(0)
    nb = pl.num_programs(0)
    rows = xbuf.shape[1]
    n_assign = TOP_K * n_tok
    nv = nv_ref[i]
    slot = i % 2
    nxt = jnp.minimum(i + 1, nb - 1)
    next_valid = (i + 1 < nb) & (nv_ref[nxt] > 0)

    def assignment(blk, r):
        return ord_ref[s0_ref[blk] + r]

    def row_in(s, r, tok):
        return pltpu.make_async_copy(h_hbm.at[tok], xbuf.at[s, pl.ds(r, 1), :], sem_in.at[s])

    def row_out(s, r, dst):
        return pltpu.make_async_copy(ybuf.at[s, pl.ds(r, 1), :], yu_hbm.at[dst], sem_out.at[s])

    def gather_start(blk, s):
        for r in range(rows):
            row_in(s, r, lax.shift_right_logical(assignment(blk, r), 1)).start()

    def gather_wait(s):
        for r in range(rows):
            row_in(s, r, 0).wait()

    def scatter_start(blk, s):
        n_valid = nv_ref[blk]
        for r in range(rows):
            m = assignment(blk, r)
            dst = jnp.where(r < n_valid, (m & 1) * n_tok + lax.shift_right_logical(m, 1),
                            n_assign + s * rows + r)
            row_out(s, r, dst).start()

    def scatter_wait(s):
        for r in range(rows):
            row_out(s, r, 0).wait()

    @pl.when(i == 0)
    def _clear_spare_rows():
        ybuf[...] = jnp.zeros(ybuf.shape, ybuf.dtype)
        for s in range(2):
            for r in range(rows):
                row_out(s, r, n_assign + s * rows + r).start()
            scatter_wait(s)

    @pl.when((i == 0) & (nv > 0))
    def _prologue():
        gather_start(i, 0)

    @pl.when(next_valid)
    def _prefetch():
        gather_start(nxt, 1 - slot)

    prev = be_ref[jnp.maximum(i - 1, 0)]

    @pl.when((i == 0) | (be_ref[i] != prev))
    def _load_expert():
        w1b[...] = w1_ref[0].astype(BF16)
        w3b[...] = w3_ref[0].astype(BF16)
        w2b[...] = w2_ref[0].astype(BF16)

    @pl.when(nv > 0)
    def _compute():
        gather_wait(slot)

        @pl.when(i >= 2)
        def _free_ybuf():
            scatter_wait(slot)

        x = xbuf[slot].astype(BF16)
        a = jnp.dot(x, w1b[...], preferred_element_type=F32)
        b = jnp.dot(x, w3b[...], preferred_element_type=F32)
        mid = (_silu(a) * b).astype(BF16)
        y = jnp.dot(mid, w2b[...], preferred_element_type=F32)
        ybuf[slot] = y
        scatter_start(i, slot)

        @pl.when(jnp.logical_not(next_valid))
        def _drain():
            scatter_wait(slot)

            @pl.when(i >= 1)
            def _drain_prev():
                scatter_wait(1 - slot)


def _moe(h2, route, w_gate, w_up, w_down):
    T, _, D = h2.shape
    F = w_gate.shape[2]
    assert TOP_K == 2
    M = T * TOP_K
    BM = MOE_ROWS
    nb = (M + N_EXPERTS * (BM - 1) + BM - 1) // BM
    e_flat = route[:, TOP_K:2 * TOP_K].astype(jnp.int32).reshape(M)
    idx_bits = (M - 1).bit_length()
    assert (N_EXPERTS << idx_bits) < 2 ** 31
    packed = lax.sort(e_flat * (1 << idx_bits) + jnp.arange(M, dtype=jnp.int32), is_stable=False)
    order = jnp.pad(packed & ((1 << idx_bits) - 1), (0, BM))
    experts = jnp.arange(N_EXPERTS, dtype=jnp.int32)
    counts = jnp.sum((experts[:, None] == e_flat[None, :]).astype(jnp.int32), axis=1)
    padded = (counts + BM - 1) // BM * BM
    pad_end = jnp.cumsum(padded)
    pad_start = pad_end - padded
    start = jnp.cumsum(counts) - counts
    blk0 = jnp.arange(nb, dtype=jnp.int32) * BM
    blk_exp = jnp.minimum(jnp.sum((pad_end[None, :] <= blk0[:, None]).astype(jnp.int32), axis=1), N_EXPERTS - 1)
    pick = (blk_exp[:, None] == experts[None, :]).astype(jnp.int32)
    off0 = blk0 - jnp.sum(pick * pad_start[None, :], axis=1)
    blk_nv = jnp.clip(jnp.sum(pick * counts[None, :], axis=1) - off0, 0, BM).astype(jnp.int32)
    blk_s0 = jnp.clip(jnp.sum(pick * start[None, :], axis=1) + off0, 0, M - 1).astype(jnp.int32)

    grid_spec = pltpu.PrefetchScalarGridSpec(
        num_scalar_prefetch=4,
        grid=(nb,),
        in_specs=[pl.BlockSpec(memory_space=pl.ANY),
                  pl.BlockSpec((1, D, F), lambda i, be, nv, s0, od: (be[i], 0, 0)),
                  pl.BlockSpec((1, D, F), lambda i, be, nv, s0, od: (be[i], 0, 0)),
                  pl.BlockSpec((1, F, D), lambda i, be, nv, s0, od: (be[i], 0, 0))],
        out_specs=pl.BlockSpec(memory_space=pl.ANY),
        scratch_shapes=[pltpu.VMEM((2, BM, D), F32),
                        pltpu.VMEM((2, BM, D), F32),
                        pltpu.VMEM((D, F), BF16),
                        pltpu.VMEM((D, F), BF16),
                        pltpu.VMEM((F, D), BF16),
                        pltpu.SemaphoreType.DMA((2,)),
                        pltpu.SemaphoreType.DMA((2,))],
    )
    return pl.pallas_call(
        functools.partial(_moe_kernel, n_tok=T),
        grid_spec=grid_spec,
        out_shape=jax.ShapeDtypeStruct((M + 2 * BM, 1, D), F32),
        compiler_params=_params(("arbitrary",)),
        name="moe_experts",
    )(blk_exp.astype(jnp.int32), blk_nv, blk_s0, order, h2, w_gate, w_up, w_down)


def _combine_kernel(x_ref, r_ref, y0_ref, y1_ref, o_ref):
    w0 = r_ref[:, 0:1]
    w1 = r_ref[:, 1:2]
    o_ref[...] = x_ref[...] + (w0 * y0_ref[:, 0, :] + w1 * y1_ref[:, 0, :])


def _combine(x, route, yu, blk0, tm):
    T, D = x.shape
    n_all = route.shape[0] // tm
    return pl.pallas_call(
        _combine_kernel,
        grid=(T // tm,),
        in_specs=[pl.BlockSpec((tm, D), lambda i: (i, 0)),
                  pl.BlockSpec((tm, LANES), lambda i: (blk0 + i, 0)),
                  pl.BlockSpec((tm, 1, D), lambda i: (blk0 + i, 0, 0)),
                  pl.BlockSpec((tm, 1, D), lambda i: (n_all + blk0 + i, 0, 0))],
        out_specs=pl.BlockSpec((tm, D), lambda i: (i, 0)),
        out_shape=jax.ShapeDtypeStruct((T, D), F32),
        compiler_params=_params(("parallel",)),
        name="moe_combine",
    )(x, route, yu, yu)


def _layer(x, B, L, p, conv_prev8, h0_t, mk, mv, swa_prev, swa_bias, Lc, TQ):
    T, D = x.shape
    tmm = min(MM_ROWS, T)
    h = _rmsnorm(x, p["g_mix"], min(512, T))
    z = _matmul(h, p["wz"], BF16, tmm, MM_COLS_WIDE, name="proj_z")
    xbc = _matmul(h, p["wxbc"], BF16, tmm, MM_COLS_WIDE, name="proj_xbc")
    q_s = _matmul(h, p["wqs"], BF16, tmm, MM_COLS_WIDE, epi=_epi_group_norm, col_extras=(p["g_q_swa"],),
                  const_extras=(p["mavg"],), name="proj_q_swa")
    kvw = SWA_N_KV * SWA_HEAD_DIM
    kvd = _matmul(h, p["wkvdt"], F32, tmm, 2 * kvw + PAD_HEADS, epi=_epi_kv_dt, const_extras=(p["g_k"], p["mavg"]),
                  name="proj_kv_dt")
    kv = kvd
    q_m = _matmul(h, p["wqm"], BF16, tmm, MM_COLS_WIDE, epi=_epi_row_norm, col_extras=(p["g_q_mem"],),
                  name="proj_q_mem")
    gates = _matmul(h, p["wg"], BF16, tmm, MM_COLS_WIDE, epi=_epi_sigmoid, name="proj_gates")

    y_ssd, h_t, conv8 = _ssd(z, xbc, kvd, (2 * kvw) // PAD_HEADS, conv_prev8, h0_t, p, B, L, Lc)

    nblk = L // TQ
    if swa_prev is None:
        prev_map = (lambda b, i: (b * nblk + jnp.maximum(i - 1, 0), 0),
                    lambda b, i: (b * nblk + jnp.maximum(i - 1, 0), 1))
        cur_map = (lambda b, i: (b * nblk + i, 0), lambda b, i: (b * nblk + i, 1))
        o_s = _swa(q_s, kv, kv, prev_map, kv, kv, cur_map, TQ, swa_bias, p["sinks"], B, L, TQ)
    else:
        n_cur = swa_bias.shape[3] // 2 - swa_prev[0].shape[0] // B
        kv3 = jnp.pad(kv[:, :2 * kvw].reshape(B, L, 2 * kvw), ((0, 0), (0, n_cur - L), (0, 0)))
        k_new = kv3[:, :, :kvw].reshape(B * n_cur, kvw)
        v_new = kv3[:, :, kvw:].reshape(B * n_cur, kvw)
        per_b = (lambda b, i: (b, 0), lambda b, i: (b, 0))
        o_s = _swa(q_s, swa_prev[0], swa_prev[1], per_b, k_new, v_new, per_b, n_cur, swa_bias, p["sinks"],
                   B, L, TQ)

    o_m = _mem_attend(q_m, mk, mv, L, min(512, L))

    merged = _merge(y_ssd, o_s, o_m, p["w_o_ssd"], p["w_o_swa"], p["w_o_mem"], gates, tmm, MERGE_COLS)
    x1 = _matmul(merged, p["w_out"], F32, min(MM_ROWS // 2, T), MM_COLS_WIDE, epi=_epi_residual, tile_extras=(x,),
                 name="proj_out")

    return x1, conv8, h_t, kv


def _state_to_heads(h_t, B):
    return jnp.transpose(h_t.reshape(B, SSD_D_STATE, SSD_N_HEADS, SSD_HEAD_DIM), (0, 2, 3, 1))


def kernel(x_prompt, x_sample, cache_conv, state_ssd, cache_swa_k, cache_swa_v, cache_mem_k, cache_mem_v, mem_prompt, rel_bias_table, g_mix, w_in, conv_w, conv_b, dt_bias, a_log, d_skip, g_ssd, w_o_ssd, g_q_swa, g_k_swa, sinks, w_o_swa, g_mem, w_mem_k, w_mem_v, g_q_mem, g_k_mem, w_o_mem, w_out, g_ffn, w_router_grp, w_router_exp, w_exp_gate, w_exp_up, w_exp_down):
    B, S, D = x_prompt.shape
    Bd, Sd, _ = x_sample.shape
    depth = w_in.shape[0]
    assert depth == 1
    l = 0
    kvw = SWA_N_KV * SWA_HEAD_DIM
    qw = SWA_N_HEADS * SWA_HEAD_DIM
    mw = MEM_N_HEADS * MEM_HEAD_DIM

    sizes = (SSD_D_INNER, SSD_CONV_DIM, SSD_N_HEADS, qw, kvw, kvw, mw, 3 * D)
    offs = np.concatenate([[0], np.cumsum(sizes)])
    w = w_in[l]
    cols = [w[:, int(offs[k]):int(offs[k + 1])] for k in range(len(sizes))]
    pad_h = PAD_HEADS - SSD_N_HEADS
    w_r = jnp.pad(jnp.concatenate([w_router_exp[l], w_router_grp[l]], axis=1),
                  ((0, 0), (0, LANES - N_EXPERTS - N_EXPERT_GROUPS)))
    w_r_hi = w_r.astype(BF16)
    p = {
        "g_mix": g_mix[l],
        "wz": cols[0].astype(BF16),
        "wxbc": cols[1].astype(BF16),
        "wkvdt": jnp.concatenate([cols[4], cols[5], jnp.pad(cols[2], ((0, 0), (0, pad_h)))], axis=1).astype(BF16),
        "wqs": cols[3].astype(BF16),
        "wqm": cols[6].astype(BF16),
        "wg": cols[7].astype(BF16),
        "conv_w": conv_w[l].astype(F32),
        "conv_b": conv_b[l].reshape(1, SSD_CONV_DIM).astype(F32),
        "dt_bias": jnp.pad(dt_bias[l], (0, pad_h)).reshape(1, PAD_HEADS).astype(F32),
        "a_log": jnp.pad(a_log[l], (0, pad_h)).reshape(1, PAD_HEADS).astype(F32),
        "d_skip": jnp.repeat(d_skip[l], SSD_HEAD_DIM).reshape(1, SSD_D_INNER).astype(F32),
        "g_ssd": g_ssd[l].reshape(1, SSD_D_INNER).astype(F32),
        "g_q_swa": (jnp.tile(g_q_swa[l], SWA_N_HEADS) * SWA_HEAD_DIM ** -0.5).reshape(1, qw).astype(F32),
        "g_k": jnp.tile(g_k_swa[l], SWA_N_KV).reshape(1, kvw).astype(F32),
        "g_q_mem": jnp.tile(g_q_mem[l], MEM_N_HEADS).reshape(1, mw).astype(F32),
        "mavg": _group_mean_matrix(256, SWA_HEAD_DIM),
        "sinks": sinks[l],
        "w_o_ssd": w_o_ssd[l].astype(BF16),
        "w_o_swa": w_o_swa[l].astype(BF16),
        "w_o_mem": w_o_mem[l].astype(BF16),
        "w_out": w_out[l].astype(BF16),
        "g_ffn": g_ffn[l],
        "w_r_hi": w_r_hi,
        "w_r_lo": (w_r - w_r_hi.astype(F32)).astype(BF16),
        "w_exp_gate": w_exp_gate[l],
        "w_exp_up": w_exp_up[l],
        "w_exp_down": w_exp_down[l],
    }

    M = mem_prompt.shape[1]
    mn = _rmsnorm(mem_prompt.reshape(B * M, D), g_mem[l], min(256, B * M))
    mk_p = _matmul(mn, w_mem_k[l].astype(BF16), F32, min(256, B * M), MEM_HEAD_DIM, epi=_epi_row_norm,
                   col_extras=(jnp.tile(g_k_mem[l], MEM_N_HEADS).reshape(1, mw).astype(F32),), name="mem_k")
    mv_p = _matmul(mn, w_mem_v[l].astype(BF16), F32, min(256, B * M), MEM_HEAD_DIM, name="mem_v")

    TQ = 2 * CHUNK
    qpos = np.arange(TQ)
    prev_p = _bias_from_table(_bucket_map(qpos, np.arange(TQ) - TQ), rel_bias_table)
    cur_p = _bias_from_table(_bucket_map(qpos, np.arange(TQ)), rel_bias_table)
    bias_p = jnp.stack([_pair_bias(prev_p, cur_p), _pair_bias(jnp.full_like(prev_p, NEG_INF), cur_p)])
    C = cache_swa_k.shape[2]
    qpos_s = PAST_LEN + np.arange(Sd)
    cur_map_s = _bucket_map(qpos_s, PAST_LEN + np.arange(C))
    cur_map_s[:, Sd:] = -1
    bias_s = _pair_bias(_bias_from_table(_bucket_map(qpos_s, PAST_LEN - C + np.arange(C)), rel_bias_table),
                        _bias_from_table(cur_map_s, rel_bias_table))[None]

    conv0 = jnp.zeros((B, SUBLANES, SSD_CONV_DIM), F32)
    h0 = jnp.zeros((B, SSD_D_STATE, SSD_D_INNER), F32)
    x1_p, conv8_p, ht_p, kv_p = _layer(x_prompt.reshape(B * S, D), B, S, p, conv0, h0,
                                     mk_p.astype(BF16), mv_p.astype(BF16), None,
                                     bias_p, CHUNK, TQ)
    conv_prev = jnp.pad(cache_conv[l], ((0, 0), (SUBLANES - (SSD_CONV - 1), 0), (0, 0)))
    h0_s = jnp.transpose(state_ssd[l], (0, 3, 1, 2)).reshape(Bd, SSD_D_STATE, SSD_D_INNER)
    x1_s, conv8_s, ht_s, kv_s = _layer(x_sample.reshape(Bd * Sd, D), Bd, Sd, p, conv_prev, h0_s,
                                     cache_mem_k[l].reshape(Bd * M, mw).astype(BF16),
                                     cache_mem_v[l].reshape(Bd * M, mw).astype(BF16),
                                     (cache_swa_k[l].reshape(Bd * C, kvw), cache_swa_v[l].reshape(Bd * C, kvw)),
                                     bias_s, Sd, Sd)

    tr = math.gcd(B * S, Bd * Sd, MOE_ROWS)
    h2, route = _norm_route(x1_p, x1_s, p["g_ffn"], p["w_r_hi"], p["w_r_lo"], tr)
    yu = _moe(h2, route, p["w_exp_gate"], p["w_exp_up"], p["w_exp_down"])
    yp = _combine(x1_p, route, yu, 0, tr)
    ys = _combine(x1_s, route, yu, (B * S) // tr, tr)

    keep = min(WINDOW, S)
    kv_p = kv_p.reshape(B, S, -1)[:, S - keep:, :2 * kvw].reshape(B, keep, 2, SWA_N_KV, SWA_HEAD_DIM)
    kv_s = kv_s[:, :2 * kvw].reshape(Bd, Sd, 2, SWA_N_KV, SWA_HEAD_DIM)
    tail = SUBLANES - (SSD_CONV - 1)
    return (yp.reshape(B, S, D), ys.reshape(Bd, Sd, D),
            conv8_p[None, :, tail:], _state_to_heads(ht_p, B)[None],
            kv_p[None, :, :, 0], kv_p[None, :, :, 1],
            mk_p.reshape(1, B, M, MEM_N_HEADS, MEM_HEAD_DIM), mv_p.reshape(1, B, M, MEM_N_HEADS, MEM_HEAD_DIM),
            conv8_s[None, :, tail:], _state_to_heads(ht_s, Bd)[None],
            kv_s[None, :, :, 0], kv_s[None, :, :, 1])
```

```python
import functools
import math

import numpy as np
import jax
import jax.numpy as jnp
from jax import lax
from jax.experimental import pallas as pl
from jax.experimental.pallas import tpu as pltpu

F32 = jnp.float32
BF16 = jnp.bfloat16
EPS = 1e-6
NEG_INF = float("-inf")

CHUNK = 64
SSD_HEAD_DIM = 64
SSD_N_HEADS = 64
SSD_N_GROUPS = 8
SSD_D_STATE = 128
SSD_D_INNER = SSD_N_HEADS * SSD_HEAD_DIM
SSD_GN = SSD_N_GROUPS * SSD_D_STATE
SSD_CONV_DIM = SSD_D_INNER + 2 * SSD_GN
SSD_CONV = 4
SWA_N_HEADS = 32
SWA_N_KV = 4
SWA_HEAD_DIM = 64
SWA_REP = SWA_N_HEADS // SWA_N_KV
WINDOW = 128
WINDOW_CHUNKS = WINDOW // CHUNK
MEM_N_HEADS = 4
MEM_HEAD_DIM = 512
N_BUCKETS = 32
MAX_DISTANCE = 128
N_EXPERT_GROUPS = 4
EXPERTS_PER_GROUP = 16
N_EXPERTS = N_EXPERT_GROUPS * EXPERTS_PER_GROUP
TOP_K = 2
PAST_LEN = 4096

LANES = 128
SUBLANES = 8
VMEM_LIMIT = 56 * 1024 * 1024
MOE_ROWS = 256
MM_ROWS = 1024
MM_COLS = 1024
MM_COLS_WIDE = 2048
MERGE_COLS = 256
PAD_HEADS = LANES


def _params(sem, vmem=VMEM_LIMIT):
    return pltpu.CompilerParams(dimension_semantics=sem, vmem_limit_bytes=vmem)


def _sigmoid(x):
    return 0.5 * (jnp.tanh(0.5 * x) + 1.0)


def _silu(x):
    u = 0.5 * x
    return u * (jnp.tanh(u) + 1.0)


def _split3(x):
    x1 = x.astype(BF16)
    r1 = x - x1.astype(F32)
    x2 = r1.astype(BF16)
    x3 = (r1 - x2.astype(F32)).astype(BF16)
    return x1, x2, x3


def _rmsnorm_kernel(x_ref, g_ref, o_ref):
    x = x_ref[...]
    ms = jnp.mean(x * x, axis=-1, keepdims=True)
    o_ref[...] = (x * lax.rsqrt(ms + EPS) * g_ref[...]).astype(o_ref.dtype)


def _rmsnorm(x, g, tm):
    T, D = x.shape
    return pl.pallas_call(
        _rmsnorm_kernel,
        grid=(T // tm,),
        in_specs=[pl.BlockSpec((tm, D), lambda i: (i, 0)),
                  pl.BlockSpec((1, D), lambda i: (0, 0))],
        out_specs=pl.BlockSpec((tm, D), lambda i: (i, 0)),
        out_shape=jax.ShapeDtypeStruct((T, D), BF16),
        compiler_params=_params(("parallel",)),
        name="rmsnorm",
    )(x, g.reshape(1, D).astype(F32))


def _mm_kernel(a_ref, b_ref, *refs, epi):
    o_ref = refs[-1]
    acc = jnp.dot(a_ref[...], b_ref[...], preferred_element_type=F32)
    if epi is not None:
        acc = epi(acc, *[r[...] for r in refs[:-1]])
    o_ref[...] = acc.astype(o_ref.dtype)


def _matmul(a, b, out_dtype, tm, tn, epi=None, col_extras=(), tile_extras=(), const_extras=(), name="matmul"):
    M, K = a.shape
    N = b.shape[1]
    assert M % tm == 0 and N % tn == 0, (M, N, tm, tn)
    in_specs = [pl.BlockSpec((tm, K), lambda i, j: (i, 0)),
                pl.BlockSpec((K, tn), lambda i, j: (0, j))]
    for _ in col_extras:
        in_specs.append(pl.BlockSpec((1, tn), lambda i, j: (0, j)))
    for _ in tile_extras:
        in_specs.append(pl.BlockSpec((tm, tn), lambda i, j: (i, j)))
    for c in const_extras:
        in_specs.append(pl.BlockSpec(c.shape, lambda i, j: (0, 0)))
    return pl.pallas_call(
        functools.partial(_mm_kernel, epi=epi),
        grid=(M // tm, N // tn),
        in_specs=in_specs,
        out_specs=pl.BlockSpec((tm, tn), lambda i, j: (i, j)),
        out_shape=jax.ShapeDtypeStruct((M, N), out_dtype),
        compiler_params=_params(("parallel", "arbitrary")),
        name=name,
    )(a, b, *col_extras, *tile_extras, *const_extras)


def _group_mean_matrix(width, group):
    idx = np.arange(width) // group
    return jnp.asarray((idx[:, None] == idx[None, :]).astype(np.float32) / group, dtype=BF16)


def _epi_group_norm(acc, gain, mavg):
    w = mavg.shape[0]
    outs = []
    for c in range(acc.shape[1] // w):
        a = acc[:, c * w:(c + 1) * w]
        s = a * a
        hi = s.astype(BF16)
        lo = (s - hi.astype(F32)).astype(BF16)
        ms = (jnp.dot(hi, mavg, preferred_element_type=F32)
              + jnp.dot(lo, mavg, preferred_element_type=F32))
        outs.append(a * lax.rsqrt(ms + EPS))
    normed = outs[0] if len(outs) == 1 else jnp.concatenate(outs, axis=1)
    return normed * gain


def _epi_kv_dt(acc, gain, mavg):
    kw = mavg.shape[0]
    return jnp.concatenate([_epi_group_norm(acc[:, :kw], gain, mavg), acc[:, kw:]], axis=1)


def _epi_row_norm(acc, gain):
    outs = []
    for c in range(acc.shape[1] // MEM_HEAD_DIM):
        a = acc[:, c * MEM_HEAD_DIM:(c + 1) * MEM_HEAD_DIM]
        outs.append(a * lax.rsqrt(jnp.mean(a * a, axis=-1, keepdims=True) + EPS))
    normed = outs[0] if len(outs) == 1 else jnp.concatenate(outs, axis=1)
    return normed * gain


def _epi_sigmoid(acc):
    return _sigmoid(acc)


def _epi_residual(acc, res):
    return acc + res


def _ssd_kernel(z_ref, xbc_ref, dt_ref, cprev_ref, h0_ref, cw_ref, cb_ref, dtb_ref, alog_ref,
                dskip_ref, gn_ref, y_ref, hout_ref, cout_ref, xp_s, h_s, conv_s, y_s, *, Lc):
    c = pl.program_id(1)
    n_chunks = pl.num_programs(1)
    P2 = 2 * SSD_HEAD_DIM
    L2 = 2 * Lc

    @pl.when(c == 0)
    def _init():
        xp_s[0:SUBLANES, :] = cprev_ref[0]
        h_s[...] = h0_ref[0]

    xp_s[SUBLANES:SUBLANES + Lc, :] = xbc_ref[...].astype(F32)
    cblk = 512
    row8 = lax.broadcasted_iota(jnp.int32, (SUBLANES, cblk), 0)
    for j in range(SSD_CONV_DIM // cblk):
        sl = slice(j * cblk, (j + 1) * cblk)
        cur = xp_s[SUBLANES:SUBLANES + Lc, sl]
        tail = xp_s[0:SUBLANES, sl]
        acc = cb_ref[:, sl] + cw_ref[SSD_CONV - 1:SSD_CONV, sl] * cur
        for k in range(SSD_CONV - 1):
            sh = SSD_CONV - 1 - k
            down = pltpu.roll(cur, sh, axis=0)
            top = jnp.where(row8 >= sh, down[0:SUBLANES], pltpu.roll(tail, sh, axis=0))
            acc = acc + cw_ref[k:k + 1, sl] * jnp.concatenate([top, down[SUBLANES:]], axis=0)
        conv_s[:, sl] = _silu(acc)
    xp_s[0:SUBLANES, :] = xp_s[Lc:Lc + SUBLANES, :]

    dtv = dt_ref[...] + dtb_ref[...]
    dt = jnp.maximum(dtv, 0.0) + jnp.log1p(jnp.exp(-jnp.abs(dtv)))
    adt = dt * (-jnp.exp(alog_ref[...]))
    row = lax.broadcasted_iota(jnp.int32, (Lc, Lc), 0)
    col = lax.broadcasted_iota(jnp.int32, (Lc, Lc), 1)
    tri = (col <= row).astype(BF16)
    row2 = lax.broadcasted_iota(jnp.int32, (Lc, L2), 0)
    col2 = lax.broadcasted_iota(jnp.int32, (Lc, L2), 1)
    col2m = jnp.where(col2 >= Lc, col2 - Lc, col2)
    tri_t2 = (row2 <= col2m).astype(BF16)
    causal2 = col2m <= row2
    a1, a2, a3 = _split3(adt)
    acs = (jnp.dot(tri, a1, preferred_element_type=F32)
           + jnp.dot(tri, a2, preferred_element_type=F32)
           + jnp.dot(tri, a3, preferred_element_type=F32))
    tdims = (((0,), (0,)), ((), ()))
    acs_t2 = (lax.dot_general(a1, tri_t2, tdims, preferred_element_type=F32)
              + lax.dot_general(a2, tri_t2, tdims, preferred_element_type=F32)
              + lax.dot_general(a3, tri_t2, tdims, preferred_element_type=F32))

    lane_p = lax.broadcasted_iota(jnp.int32, (Lc, P2), 1)
    first_p = lane_p < SSD_HEAD_DIM
    first_l = col2 < Lc
    first_l1 = first_l[0:1, :]
    ndims = (((1,), (1,)), ((), ()))

    for g in range(SSD_N_GROUPS):
        b_g = conv_s[:, SSD_D_INNER + g * SSD_D_STATE:SSD_D_INNER + (g + 1) * SSD_D_STATE].astype(BF16)
        c_g = conv_s[:, SSD_D_INNER + SSD_GN + g * SSD_D_STATE:
                     SSD_D_INNER + SSD_GN + (g + 1) * SSD_D_STATE].astype(BF16)
        b2 = jnp.concatenate([b_g, b_g], axis=0)
        cb2 = lax.dot_general(c_g, b2, ndims, preferred_element_type=F32)
        gw = SSD_HEAD_DIM * (SSD_N_HEADS // SSD_N_GROUPS)
        inter = jnp.dot(c_g, h_s[:, g * gw:(g + 1) * gw].astype(BF16), preferred_element_type=F32)
        for jj in range(gw // P2):
            j = g * (gw // P2) + jj
            sl = slice(j * P2, (j + 1) * P2)
            acs_a = acs[:, 2 * j:2 * j + 1]
            acs_b = acs[:, 2 * j + 1:2 * j + 2]
            col_l = jnp.where(first_l, acs_a, acs_b)
            row_l = jnp.where(first_l1, acs_t2[2 * j:2 * j + 1, :], acs_t2[2 * j + 1:2 * j + 2, :])
            dec = jnp.exp(jnp.where(causal2, col_l - row_l, NEG_INF))
            m_pair = (cb2 * dec).astype(BF16)
            col_p = col_l if L2 == P2 else jnp.where(first_p, acs_a, acs_b)
            dt_p = jnp.where(first_p, dt[:, 2 * j:2 * j + 1], dt[:, 2 * j + 1:2 * j + 2])
            xs_p = conv_s[:, sl]
            xdt = xs_p * dt_p
            rhs = jnp.concatenate([jnp.where(first_p, xdt, 0.0), jnp.where(first_p, 0.0, xdt)],
                                  axis=0).astype(BF16)
            y = jnp.dot(m_pair, rhs, preferred_element_type=F32)
            y = y + inter[:, jj * P2:(jj + 1) * P2] * jnp.exp(col_p) + dskip_ref[:, sl] * xs_p
            y_s[:, sl] = y
            a_end = col_p[Lc - 1:Lc, :]
            xw = (xdt * jnp.exp(a_end - col_p)).astype(BF16)
            h_s[:, sl] = (h_s[:, sl] * jnp.exp(a_end)
                          + lax.dot_general(b_g, xw, tdims, preferred_element_type=F32))

    gdim = SSD_D_INNER // SSD_N_GROUPS
    for g in range(SSD_N_GROUPS):
        sl = slice(g * gdim, (g + 1) * gdim)
        zz = z_ref[:, sl].astype(F32)
        yy = y_s[:, sl] * _silu(zz)
        ms = jnp.mean(yy * yy, axis=-1, keepdims=True)
        y_ref[:, sl] = (yy * lax.rsqrt(ms + EPS) * gn_ref[:, sl]).astype(y_ref.dtype)

    @pl.when(c == n_chunks - 1)
    def _fin():
        hout_ref[0] = h_s[...]
        cout_ref[0] = xp_s[0:SUBLANES, :]


def _ssd(z, xbc, dt, dt_col, conv_prev8, h0_t, p, B, L, Lc):
    T = B * L
    nc = L // Lc
    tok = lambda b, c: (b * nc + c, 0)
    tok_dt = lambda b, c: (b * nc + c, dt_col)
    per_b = lambda b, c: (b, 0, 0)
    whole = lambda b, c: (0, 0)
    y, h_t, conv8 = pl.pallas_call(
        functools.partial(_ssd_kernel, Lc=Lc),
        grid=(B, nc),
        in_specs=[pl.BlockSpec((Lc, SSD_D_INNER), tok),
                  pl.BlockSpec((Lc, SSD_CONV_DIM), tok),
                  pl.BlockSpec((Lc, PAD_HEADS), tok_dt),
                  pl.BlockSpec((1, SUBLANES, SSD_CONV_DIM), per_b),
                  pl.BlockSpec((1, SSD_D_STATE, SSD_D_INNER), per_b),
                  pl.BlockSpec((SSD_CONV, SSD_CONV_DIM), whole),
                  pl.BlockSpec((1, SSD_CONV_DIM), whole),
                  pl.BlockSpec((1, PAD_HEADS), whole),
                  pl.BlockSpec((1, PAD_HEADS), whole),
                  pl.BlockSpec((1, SSD_D_INNER), whole),
                  pl.BlockSpec((1, SSD_D_INNER), whole)],
        out_specs=[pl.BlockSpec((Lc, SSD_D_INNER), tok),
                   pl.BlockSpec((1, SSD_D_STATE, SSD_D_INNER), per_b),
                   pl.BlockSpec((1, SUBLANES, SSD_CONV_DIM), per_b)],
        out_shape=[jax.ShapeDtypeStruct((T, SSD_D_INNER), BF16),
                   jax.ShapeDtypeStruct((B, SSD_D_STATE, SSD_D_INNER), F32),
                   jax.ShapeDtypeStruct((B, SUBLANES, SSD_CONV_DIM), F32)],
        scratch_shapes=[pltpu.VMEM((SUBLANES + Lc, SSD_CONV_DIM), F32),
                        pltpu.VMEM((SSD_D_STATE, SSD_D_INNER), F32),
                        pltpu.VMEM((Lc, SSD_CONV_DIM), F32),
                        pltpu.VMEM((Lc, SSD_D_INNER), F32)],
        compiler_params=_params(("arbitrary", "arbitrary")),
        name="ssd_scan",
    )(z, xbc, dt, conv_prev8, h0_t, p["conv_w"], p["conv_b"], p["dt_bias"], p["a_log"],
      p["d_skip"], p["g_ssd"])
    return y, h_t, conv8


def _bias_kernel(idx_ref, tab_ref, o_ref):
    h = pl.program_id(0)
    idx = idx_ref[...]
    acc = jnp.full(idx.shape, NEG_INF, F32)
    for b in range(N_BUCKETS):
        acc = jnp.where(idx == b, tab_ref[b, h], acc)
    o_ref[0] = acc


def _bias_from_table(idx, table):
    Q, K = idx.shape
    return pl.pallas_call(
        _bias_kernel,
        grid=(SWA_N_HEADS,),
        in_specs=[pl.BlockSpec((Q, K), lambda h: (0, 0)),
                  pl.BlockSpec(memory_space=pltpu.SMEM)],
        out_specs=pl.BlockSpec((1, Q, K), lambda h: (h, 0, 0)),
        out_shape=jax.ShapeDtypeStruct((SWA_N_HEADS, Q, K), F32),
        compiler_params=_params(("arbitrary",)),
        name="rel_bias",
    )(jnp.asarray(idx, jnp.int32), table.astype(F32))


def _t5_bucket_np(rel):
    nb = N_BUCKETS // 2
    max_exact = nb // 2
    ret = np.where(rel > 0, nb, 0)
    n = np.abs(rel)
    nf = np.maximum(n, 1).astype(np.float32)
    large = max_exact + (np.log(nf / np.float32(max_exact)) / np.float32(math.log(MAX_DISTANCE / max_exact))
                         * np.float32(nb - max_exact)).astype(np.int32)
    large = np.minimum(large, nb - 1)
    return (ret + np.where(n < max_exact, n, large)).astype(np.int32)


def _bucket_map(q_pos, k_pos):
    qc, kc = q_pos // CHUNK, k_pos // CHUNK
    valid = (kc[None, :] >= qc[:, None] - WINDOW_CHUNKS) & (kc[None, :] <= qc[:, None])
    return np.where(valid, _t5_bucket_np(k_pos[None, :] - q_pos[:, None]), -1).astype(np.int32)


def _swa_kernel(q_ref, kp_ref, vp_ref, kc_ref, vc_ref, bias_ref, sink_ref, o_ref, s_scr, p_scr, t_scr):
    TQ = q_ref.shape[0]
    NP = kp_ref.shape[0]
    NK = NP + kc_ref.shape[0]
    HD = SWA_HEAD_DIM
    PW = 2 * HD
    pairs = SWA_REP // 2
    ndims = (((1,), (1,)), ((), ()))
    first_o = lax.broadcasted_iota(jnp.int32, (TQ, PW), 1) < HD
    first_s = lax.broadcasted_iota(jnp.int32, (TQ, 2 * NK), 1) < NK
    zero = jnp.zeros((NK, HD), BF16)
    one = jnp.ones((NK, HD), BF16)
    for g in range(SWA_N_KV):
        ks = slice(g * HD, (g + 1) * HD)
        k_g = jnp.concatenate([kp_ref[:, ks], kc_ref[:, ks]], axis=0).astype(BF16)
        v_g = jnp.concatenate([vp_ref[:, ks], vc_ref[:, ks]], axis=0).astype(BF16)
        kk = jnp.concatenate([jnp.concatenate([k_g, zero], axis=1),
                              jnp.concatenate([zero, k_g], axis=1)], axis=0)
        vv = jnp.concatenate([jnp.concatenate([v_g, zero, one, zero], axis=1),
                              jnp.concatenate([zero, v_g, zero, one], axis=1)], axis=0)
        for pr in range(pairs):
            pidx = g * pairs + pr
            s_scr[pr * TQ:(pr + 1) * TQ, :] = (
                lax.dot_general(q_ref[:, pidx * PW:(pidx + 1) * PW], kk, ndims, preferred_element_type=F32)
                + bias_ref[0, pidx])
        for pr in range(pairs):
            pidx = g * pairs + pr
            rows = slice(pr * TQ, (pr + 1) * TQ)
            s = s_scr[rows, :]
            sink_a = sink_ref[2 * pidx]
            sink_b = sink_ref[2 * pidx + 1]
            ma = jnp.maximum(jnp.max(s[:, :NK], axis=-1, keepdims=True), sink_a)
            mb = jnp.maximum(jnp.max(s[:, NK:], axis=-1, keepdims=True), sink_b)
            p_scr[rows, :] = jnp.exp(s - jnp.where(first_s, ma, mb)).astype(BF16)
            t_scr[rows, :] = jnp.where(first_o, jnp.exp(sink_a - ma), jnp.exp(sink_b - mb))
        for pr in range(pairs):
            pidx = g * pairs + pr
            rows = slice(pr * TQ, (pr + 1) * TQ)
            ov = jnp.dot(p_scr[rows, :], vv, preferred_element_type=F32)
            o = ov[:, :PW] / (ov[:, PW:] + t_scr[rows, :])
            o_ref[:, pidx * PW:(pidx + 1) * PW] = o.astype(o_ref.dtype)


def _pair_bias(bias_prev, bias_cur):
    full = jnp.concatenate([bias_prev, bias_cur], axis=-1)
    H, Q, NK = full.shape
    return jnp.transpose(full.reshape(H // 2, 2, Q, NK), (0, 2, 1, 3)).reshape(H // 2, Q, 2 * NK)


def _swa(q, k_prev_arr, v_prev_arr, prev_map, k_cur_arr, v_cur_arr, cur_map, n_cur, bias, sinks,
         B, L, TQ):
    T = B * L
    nblk = L // TQ
    kvw = SWA_N_KV * SWA_HEAD_DIM
    n_var = bias.shape[0]
    n_keys = bias.shape[3] // 2
    n_prev = n_keys - n_cur
    pairs = SWA_REP // 2
    return pl.pallas_call(
        _swa_kernel,
        grid=(B, nblk),
        in_specs=[pl.BlockSpec((TQ, SWA_N_HEADS * SWA_HEAD_DIM), lambda b, i: (b * nblk + i, 0)),
                  pl.BlockSpec((n_prev, kvw), prev_map[0]),
                  pl.BlockSpec((n_prev, kvw), prev_map[1]),
                  pl.BlockSpec((n_cur, kvw), cur_map[0]),
                  pl.BlockSpec((n_cur, kvw), cur_map[1]),
                  pl.BlockSpec((1,) + bias.shape[1:], lambda b, i: (jnp.where(i == 0, n_var - 1, 0), 0, 0, 0)),
                  pl.BlockSpec(memory_space=pltpu.SMEM)],
        out_specs=pl.BlockSpec((TQ, SWA_N_HEADS * SWA_HEAD_DIM), lambda b, i: (b * nblk + i, 0)),
        out_shape=jax.ShapeDtypeStruct((T, SWA_N_HEADS * SWA_HEAD_DIM), BF16),
        scratch_shapes=[pltpu.VMEM((pairs * TQ, 2 * n_keys), F32),
                        pltpu.VMEM((pairs * TQ, 2 * n_keys), BF16),
                        pltpu.VMEM((pairs * TQ, 2 * SWA_HEAD_DIM), F32)],
        compiler_params=_params(("parallel", "arbitrary")),
        name="swa_attention",
    )(q, k_prev_arr, v_prev_arr, k_cur_arr, v_cur_arr, bias, sinks.astype(F32))


def _mem_kernel(q_ref, mk_ref, mv_ref, o_ref):
    scale = MEM_HEAD_DIM ** -0.5
    ndims = (((1,), (1,)), ((), ()))
    for h in range(MEM_N_HEADS):
        hs = slice(h * MEM_HEAD_DIM, (h + 1) * MEM_HEAD_DIM)
        s = lax.dot_general(q_ref[:, hs], mk_ref[:, hs], ndims, preferred_element_type=F32) * scale
        m = jnp.max(s, axis=-1, keepdims=True)
        p = jnp.exp(s - m)
        den = jnp.sum(p, axis=-1, keepdims=True)
        o = jnp.dot(p.astype(BF16), mv_ref[:, hs], preferred_element_type=F32)
        o_ref[:, hs] = (o / den).astype(o_ref.dtype)


def _mem_attend(q, mk, mv, L, tm):
    T, W = q.shape
    M = mk.shape[0] // (T // L)
    return pl.pallas_call(
        _mem_kernel,
        grid=(T // tm,),
        in_specs=[pl.BlockSpec((tm, W), lambda i: (i, 0)),
                  pl.BlockSpec((M, W), lambda i: ((i * tm) // L, 0)),
                  pl.BlockSpec((M, W), lambda i: ((i * tm) // L, 0))],
        out_specs=pl.BlockSpec((tm, W), lambda i: (i, 0)),
        out_shape=jax.ShapeDtypeStruct((T, W), BF16),
        compiler_params=_params(("parallel",)),
        name="mem_attention",
    )(q, mk, mv)


def _merge_kernel(ys_ref, os_ref, om_ref, w1_ref, w2_ref, w3_ref, g0_ref, g1_ref, g2_ref, o_ref):
    a = jnp.dot(ys_ref[...], w1_ref[...], preferred_element_type=F32)
    b = jnp.dot(os_ref[...], w2_ref[...], preferred_element_type=F32)
    c = jnp.dot(om_ref[...], w3_ref[...], preferred_element_type=F32)
    o = (g0_ref[...].astype(F32) * a + g1_ref[...].astype(F32) * b + g2_ref[...].astype(F32) * c)
    o_ref[...] = o.astype(o_ref.dtype)


def _merge(y_ssd, o_s, o_m, w1, w2, w3, gates, tm, tn):
    T = y_ssd.shape[0]
    D = w1.shape[1]
    nj = D // tn
    row = lambda i, j: (i, 0)
    colw = lambda i, j: (0, j)
    return pl.pallas_call(
        _merge_kernel,
        grid=(T // tm, nj),
        in_specs=[pl.BlockSpec((tm, y_ssd.shape[1]), row),
                  pl.BlockSpec((tm, o_s.shape[1]), row),
                  pl.BlockSpec((tm, o_m.shape[1]), row),
                  pl.BlockSpec((w1.shape[0], tn), colw),
                  pl.BlockSpec((w2.shape[0], tn), colw),
                  pl.BlockSpec((w3.shape[0], tn), colw),
                  pl.BlockSpec((tm, tn), lambda i, j: (i, j)),
                  pl.BlockSpec((tm, tn), lambda i, j: (i, j + nj)),
                  pl.BlockSpec((tm, tn), lambda i, j: (i, j + 2 * nj))],
        out_specs=pl.BlockSpec((tm, tn), lambda i, j: (i, j)),
        out_shape=jax.ShapeDtypeStruct((T, D), BF16),
        compiler_params=_params(("parallel", "arbitrary")),
        name="gated_merge",
    )(y_ssd, o_s, o_m, w1, w2, w3, gates, gates, gates)


def _norm_route_kernel(xa_ref, xb_ref, g_ref, whi_ref, wlo_ref, h_ref, r_ref, *, n_a):
    x = jnp.where(pl.program_id(0) < n_a, xa_ref[...], xb_ref[...])
    ms = jnp.mean(x * x, axis=-1, keepdims=True)
    h = x * lax.rsqrt(ms + EPS) * g_ref[...]
    h_ref[:, 0, :] = h
    hb = h.astype(BF16)
    lo = (h - hb.astype(F32)).astype(BF16)
    lg = (jnp.dot(hb, whi_ref[...], preferred_element_type=F32)
          + jnp.dot(lo, whi_ref[...], preferred_element_type=F32)
          + jnp.dot(hb, wlo_ref[...], preferred_element_type=F32))
    lane = lax.broadcasted_iota(jnp.int32, lg.shape, 1)
    lane_f = lane.astype(F32)
    far = float(LANES)
    gl = jnp.where((lane >= N_EXPERTS) & (lane < N_EXPERTS + N_EXPERT_GROUPS), lg, NEG_INF)
    gmax = jnp.max(gl, axis=-1, keepdims=True)
    gidx = jnp.min(jnp.where(gl == gmax, lane_f - N_EXPERTS, far), axis=-1, keepdims=True)
    gw = 1.0 / jnp.sum(jnp.exp(gl - gmax), axis=-1, keepdims=True)
    lo_e = gidx * EXPERTS_PER_GROUP
    el = jnp.where((lane_f >= lo_e) & (lane_f < lo_e + EXPERTS_PER_GROUP), lg, NEG_INF)
    v1 = jnp.max(el, axis=-1, keepdims=True)
    i1 = jnp.min(jnp.where(el == v1, lane_f, far), axis=-1, keepdims=True)
    el2 = jnp.where(lane_f == i1, NEG_INF, el)
    v2 = jnp.max(el2, axis=-1, keepdims=True)
    i2 = jnp.min(jnp.where(el2 == v2, lane_f, far), axis=-1, keepdims=True)
    e = jnp.exp(v2 - v1)
    w1 = gw / (1.0 + e)
    w2 = gw * e / (1.0 + e)
    r_ref[...] = jnp.where(lane == 0, w1, jnp.where(lane == 1, w2,
                           jnp.where(lane == 2, i1, jnp.where(lane == 3, i2, 0.0))))


def _norm_route(xa, xb, g, w_hi, w_lo, tm):
    Ta, D = xa.shape
    Tb = xb.shape[0]
    T = Ta + Tb
    n_a, n_b = Ta // tm, Tb // tm
    assert n_a * tm == Ta and n_b * tm == Tb
    return pl.pallas_call(
        functools.partial(_norm_route_kernel, n_a=n_a),
        grid=(n_a + n_b,),
        in_specs=[pl.BlockSpec((tm, D), lambda i: (jnp.minimum(i, n_a - 1), 0)),
                  pl.BlockSpec((tm, D), lambda i: (jnp.maximum(i - n_a, 0), 0)),
                  pl.BlockSpec((1, D), lambda i: (0, 0)),
                  pl.BlockSpec((D, LANES), lambda i: (0, 0)),
                  pl.BlockSpec((D, LANES), lambda i: (0, 0))],
        out_specs=[pl.BlockSpec((tm, 1, D), lambda i: (i, 0, 0)),
                   pl.BlockSpec((tm, LANES), lambda i: (i, 0))],
        out_shape=[jax.ShapeDtypeStruct((T, 1, D), F32),
                   jax.ShapeDtypeStruct((T, LANES), F32)],
        compiler_params=_params(("parallel",)),
        name="ffn_norm_router",
    )(xa, xb, g.reshape(1, D).astype(F32), w_hi, w_lo)


def _moe_kernel(be_ref, nv_ref, s0_ref, ord_ref, h_hbm, w1_ref, w3_ref, w2_ref, yu_hbm,
                xbuf, ybuf, w1b, w3b, w2b, sem_in, sem_out, *, n_tok):
    i = pl.program_id(0)
    nb = pl.num_programs(0)
    rows = xbuf.shape[1]
    n_assign = TOP_K * n_tok
    nv = nv_ref[i]
    slot = i % 2
    nxt = jnp.minimum(i + 1, nb - 1)
    next_valid = (i + 1 < nb) & (nv_ref[nxt] > 0)

    def assignment(blk, r):
        return ord_ref[s0_ref[blk] + r]

    def row_in(s, r, tok):
        return pltpu.make_async_copy(h_hbm.at[tok], xbuf.at[s, pl.ds(r, 1), :], sem_in.at[s])

    def row_out(s, r, dst):
        return pltpu.make_async_copy(ybuf.at[s, pl.ds(r, 1), :], yu_hbm.at[dst], sem_out.at[s])

    def gather_start(blk, s):
        for r in range(rows):
            row_in(s, r, lax.shift_right_logical(assignment(blk, r), 1)).start(priority=r % 2)

    def gather_wait(s):
        for r in range(rows):
            row_in(s, r, 0).wait()

    def scatter_start(blk, s):
        n_valid = nv_ref[blk]
        for r in range(rows):
            m = assignment(blk, r)
            dst = jnp.where(r < n_valid, (m & 1) * n_tok + lax.shift_right_logical(m, 1),
                            n_assign + s * rows + r)
            row_out(s, r, dst).start(priority=r % 2)

    def scatter_wait(s):
        for r in range(rows):
            row_out(s, r, 0).wait()

    @pl.when(i == 0)
    def _clear_spare_rows():
        ybuf[...] = jnp.zeros(ybuf.shape, ybuf.dtype)
        for s in range(2):
            for r in range(rows):
                row_out(s, r, n_assign + s * rows + r).start()
            scatter_wait(s)

    @pl.when((i == 0) & (nv > 0))
    def _prologue():
        gather_start(i, 0)

    @pl.when(next_valid)
    def _prefetch():
        gather_start(nxt, 1 - slot)

    prev = be_ref[jnp.maximum(i - 1, 0)]

    @pl.when((i == 0) | (be_ref[i] != prev))
    def _load_expert():
        w1b[...] = w1_ref[0].astype(BF16)
        w3b[...] = w3_ref[0].astype(BF16)
        w2b[...] = w2_ref[0].astype(BF16)

    @pl.when(nv > 0)
    def _compute():
        gather_wait(slot)

        @pl.when(i >= 2)
        def _free_ybuf():
            scatter_wait(slot)

        x = xbuf[slot].astype(BF16)
        a = jnp.dot(x, w1b[...], preferred_element_type=F32)
        b = jnp.dot(x, w3b[...], preferred_element_type=F32)
        mid = (_silu(a) * b).astype(BF16)
        y = jnp.dot(mid, w2b[...], preferred_element_type=F32)
        ybuf[slot] = y
        scatter_start(i, slot)

        @pl.when(jnp.logical_not(next_valid))
        def _drain():
            scatter_wait(slot)

            @pl.when(i >= 1)
            def _drain_prev():
                scatter_wait(1 - slot)


def _moe(h2, route, w_gate, w_up, w_down):
    T, _, D = h2.shape
    F = w_gate.shape[2]
    assert TOP_K == 2
    M = T * TOP_K
    BM = MOE_ROWS
    nb = (M + N_EXPERTS * (BM - 1) + BM - 1) // BM
    e_flat = route[:, TOP_K:2 * TOP_K].astype(jnp.int32).reshape(M)
    idx_bits = (M - 1).bit_length()
    assert (N_EXPERTS << idx_bits) < 2 ** 31
    packed = lax.sort(e_flat * (1 << idx_bits) + jnp.arange(M, dtype=jnp.int32), is_stable=False)
    order = jnp.pad(packed & ((1 << idx_bits) - 1), (0, BM))
    experts = jnp.arange(N_EXPERTS, dtype=jnp.int32)
    counts = jnp.sum((experts[:, None] == e_flat[None, :]).astype(jnp.int32), axis=1)
    padded = (counts + BM - 1) // BM * BM
    pad_end = jnp.cumsum(padded)
    pad_start = pad_end - padded
    start = jnp.cumsum(counts) - counts
    blk0 = jnp.arange(nb, dtype=jnp.int32) * BM
    blk_exp = jnp.minimum(jnp.sum((pad_end[None, :] <= blk0[:, None]).astype(jnp.int32), axis=1), N_EXPERTS - 1)
    pick = (blk_exp[:, None] == experts[None, :]).astype(jnp.int32)
    off0 = blk0 - jnp.sum(pick * pad_start[None, :], axis=1)
    blk_nv = jnp.clip(jnp.sum(pick * counts[None, :], axis=1) - off0, 0, BM).astype(jnp.int32)
    blk_s0 = jnp.clip(jnp.sum(pick * start[None, :], axis=1) + off0, 0, M - 1).astype(jnp.int32)

    grid_spec = pltpu.PrefetchScalarGridSpec(
        num_scalar_prefetch=4,
        grid=(nb,),
        in_specs=[pl.BlockSpec(memory_space=pl.ANY),
                  pl.BlockSpec((1, D, F), lambda i, be, nv, s0, od: (be[i], 0, 0)),
                  pl.BlockSpec((1, D, F), lambda i, be, nv, s0, od: (be[i], 0, 0)),
                  pl.BlockSpec((1, F, D), lambda i, be, nv, s0, od: (be[i], 0, 0))],
        out_specs=pl.BlockSpec(memory_space=pl.ANY),
        scratch_shapes=[pltpu.VMEM((2, BM, D), F32),
                        pltpu.VMEM((2, BM, D), F32),
                        pltpu.VMEM((D, F), BF16),
                        pltpu.VMEM((D, F), BF16),
                        pltpu.VMEM((F, D), BF16),
                        pltpu.SemaphoreType.DMA((2,)),
                        pltpu.SemaphoreType.DMA((2,))],
    )
    return pl.pallas_call(
        functools.partial(_moe_kernel, n_tok=T),
        grid_spec=grid_spec,
        out_shape=jax.ShapeDtypeStruct((M + 2 * BM, 1, D), F32),
        compiler_params=_params(("arbitrary",)),
        name="moe_experts",
    )(blk_exp.astype(jnp.int32), blk_nv, blk_s0, order, h2, w_gate, w_up, w_down)


def _combine_kernel(x_ref, r_ref, y0_ref, y1_ref, o_ref):
    w0 = r_ref[:, 0:1]
    w1 = r_ref[:, 1:2]
    o_ref[...] = x_ref[...] + (w0 * y0_ref[:, 0, :] + w1 * y1_ref[:, 0, :])


def _combine(x, route, yu, blk0, tm):
    T, D = x.shape
    n_all = route.shape[0] // tm
    return pl.pallas_call(
        _combine_kernel,
        grid=(T // tm,),
        in_specs=[pl.BlockSpec((tm, D), lambda i: (i, 0)),
                  pl.BlockSpec((tm, LANES), lambda i: (blk0 + i, 0)),
                  pl.BlockSpec((tm, 1, D), lambda i: (blk0 + i, 0, 0)),
                  pl.BlockSpec((tm, 1, D), lambda i: (n_all + blk0 + i, 0, 0))],
        out_specs=pl.BlockSpec((tm, D), lambda i: (i, 0)),
        out_shape=jax.ShapeDtypeStruct((T, D), F32),
        compiler_params=_params(("parallel",)),
        name="moe_combine",
    )(x, route, yu, yu)


def _layer(x, B, L, p, conv_prev8, h0_t, mk, mv, swa_prev, swa_bias, Lc, TQ):
    T, D = x.shape
    tmm = min(MM_ROWS, T)
    h = _rmsnorm(x, p["g_mix"], min(512, T))
    z = _matmul(h, p["wz"], BF16, tmm, MM_COLS_WIDE, name="proj_z")
    xbc = _matmul(h, p["wxbc"], BF16, tmm, MM_COLS_WIDE, name="proj_xbc")
    q_s = _matmul(h, p["wqs"], BF16, tmm, MM_COLS_WIDE, epi=_epi_group_norm, col_extras=(p["g_q_swa"],),
                  const_extras=(p["mavg"],), name="proj_q_swa")
    kvw = SWA_N_KV * SWA_HEAD_DIM
    kvd = _matmul(h, p["wkvdt"], F32, tmm, 2 * kvw + PAD_HEADS, epi=_epi_kv_dt, const_extras=(p["g_k"], p["mavg"]),
                  name="proj_kv_dt")
    kv = kvd
    q_m = _matmul(h, p["wqm"], BF16, tmm, MM_COLS_WIDE, epi=_epi_row_norm, col_extras=(p["g_q_mem"],),
                  name="proj_q_mem")
    gates = _matmul(h, p["wg"], BF16, tmm, MM_COLS_WIDE, epi=_epi_sigmoid, name="proj_gates")

    y_ssd, h_t, conv8 = _ssd(z, xbc, kvd, (2 * kvw) // PAD_HEADS, conv_prev8, h0_t, p, B, L, Lc)

    nblk = L // TQ
    if swa_prev is None:
        prev_map = (lambda b, i: (b * nblk + jnp.maximum(i - 1, 0), 0),
                    lambda b, i: (b * nblk + jnp.maximum(i - 1, 0), 1))
        cur_map = (lambda b, i: (b * nblk + i, 0), lambda b, i: (b * nblk + i, 1))
        o_s = _swa(q_s, kv, kv, prev_map, kv, kv, cur_map, TQ, swa_bias, p["sinks"], B, L, TQ)
    else:
        n_cur = swa_bias.shape[3] // 2 - swa_prev[0].shape[0] // B
        kv3 = jnp.pad(kv[:, :2 * kvw].reshape(B, L, 2 * kvw), ((0, 0), (0, n_cur - L), (0, 0)))
        k_new = kv3[:, :, :kvw].reshape(B * n_cur, kvw)
        v_new = kv3[:, :, kvw:].reshape(B * n_cur, kvw)
        per_b = (lambda b, i: (b, 0), lambda b, i: (b, 0))
        o_s = _swa(q_s, swa_prev[0], swa_prev[1], per_b, k_new, v_new, per_b, n_cur, swa_bias, p["sinks"],
                   B, L, TQ)

    o_m = _mem_attend(q_m, mk, mv, L, min(512, L))

    merged = _merge(y_ssd, o_s, o_m, p["w_o_ssd"], p["w_o_swa"], p["w_o_mem"], gates, tmm, MERGE_COLS)
    x1 = _matmul(merged, p["w_out"], F32, min(MM_ROWS // 2, T), MM_COLS_WIDE, epi=_epi_residual, tile_extras=(x,),
                 name="proj_out")

    return x1, conv8, h_t, kv


def _state_to_heads(h_t, B):
    return jnp.transpose(h_t.reshape(B, SSD_D_STATE, SSD_N_HEADS, SSD_HEAD_DIM), (0, 2, 3, 1))


def kernel(x_prompt, x_sample, cache_conv, state_ssd, cache_swa_k, cache_swa_v, cache_mem_k, cache_mem_v, mem_prompt, rel_bias_table, g_mix, w_in, conv_w, conv_b, dt_bias, a_log, d_skip, g_ssd, w_o_ssd, g_q_swa, g_k_swa, sinks, w_o_swa, g_mem, w_mem_k, w_mem_v, g_q_mem, g_k_mem, w_o_mem, w_out, g_ffn, w_router_grp, w_router_exp, w_exp_gate, w_exp_up, w_exp_down):
    B, S, D = x_prompt.shape
    Bd, Sd, _ = x_sample.shape
    depth = w_in.shape[0]
    assert depth == 1
    l = 0
    kvw = SWA_N_KV * SWA_HEAD_DIM
    qw = SWA_N_HEADS * SWA_HEAD_DIM
    mw = MEM_N_HEADS * MEM_HEAD_DIM

    sizes = (SSD_D_INNER, SSD_CONV_DIM, SSD_N_HEADS, qw, kvw, kvw, mw, 3 * D)
    offs = np.concatenate([[0], np.cumsum(sizes)])
    w = w_in[l]
    cols = [w[:, int(offs[k]):int(offs[k + 1])] for k in range(len(sizes))]
    pad_h = PAD_HEADS - SSD_N_HEADS
    w_r = jnp.pad(jnp.concatenate([w_router_exp[l], w_router_grp[l]], axis=1),
                  ((0, 0), (0, LANES - N_EXPERTS - N_EXPERT_GROUPS)))
    w_r_hi = w_r.astype(BF16)
    p = {
        "g_mix": g_mix[l],
        "wz": cols[0].astype(BF16),
        "wxbc": cols[1].astype(BF16),
        "wkvdt": jnp.concatenate([cols[4], cols[5], jnp.pad(cols[2], ((0, 0), (0, pad_h)))], axis=1).astype(BF16),
        "wqs": cols[3].astype(BF16),
        "wqm": cols[6].astype(BF16),
        "wg": cols[7].astype(BF16),
        "conv_w": conv_w[l].astype(F32),
        "conv_b": conv_b[l].reshape(1, SSD_CONV_DIM).astype(F32),
        "dt_bias": jnp.pad(dt_bias[l], (0, pad_h)).reshape(1, PAD_HEADS).astype(F32),
        "a_log": jnp.pad(a_log[l], (0, pad_h)).reshape(1, PAD_HEADS).astype(F32),
        "d_skip": jnp.repeat(d_skip[l], SSD_HEAD_DIM).reshape(1, SSD_D_INNER).astype(F32),
        "g_ssd": g_ssd[l].reshape(1, SSD_D_INNER).astype(F32),
        "g_q_swa": (jnp.tile(g_q_swa[l], SWA_N_HEADS) * SWA_HEAD_DIM ** -0.5).reshape(1, qw).astype(F32),
        "g_k": jnp.tile(g_k_swa[l], SWA_N_KV).reshape(1, kvw).astype(F32),
        "g_q_mem": jnp.tile(g_q_mem[l], MEM_N_HEADS).reshape(1, mw).astype(F32),
        "mavg": _group_mean_matrix(256, SWA_HEAD_DIM),
        "sinks": sinks[l],
        "w_o_ssd": w_o_ssd[l].astype(BF16),
        "w_o_swa": w_o_swa[l].astype(BF16),
        "w_o_mem": w_o_mem[l].astype(BF16),
        "w_out": w_out[l].astype(BF16),
        "g_ffn": g_ffn[l],
        "w_r_hi": w_r_hi,
        "w_r_lo": (w_r - w_r_hi.astype(F32)).astype(BF16),
        "w_exp_gate": w_exp_gate[l],
        "w_exp_up": w_exp_up[l],
        "w_exp_down": w_exp_down[l],
    }

    M = mem_prompt.shape[1]
    mn = _rmsnorm(mem_prompt.reshape(B * M, D), g_mem[l], min(256, B * M))
    mk_p = _matmul(mn, w_mem_k[l].astype(BF16), F32, min(256, B * M), MEM_HEAD_DIM, epi=_epi_row_norm,
                   col_extras=(jnp.tile(g_k_mem[l], MEM_N_HEADS).reshape(1, mw).astype(F32),), name="mem_k")
    mv_p = _matmul(mn, w_mem_v[l].astype(BF16), F32, min(256, B * M), MEM_HEAD_DIM, name="mem_v")

    TQ = 2 * CHUNK
    qpos = np.arange(TQ)
    prev_p = _bias_from_table(_bucket_map(qpos, np.arange(TQ) - TQ), rel_bias_table)
    cur_p = _bias_from_table(_bucket_map(qpos, np.arange(TQ)), rel_bias_table)
    bias_p = jnp.stack([_pair_bias(prev_p, cur_p), _pair_bias(jnp.full_like(prev_p, NEG_INF), cur_p)])
    C = cache_swa_k.shape[2]
    qpos_s = PAST_LEN + np.arange(Sd)
    cur_map_s = _bucket_map(qpos_s, PAST_LEN + np.arange(C))
    cur_map_s[:, Sd:] = -1
    bias_s = _pair_bias(_bias_from_table(_bucket_map(qpos_s, PAST_LEN - C + np.arange(C)), rel_bias_table),
                        _bias_from_table(cur_map_s, rel_bias_table))[None]

    conv0 = jnp.zeros((B, SUBLANES, SSD_CONV_DIM), F32)
    h0 = jnp.zeros((B, SSD_D_STATE, SSD_D_INNER), F32)
    x1_p, conv8_p, ht_p, kv_p = _layer(x_prompt.reshape(B * S, D), B, S, p, conv0, h0,
                                     mk_p.astype(BF16), mv_p.astype(BF16), None,
                                     bias_p, CHUNK, TQ)
    conv_prev = jnp.pad(cache_conv[l], ((0, 0), (SUBLANES - (SSD_CONV - 1), 0), (0, 0)))
    h0_s = jnp.transpose(state_ssd[l], (0, 3, 1, 2)).reshape(Bd, SSD_D_STATE, SSD_D_INNER)
    x1_s, conv8_s, ht_s, kv_s = _layer(x_sample.reshape(Bd * Sd, D), Bd, Sd, p, conv_prev, h0_s,
                                     cache_mem_k[l].reshape(Bd * M, mw).astype(BF16),
                                     cache_mem_v[l].reshape(Bd * M, mw).astype(BF16),
                                     (cache_swa_k[l].reshape(Bd * C, kvw), cache_swa_v[l].reshape(Bd * C, kvw)),
                                     bias_s, Sd, Sd)

    tr = math.gcd(B * S, Bd * Sd, MOE_ROWS)
    h2, route = _norm_route(x1_p, x1_s, p["g_ffn"], p["w_r_hi"], p["w_r_lo"], tr)
    yu = _moe(h2, route, p["w_exp_gate"], p["w_exp_up"], p["w_exp_down"])
    yp = _combine(x1_p, route, yu, 0, tr)
    ys = _combine(x1_s, route, yu, (B * S) // tr, tr)

    keep = min(WINDOW, S)
    kv_p = kv_p.reshape(B, S, -1)[:, S - keep:, :2 * kvw].reshape(B, keep, 2, SWA_N_KV, SWA_HEAD_DIM)
    kv_s = kv_s[:, :2 * kvw].reshape(Bd, Sd, 2, SWA_N_KV, SWA_HEAD_DIM)
    tail = SUBLANES - (SSD_CONV - 1)
    return (yp.reshape(B, S, D), ys.reshape(Bd, Sd, D),
            conv8_p[None, :, tail:], _state_to_heads(ht_p, B)[None],
            kv_p[None, :, :, 0], kv_p[None, :, :, 1],
            mk_p.reshape(1, B, M, MEM_N_HEADS, MEM_HEAD_DIM), mv_p.reshape(1, B, M, MEM_N_HEADS, MEM_HEAD_DIM),
            conv8_s[None, :, tail:], _state_to_heads(ht_s, Bd)[None],
            kv_s[None, :, :, 0], kv_s[None, :, :, 1])
```
